```python
import jax, jax.numpy as jnp
from jax import lax
import numpy as np


D_MODEL = 1024
BATCH = 8
SEQ = 2048
DEPTH = 1

MOBA_WIDTH = D_MODEL // 2
MOBA_HEAD_DIM = 64
MOBA_HEADS = MOBA_WIDTH // MOBA_HEAD_DIM
MOBA_BLOCK = 256
MOBA_TOPK = 3
MOBA_QUERY_CHUNK = 32

GLA_HEADS = 4
GLA_V_WIDTH = D_MODEL - MOBA_WIDTH
GLA_VALUE_DIM = GLA_V_WIDTH // GLA_HEADS
GLA_KEY_DIM = GLA_VALUE_DIM // 2
GLA_K_WIDTH = GLA_HEADS * GLA_KEY_DIM
GLA_GATE_RANK = 16
GLA_GATE_TAU = 16.0
GLA_CHUNK = 64

MIX_WIDTH = MOBA_WIDTH + GLA_V_WIDTH
D_FF = 2816
RMS_EPS = 1e-6

IN_PARTS = (MOBA_WIDTH, MOBA_WIDTH, MOBA_WIDTH,
            GLA_K_WIDTH, GLA_K_WIDTH, GLA_V_WIDTH,
            GLA_GATE_RANK, GLA_V_WIDTH)
IN_WIDTH = sum(IN_PARTS)
IN_SPLITS = tuple(int(v) for v in np.cumsum(IN_PARTS)[:-1])

kernel_name = 'hybrid_moba_gla_macaron'


def rms_norm(x, g):
    xf = x.astype(jnp.float32)
    y = xf * lax.rsqrt(jnp.mean(xf * xf, axis=-1, keepdims=True) + RMS_EPS)
    return (y * g.astype(jnp.float32)).astype(x.dtype)


def swiglu(x, w_gate, w_up, w_down):
    return (jax.nn.silu(x @ w_gate) * (x @ w_up)) @ w_down


def split_heads(t, n_heads):
    b, s, _ = t.shape
    return t.reshape(b, s, n_heads, -1).transpose(0, 2, 1, 3)


def moba_attention(q, k, v):
    B, H, S, dh = q.shape
    n_blk = -(-S // MOBA_BLOCK)
    Lp = n_blk * MOBA_BLOCK
    pad = ((0, 0), (0, 0), (0, Lp - S), (0, 0))
    q, k, v = jnp.pad(q, pad), jnp.pad(k, pad), jnp.pad(v, pad)
    kb = k.reshape(B, H, n_blk, MOBA_BLOCK, dh)
    vb = v.reshape(B, H, n_blk, MOBA_BLOCK, dh)
    k_mean = jnp.mean(kb.astype(jnp.float32), axis=3).astype(k.dtype)
    n_sel = min(MOBA_TOPK, n_blk)
    scale = dh ** -0.5
    bi = jnp.arange(B)[:, None, None, None]
    hi = jnp.arange(H)[None, :, None, None]
    blk_ids = jnp.arange(n_blk)
    key_off = jnp.arange(MOBA_BLOCK)
    q_off = jnp.arange(MOBA_QUERY_CHUNK)
    QC = MOBA_QUERY_CHUNK

    def chunk(c):
        start = c * QC
        blk = start // MOBA_BLOCK
        qc = lax.dynamic_slice_in_dim(q, start, QC, axis=2)
        gate = jnp.einsum('bhqd,bhnd->bhqn', qc, k_mean).astype(jnp.float32)
        gate = jnp.where(blk_ids < blk, gate, -jnp.inf)
        _, sel = lax.top_k(gate, n_sel)
        valid = sel < blk
        k_sel = kb[bi, hi, sel]
        v_sel = vb[bi, hi, sel]
        s_sel = jnp.einsum('bhqd,bhqnld->bhqnl', qc, k_sel).astype(jnp.float32) * scale
        s_sel = jnp.where(valid[..., None], s_sel, -jnp.inf).reshape(B, H, QC, n_sel * MOBA_BLOCK)
        k_own = lax.dynamic_index_in_dim(kb, blk, axis=2, keepdims=False)
        v_own = lax.dynamic_index_in_dim(vb, blk, axis=2, keepdims=False)
        s_own = jnp.einsum('bhqd,bhld->bhql', qc, k_own).astype(jnp.float32) * scale
        causal = (blk * MOBA_BLOCK + key_off)[None, :] <= (start + q_off)[:, None]
        s_own = jnp.where(causal, s_own, -jnp.inf)
        p = jax.nn.softmax(jnp.concatenate([s_own, s_sel], axis=-1), axis=-1).astype(v.dtype)
        p_own = p[..., :MOBA_BLOCK]
        p_sel = p[..., MOBA_BLOCK:].reshape(B, H, QC, n_sel, MOBA_BLOCK)
        return (jnp.einsum('bhql,bhld->bhqd', p_own, v_own)
                + jnp.einsum('bhqnl,bhqnld->bhqd', p_sel, v_sel))

    outs = lax.map(chunk, jnp.arange(Lp // QC))
    out = jnp.moveaxis(outs, 0, 2).reshape(B, H, Lp, dh)
    return out[:, :, :S]


def gla_attention(q, k, v, log_alpha):
    B, H, S, dk = q.shape
    dv = v.shape[-1]
    C = GLA_CHUNK
    n_c = S // C

    def to_chunks(t):
        return jnp.moveaxis(t.reshape(B, H, n_c, C, t.shape[-1]), 2, 0)

    q = q * (dk ** -0.5)
    causal = jnp.tril(jnp.ones((C, C), dtype=bool))[:, :, None]

    def step(state, inp):
        qc, kc, vc, gc = inp
        qf, kf, vf = qc.astype(jnp.float32), kc.astype(jnp.float32), vc.astype(jnp.float32)
        b = jnp.cumsum(gc.astype(jnp.float32), axis=2)
        o_inter = jnp.einsum('bhtd,bhde->bhte', qf * jnp.exp(b), state)
        diff = b[:, :, :, None, :] - b[:, :, None, :, :]
        decay = jnp.exp(jnp.where(causal, diff, -jnp.inf))
        a = jnp.einsum('bhtd,bhsd,bhtsd->bhts', qf, kf, decay)
        o_intra = jnp.einsum('bhts,bhse->bhte', a, vf)
        b_last = b[:, :, -1:, :]
        state = (jnp.exp(b_last[:, :, 0, :])[..., None] * state
                 + jnp.einsum('bhsd,bhse->bhde', kf * jnp.exp(b_last - b), vf))
        return state, (o_inter + o_intra).astype(v.dtype)

    state0 = jnp.zeros((B, H, dk, dv), jnp.float32)
    _, o = lax.scan(step, state0, (to_chunks(q), to_chunks(k), to_chunks(v), to_chunks(log_alpha)))
    return jnp.moveaxis(o, 0, 2).reshape(B, H, S, dv)


def setup_inputs(seed: int = 0) -> dict:
    key = jax.random.key(seed)
    ks = jax.random.split(key, 20)
    f32 = jnp.float32

    def w(k, shape, fan_in):
        return jax.random.normal(k, shape, f32) * (fan_in ** -0.5)

    def gain(k, shape):
        return 1.0 + 0.02 * jax.random.normal(k, shape, f32)

    L = DEPTH
    return {
        'x': jax.random.normal(ks[0], (BATCH, SEQ, D_MODEL), f32),
        'ffn1_pre_g': gain(ks[1], (L, D_MODEL)),
        'ffn1_w_gate': w(ks[2], (L, D_MODEL, D_FF), D_MODEL),
        'ffn1_w_up': w(ks[3], (L, D_MODEL, D_FF), D_MODEL),
        'ffn1_w_down': w(ks[4], (L, D_FF, D_MODEL), D_FF),
        'ffn1_post_g': gain(ks[5], (L, D_MODEL)),
        'mix_pre_g': gain(ks[6], (L, D_MODEL)),
        'w_in': w(ks[7], (L, D_MODEL, IN_WIDTH), D_MODEL),
        'gla_w_alpha_up': w(ks[8], (L, GLA_GATE_RANK, GLA_K_WIDTH), GLA_GATE_RANK),
        'gla_b_alpha': 0.01 * jax.random.normal(ks[9], (L, GLA_K_WIDTH), f32),
        'moba_out_g': gain(ks[10], (L, MOBA_WIDTH)),
        'gla_out_g': gain(ks[11], (L, GLA_HEADS, GLA_VALUE_DIM)),
        'w_out': w(ks[12], (L, MIX_WIDTH, D_MODEL), MIX_WIDTH),
        'mix_post_g': gain(ks[13], (L, D_MODEL)),
        'ffn2_pre_g': gain(ks[14], (L, D_MODEL)),
        'ffn2_w_gate': w(ks[15], (L, D_MODEL, D_FF), D_MODEL),
        'ffn2_w_up': w(ks[16], (L, D_MODEL, D_FF), D_MODEL),
        'ffn2_w_down': w(ks[17], (L, D_FF, D_MODEL), D_FF),
        'ffn2_post_g': gain(ks[18], (L, D_MODEL)),
    }


def reference(x, ffn1_pre_g, ffn1_w_gate, ffn1_w_up, ffn1_w_down, ffn1_post_g,
              mix_pre_g, w_in, gla_w_alpha_up, gla_b_alpha, moba_out_g, gla_out_g,
              w_out, mix_post_g, ffn2_pre_g, ffn2_w_gate, ffn2_w_up, ffn2_w_down,
              ffn2_post_g):
    B, S, _ = x.shape
    for l in range(DEPTH):
        f = swiglu(rms_norm(x, ffn1_pre_g[l]), ffn1_w_gate[l], ffn1_w_up[l], ffn1_w_down[l])
        x = x + 0.5 * rms_norm(f, ffn1_post_g[l])

        h = rms_norm(x, mix_pre_g[l])
        proj = h @ w_in[l]
        mq, mk, mv, gq, gk, gv, ga, gr = jnp.split(proj, IN_SPLITS, axis=-1)

        o_moba = moba_attention(split_heads(mq, MOBA_HEADS), split_heads(mk, MOBA_HEADS),
                                split_heads(mv, MOBA_HEADS))
        o_moba = o_moba.transpose(0, 2, 1, 3).reshape(B, S, MOBA_WIDTH)
        o_moba = rms_norm(o_moba, moba_out_g[l])

        z = (ga @ gla_w_alpha_up[l] + gla_b_alpha[l]).astype(jnp.float32)
        log_alpha = jax.nn.log_sigmoid(z) / GLA_GATE_TAU
        o_gla = gla_attention(split_heads(gq, GLA_HEADS), split_heads(gk, GLA_HEADS),
                              split_heads(gv, GLA_HEADS), split_heads(log_alpha, GLA_HEADS))
        o_gla = rms_norm(o_gla.transpose(0, 2, 1, 3), gla_out_g[l])
        o_gla = o_gla.reshape(B, S, GLA_V_WIDTH) * jax.nn.silu(gr)

        mix = jnp.concatenate([o_moba, o_gla], axis=-1) @ w_out[l]
        x = x + rms_norm(mix, mix_post_g[l])

        f = swiglu(rms_norm(x, ffn2_pre_g[l]), ffn2_w_gate[l], ffn2_w_up[l], ffn2_w_down[l])
        x = x + 0.5 * rms_norm(f, ffn2_post_g[l])
    return x
```

```python
import functools

import jax
import jax.numpy as jnp
from jax import lax
from jax.experimental import pallas as pl
from jax.experimental.pallas import tpu as pltpu

D_MODEL = 1024
D_FF = 2816
RMS_EPS = 1e-6

MOBA_WIDTH = 512
MOBA_HEAD_DIM = 64
MOBA_BLOCK = 256
MOBA_TOPK = 3

GLA_HEADS = 4
GLA_KEY_DIM = 64
GLA_VALUE_DIM = 128
GLA_K_WIDTH = GLA_HEADS * GLA_KEY_DIM
GLA_V_WIDTH = GLA_HEADS * GLA_VALUE_DIM
GLA_GATE_RANK = 16
GLA_GATE_TAU = 16.0
GLA_TILE = 256

LANES = 128
GATE_PAD = LANES
NEG_BIG = -1e30

FFN_TM = 512
FFN_TF = 256
PROJ_TM = 512
OUT_TM = 512
VMEM_LIMIT = 56 * 1024 * 1024

_NT = (((1,), (1,)), ((), ()))


def _rms(x, g):
    return x * lax.rsqrt(jnp.mean(x * x, axis=-1, keepdims=True) + RMS_EPS) * g


def _silu(x):
    return x / (1.0 + jnp.exp(-x))


def _resident(shape):
    return pl.BlockSpec(shape, lambda *_: (0,) * len(shape), pipeline_mode=pl.Buffered(1))


def _ffn_kernel(x_ref, pre_ref, wg_ref, wu_ref, wd_ref, post_ref, o_ref, h_ref):
    x = x_ref[...]
    xn = _rms(x, pre_ref[...]).astype(jnp.bfloat16)
    for c in range(D_FF // FFN_TF):
        cols = slice(c * FFN_TF, (c + 1) * FFN_TF)
        g = jnp.dot(xn, wg_ref[:, cols], preferred_element_type=jnp.float32)
        u = jnp.dot(xn, wu_ref[:, cols], preferred_element_type=jnp.float32)
        h_ref[:, cols] = (_silu(g) * u).astype(jnp.bfloat16)
    f = jnp.dot(h_ref[...], wd_ref[...], preferred_element_type=jnp.float32)
    o_ref[...] = x + 0.5 * _rms(f, post_ref[...])


def _ffn(x, pre_g, wg, wu, wd, post_g):
    t = x.shape[0]
    row = pl.BlockSpec((FFN_TM, D_MODEL), lambda i: (i, 0))
    return pl.pallas_call(
        _ffn_kernel,
        grid=(t // FFN_TM,),
        in_specs=[row, _resident((1, D_MODEL)), _resident((D_MODEL, D_FF)),
                  _resident((D_MODEL, D_FF)), _resident((D_FF, D_MODEL)),
                  _resident((1, D_MODEL))],
        out_specs=row,
        out_shape=jax.ShapeDtypeStruct((t, D_MODEL), jnp.float32),
        scratch_shapes=[pltpu.VMEM((FFN_TM, D_FF), jnp.bfloat16)],
        compiler_params=pltpu.CompilerParams(
            dimension_semantics=("arbitrary",), vmem_limit_bytes=VMEM_LIMIT),
        name="ffn",
    )(x, pre_g, wg, wu, wd, post_g)


def _proj_kernel(x_ref, g_ref, wqkv_ref, wgla_ref, wa_ref, wup_ref, ba_ref,
                 mq_ref, mk_ref, mv_ref, kmean_ref, gq_ref, gk_ref, gv_ref, gr_ref, la_ref):
    h = _rms(x_ref[...], g_ref[...]).astype(jnp.bfloat16)

    def proj(w_ref, lo, hi):
        return jnp.dot(h, w_ref[:, lo:hi], preferred_element_type=jnp.float32)

    w = MOBA_WIDTH
    mq_ref[...] = proj(wqkv_ref, 0, w).astype(jnp.bfloat16)
    mk = proj(wqkv_ref, w, 2 * w)
    mk_ref[...] = mk.astype(jnp.bfloat16)
    for blk in range(PROJ_TM // MOBA_BLOCK):
        rows = slice(blk * MOBA_BLOCK, (blk + 1) * MOBA_BLOCK)
        kmean_ref[blk] = jnp.mean(mk[rows], axis=0, keepdims=True)
    mv_ref[...] = proj(wqkv_ref, 2 * w, 3 * w).astype(jnp.bfloat16)

    kw, vw = GLA_K_WIDTH, GLA_V_WIDTH
    gq_ref[...] = proj(wgla_ref, 0, kw).astype(jnp.bfloat16)
    gk_ref[...] = proj(wgla_ref, kw, 2 * kw).astype(jnp.bfloat16)
    gv_ref[...] = proj(wgla_ref, 2 * kw, 2 * kw + vw).astype(jnp.bfloat16)
    gr_ref[...] = proj(wgla_ref, 2 * kw + vw, 2 * kw + 2 * vw).astype(jnp.bfloat16)

    ga = jnp.dot(h, wa_ref[...], preferred_element_type=jnp.float32)
    z = jnp.dot(ga.astype(jnp.bfloat16), wup_ref[...],
                preferred_element_type=jnp.float32) + ba_ref[...]
    log_sig = jnp.minimum(z, 0.0) - jnp.log(1.0 + jnp.exp(-jnp.abs(z)))
    la_ref[...] = log_sig * (1.0 / GLA_GATE_TAU)


def _proj(x, g, wqkv, wgla, wa, wup, ba):
    t = x.shape[0]
    nblk = PROJ_TM // MOBA_BLOCK

    def row(width):
        return pl.BlockSpec((PROJ_TM, width), lambda i: (i, 0))

    def out(width, dtype=jnp.bfloat16):
        return jax.ShapeDtypeStruct((t, width), dtype)

    return pl.pallas_call(
        _proj_kernel,
        grid=(t // PROJ_TM,),
        in_specs=[row(D_MODEL), _resident((1, D_MODEL)), _resident(wqkv.shape),
                  _resident(wgla.shape), _resident(wa.shape), _resident(wup.shape),
                  _resident(ba.shape)],
        out_specs=[row(MOBA_WIDTH), row(MOBA_WIDTH), row(MOBA_WIDTH),
                   pl.BlockSpec((nblk, 1, MOBA_WIDTH), lambda i: (i, 0, 0)),
                   row(GLA_K_WIDTH), row(GLA_K_WIDTH), row(GLA_V_WIDTH), row(GLA_V_WIDTH),
                   row(GLA_K_WIDTH)],
        out_shape=[out(MOBA_WIDTH), out(MOBA_WIDTH), out(MOBA_WIDTH),
                   jax.ShapeDtypeStruct((t // MOBA_BLOCK, 1, MOBA_WIDTH), jnp.float32),
                   out(GLA_K_WIDTH), out(GLA_K_WIDTH), out(GLA_V_WIDTH), out(GLA_V_WIDTH),
                   out(GLA_K_WIDTH, jnp.float32)],
        compiler_params=pltpu.CompilerParams(
            dimension_semantics=("arbitrary",), vmem_limit_bytes=VMEM_LIMIT),
        name="proj",
    )(x, g, wqkv, wgla, wa, wup, ba)


def _moba_kernel(q_ref, k_ref, v_ref, kmean_ref, o_ref, *, n_blk):
    blk = MOBA_BLOCK
    dh = MOBA_HEAD_DIM
    scale = dh ** -0.5
    q_t = q_ref[...].astype(jnp.float32).T
    v_t = v_ref[...].astype(jnp.float32).T.astype(jnp.bfloat16)
    kmean = kmean_ref[0]
    km_hi = kmean.astype(jnp.bfloat16)
    km_lo = (kmean - km_hi.astype(jnp.float32)).astype(jnp.bfloat16)

    feat = lax.broadcasted_iota(jnp.int32, q_t.shape, 0)
    blk_row = lax.broadcasted_iota(jnp.int32, (n_blk, blk), 0)
    key_i = lax.broadcasted_iota(jnp.int32, (blk, blk), 0)
    qry_i = lax.broadcasted_iota(jnp.int32, (blk, blk), 1)
    causal = key_i <= qry_i

    out_t = [None] * n_blk
    for head in range(LANES // dh):
        own = (feat >= head * dh) & (feat < (head + 1) * dh)
        qh_t = jnp.where(own, q_t, 0.0).astype(jnp.bfloat16)
        for i in range(n_blk):
            qi_t = qh_t[:, i * blk:(i + 1) * blk]
            n_keys = (i + 1) * blk
            s = jnp.dot(k_ref[0:n_keys, :], qi_t, preferred_element_type=jnp.float32) * scale
            parts = []
            if i > 0:
                if i > MOBA_TOPK:
                    gate = (jnp.dot(km_hi, qi_t, preferred_element_type=jnp.float32)
                            + jnp.dot(km_lo, qi_t, preferred_element_type=jnp.float32))
                    rank = jnp.zeros((n_blk, blk), jnp.float32)
                    for j in range(i):
                        gj = gate[j:j + 1, :]
                        beats = (gj > gate) | ((gj == gate) & (blk_row > j))
                        rank = rank + jnp.where(beats, 1.0, 0.0)
                    chosen = rank < float(MOBA_TOPK)
                for j in range(i):
                    sj = s[j * blk:(j + 1) * blk, :]
                    if i > MOBA_TOPK:
                        sj = jnp.where(chosen[j:j + 1, :], sj, NEG_BIG)
                    parts.append(sj)
            parts.append(jnp.where(causal, s[i * blk:n_keys, :], NEG_BIG))
            sm = parts[0] if len(parts) == 1 else jnp.concatenate(parts, axis=0)
            m = jnp.max(sm, axis=0, keepdims=True)
            p = jnp.exp(sm - m)
            inv = 1.0 / jnp.sum(p, axis=0, keepdims=True)
            pn = (p * inv).astype(jnp.bfloat16)
            o_t = jnp.dot(v_t[:, 0:n_keys], pn, preferred_element_type=jnp.float32)
            rows = slice(head * dh, (head + 1) * dh)
            if head == 0:
                out_t[i] = o_t[rows, :]
            else:
                out_t[i] = jnp.concatenate([out_t[i], o_t[rows, :]], axis=0)
    for i in range(n_blk):
        o_ref[i * blk:(i + 1) * blk, :] = out_t[i].T.astype(o_ref.dtype)


def _moba(mq, mk, mv, kmean, batch, seq):
    n_blk = seq // MOBA_BLOCK
    n_col = MOBA_WIDTH // LANES
    tok = pl.BlockSpec((seq, LANES), lambda b, c: (b, c))
    return pl.pallas_call(
        functools.partial(_moba_kernel, n_blk=n_blk),
        grid=(batch, n_col),
        in_specs=[tok, tok, tok, pl.BlockSpec((1, n_blk, LANES), lambda b, c: (b, 0, c))],
        out_specs=tok,
        out_shape=jax.ShapeDtypeStruct((batch * seq, MOBA_WIDTH), jnp.bfloat16),
        compiler_params=pltpu.CompilerParams(
            dimension_semantics=("arbitrary", "arbitrary"), vmem_limit_bytes=VMEM_LIMIT),
        name="moba",
    )(mq, mk, mv, kmean)


def _gla_kernel(q_ref, k_ref, v_ref, la_ref, o_ref, st_ref):
    c = GLA_TILE
    dk, dv = GLA_KEY_DIM, GLA_VALUE_DIM

    @pl.when(pl.program_id(1) == 0)
    def _():
        st_ref[...] = jnp.zeros_like(st_ref)

    la = la_ref[...]
    row = lax.broadcasted_iota(jnp.int32, la.shape, 0)
    lane = lax.broadcasted_iota(jnp.int32, la.shape, 1)
    odd_head = (lane & dk) != 0
    st_lane = lax.broadcasted_iota(jnp.int32, (dv, GLA_K_WIDTH), 1)

    b = la
    step = 1
    while step < c:
        b = b + jnp.where(row >= step, pltpu.roll(b, step, 0), 0.0)
        step *= 2

    qf = q_ref[...].astype(jnp.float32) * (dk ** -0.5)
    kf = k_ref[...].astype(jnp.float32)

    t_i = lax.broadcasted_iota(jnp.int32, (c, c), 0)
    s_i = lax.broadcasted_iota(jnp.int32, (c, c), 1)
    level = jnp.where(t_i >= s_i, 31 - lax.clz(t_i ^ s_i), -2)

    def split_heads(x):
        return (jnp.where(odd_head, 0.0, x).astype(jnp.bfloat16),
                jnp.where(odd_head, x, 0.0).astype(jnp.bfloat16))

    def add_level(acc, q_lvl, k_lvl, lvl):
        q_even, q_odd = split_heads(q_lvl)
        k_b = k_lvl.astype(jnp.bfloat16)
        hit = level == lvl
        for h in range(GLA_HEADS):
            cols = slice((h // 2) * LANES, (h // 2 + 1) * LANES)
            q_h = (q_odd if h % 2 else q_even)[:, cols]
            a = lax.dot_general(q_h, k_b[:, cols], _NT, preferred_element_type=jnp.float32)
            acc[h] = acc[h] + jnp.where(hit, a, 0.0)
        return acc

    acc = [jnp.zeros((c, c), jnp.float32) for _ in range(GLA_HEADS)]
    acc = add_level(acc, qf, kf, -1)
    b_end = b
    half = 1
    lvl = 0
    while half < c:
        b_before = pltpu.roll(b_end, half, 0)
        q_lvl = qf * jnp.exp(jnp.minimum(b - b_before, 0.0))
        k_lvl = kf * jnp.exp(jnp.minimum(b_end - b, 0.0))
        acc = add_level(acc, q_lvl, k_lvl, lvl)
        upper = (row & half) != 0
        b_end = jnp.where(upper, b_end, pltpu.roll(b_end, c - half, 0))
        half *= 2
        lvl += 1
    q_in = (qf * jnp.exp(b)).astype(jnp.bfloat16)
    k_out = (kf * jnp.exp(jnp.minimum(b_end - b, 0.0))).astype(jnp.bfloat16)
    tile_decay = jnp.exp(b_end[0:1, :])

    for h in range(GLA_HEADS):
        cols = slice((h // 2) * LANES, (h // 2 + 1) * LANES)
        v_h = v_ref[:, h * dv:(h + 1) * dv]
        st = st_ref[h]
        o_inter = lax.dot_general(q_in[:, cols], st[:, cols].astype(jnp.bfloat16), _NT,
                                  preferred_element_type=jnp.float32)
        o_intra = jnp.dot(acc[h].astype(jnp.bfloat16), v_h, preferred_element_type=jnp.float32)
        o_ref[:, h * dv:(h + 1) * dv] = (o_inter + o_intra).astype(o_ref.dtype)
        v_t = v_h.astype(jnp.float32).T.astype(jnp.bfloat16)
        upd = jnp.dot(v_t, k_out, preferred_element_type=jnp.float32)
        own = (st_lane >= h * dk) & (st_lane < (h + 1) * dk)
        st_ref[h] = st * tile_decay + jnp.where(own, upd, 0.0)


def _gla(gq, gk, gv, la, batch, seq):
    n_tile = seq // GLA_TILE

    def tok(width):
        return pl.BlockSpec((GLA_TILE, width), lambda b, t: (b * n_tile + t, 0))

    return pl.pallas_call(
        _gla_kernel,
        grid=(batch, n_tile),
        in_specs=[tok(GLA_K_WIDTH), tok(GLA_K_WIDTH), tok(GLA_V_WIDTH), tok(GLA_K_WIDTH)],
        out_specs=tok(GLA_V_WIDTH),
        out_shape=jax.ShapeDtypeStruct((batch * seq, GLA_V_WIDTH), jnp.bfloat16),
        scratch_shapes=[pltpu.VMEM((GLA_HEADS, GLA_VALUE_DIM, GLA_K_WIDTH), jnp.float32)],
        compiler_params=pltpu.CompilerParams(
            dimension_semantics=("arbitrary", "arbitrary"), vmem_limit_bytes=VMEM_LIMIT),
        name="gla",
    )(gq, gk, gv, la)


def _mixout_kernel(x_ref, om_ref, og_ref, gr_ref, mg_ref, gg_ref, wo_ref, post_ref, o_ref):
    om = _rms(om_ref[...].astype(jnp.float32), mg_ref[...]).astype(jnp.bfloat16)
    gate = _silu(gr_ref[...].astype(jnp.float32))
    og = og_ref[...].astype(jnp.float32)
    parts = [om]
    for h in range(GLA_HEADS):
        cols = slice(h * GLA_VALUE_DIM, (h + 1) * GLA_VALUE_DIM)
        parts.append((_rms(og[:, cols], gg_ref[:, cols]) * gate[:, cols]).astype(jnp.bfloat16))
    mix_in = jnp.concatenate(parts, axis=-1)
    mix = jnp.dot(mix_in, wo_ref[...], preferred_element_type=jnp.float32)
    o_ref[...] = x_ref[...] + _rms(mix, post_ref[...])


def _mixout(x, om, og, gr, mg, gg, wo, post_g):
    t = x.shape[0]

    def row(width):
        return pl.BlockSpec((OUT_TM, width), lambda i: (i, 0))

    return pl.pallas_call(
        _mixout_kernel,
        grid=(t // OUT_TM,),
        in_specs=[row(D_MODEL), row(MOBA_WIDTH), row(GLA_V_WIDTH), row(GLA_V_WIDTH),
                  _resident(mg.shape), _resident(gg.shape), _resident(wo.shape),
                  _resident(post_g.shape)],
        out_specs=row(D_MODEL),
        out_shape=jax.ShapeDtypeStruct((t, D_MODEL), jnp.float32),
        compiler_params=pltpu.CompilerParams(
            dimension_semantics=("arbitrary",), vmem_limit_bytes=VMEM_LIMIT),
        name="mixout",
    )(x, om, og, gr, mg, gg, wo, post_g)


def kernel(x, ffn1_pre_g, ffn1_w_gate, ffn1_w_up, ffn1_w_down, ffn1_post_g, mix_pre_g, w_in,
           gla_w_alpha_up, gla_b_alpha, moba_out_g, gla_out_g, w_out, mix_post_g, ffn2_pre_g,
           ffn2_w_gate, ffn2_w_up, ffn2_w_down, ffn2_post_g):
    batch, seq, d = x.shape
    assert d == D_MODEL and seq % MOBA_BLOCK == 0 and seq % GLA_TILE == 0
    assert (batch * seq) % FFN_TM == 0
    bf = jnp.bfloat16
    xt = x.reshape(batch * seq, d)
    for l in range(w_in.shape[0]):
        xt = _ffn(xt, ffn1_pre_g[l][None], ffn1_w_gate[l].astype(bf), ffn1_w_up[l].astype(bf),
                  ffn1_w_down[l].astype(bf), ffn1_post_g[l][None])

        n_qkv = 3 * MOBA_WIDTH
        n_gla = 2 * GLA_K_WIDTH + GLA_V_WIDTH
        w = w_in[l]
        wqkv = w[:, :n_qkv].astype(bf)
        wgla = jnp.concatenate(
            [w[:, n_qkv:n_qkv + n_gla], w[:, n_qkv + n_gla + GLA_GATE_RANK:]], axis=1).astype(bf)
        wa = jnp.pad(w[:, n_qkv + n_gla:n_qkv + n_gla + GLA_GATE_RANK],
                     ((0, 0), (0, GATE_PAD - GLA_GATE_RANK))).astype(bf)
        wup = jnp.pad(gla_w_alpha_up[l], ((0, GATE_PAD - GLA_GATE_RANK), (0, 0))).astype(bf)
        mq, mk, mv, kmean, gq, gk, gv, gr, la = _proj(
            xt, mix_pre_g[l][None], wqkv, wgla, wa, wup, gla_b_alpha[l][None])

        kmean = kmean.reshape(batch, seq // MOBA_BLOCK, MOBA_WIDTH)
        o_moba = _moba(mq, mk, mv, kmean, batch, seq)
        o_gla = _gla(gq, gk, gv, la, batch, seq)

        xt = _mixout(xt, o_moba, o_gla, gr, moba_out_g[l][None],
                     gla_out_g[l].reshape(1, GLA_V_WIDTH), w_out[l].astype(bf), mix_post_g[l][None])

        xt = _ffn(xt, ffn2_pre_g[l][None], ffn2_w_gate[l].astype(bf), ffn2_w_up[l].astype(bf),
                  ffn2_w_down[l].astype(bf), ffn2_post_g[l][None])
    return xt.reshape(batch, seq, d)
```

```python
import functools

import jax
import jax.numpy as jnp
from jax import lax
from jax.experimental import pallas as pl
from jax.experimental.pallas import tpu as pltpu

D_MODEL = 1024
D_FF = 2816
RMS_EPS = 1e-6

MOBA_WIDTH = 512
MOBA_HEAD_DIM = 64
MOBA_BLOCK = 256
MOBA_TOPK = 3

GLA_HEADS = 4
GLA_KEY_DIM = 64
GLA_VALUE_DIM = 128
GLA_K_WIDTH = GLA_HEADS * GLA_KEY_DIM
GLA_V_WIDTH = GLA_HEADS * GLA_VALUE_DIM
GLA_GATE_RANK = 16
GLA_GATE_TAU = 16.0
GLA_TILE = 256

LANES = 128
GATE_PAD = LANES
NEG_BIG = -1e30

FFN_TM = 512
FFN_TF = 256
PROJ_TM = 512
OUT_TM = 512
VMEM_LIMIT = 56 * 1024 * 1024

_NT = (((1,), (1,)), ((), ()))


def _rms(x, g):
    return x * lax.rsqrt(jnp.mean(x * x, axis=-1, keepdims=True) + RMS_EPS) * g


def _silu(x):
    return x / (1.0 + jnp.exp(-x))


def _resident(shape):
    return pl.BlockSpec(shape, lambda *_: (0,) * len(shape), pipeline_mode=pl.Buffered(1))


def _ffn_kernel(x_ref, pre_ref, wg_ref, wu_ref, wd_ref, post_ref, o_ref, h_ref):
    x = x_ref[...]
    xn = _rms(x, pre_ref[...]).astype(jnp.bfloat16)
    for c in range(D_FF // FFN_TF):
        cols = slice(c * FFN_TF, (c + 1) * FFN_TF)
        g = jnp.dot(xn, wg_ref[:, cols], preferred_element_type=jnp.float32)
        u = jnp.dot(xn, wu_ref[:, cols], preferred_element_type=jnp.float32)
        h_ref[:, cols] = (_silu(g) * u).astype(jnp.bfloat16)
    f = jnp.dot(h_ref[...], wd_ref[...], preferred_element_type=jnp.float32)
    o_ref[...] = x + 0.5 * _rms(f, post_ref[...])


def _ffn(x, pre_g, wg, wu, wd, post_g):
    t = x.shape[0]
    row = pl.BlockSpec((FFN_TM, D_MODEL), lambda i: (i, 0))
    return pl.pallas_call(
        _ffn_kernel,
        grid=(t // FFN_TM,),
        in_specs=[row, _resident((1, D_MODEL)), _resident((D_MODEL, D_FF)),
                  _resident((D_MODEL, D_FF)), _resident((D_FF, D_MODEL)),
                  _resident((1, D_MODEL))],
        out_specs=row,
        out_shape=jax.ShapeDtypeStruct((t, D_MODEL), jnp.float32),
        scratch_shapes=[pltpu.VMEM((FFN_TM, D_FF), jnp.bfloat16)],
        compiler_params=pltpu.CompilerParams(
            dimension_semantics=("arbitrary",), vmem_limit_bytes=VMEM_LIMIT),
        name="ffn",
    )(x, pre_g, wg, wu, wd, post_g)


def _proj_kernel(x_ref, g_ref, wqkv_ref, wgla_ref, wa_ref, wup_ref, ba_ref,
                 mq_ref, mk_ref, mv_ref, kmean_ref, gq_ref, gk_ref, gv_ref, gr_ref, la_ref):
    h = _rms(x_ref[...], g_ref[...]).astype(jnp.bfloat16)

    def proj(w_ref, lo, hi):
        return jnp.dot(h, w_ref[:, lo:hi], preferred_element_type=jnp.float32)

    w = MOBA_WIDTH
    mq_ref[...] = proj(wqkv_ref, 0, w).astype(jnp.bfloat16)
    mk = proj(wqkv_ref, w, 2 * w)
    mk_ref[...] = mk.astype(jnp.bfloat16)
    for blk in range(PROJ_TM // MOBA_BLOCK):
        rows = slice(blk * MOBA_BLOCK, (blk + 1) * MOBA_BLOCK)
        kmean_ref[blk] = jnp.mean(mk[rows], axis=0, keepdims=True)
    mv_ref[...] = proj(wqkv_ref, 2 * w, 3 * w).astype(jnp.bfloat16)

    kw, vw = GLA_K_WIDTH, GLA_V_WIDTH
    gq_ref[...] = proj(wgla_ref, 0, kw).astype(jnp.bfloat16)
    gk_ref[...] = proj(wgla_ref, kw, 2 * kw).astype(jnp.bfloat16)
    gv_ref[...] = proj(wgla_ref, 2 * kw, 2 * kw + vw).astype(jnp.bfloat16)
    gr_ref[...] = proj(wgla_ref, 2 * kw + vw, 2 * kw + 2 * vw).astype(jnp.bfloat16)

    ga = jnp.dot(h, wa_ref[...], preferred_element_type=jnp.float32)
    z = jnp.dot(ga.astype(jnp.bfloat16), wup_ref[...],
                preferred_element_type=jnp.float32) + ba_ref[...]
    log_sig = jnp.minimum(z, 0.0) - jnp.log(1.0 + jnp.exp(-jnp.abs(z)))
    la_ref[...] = log_sig * (1.0 / GLA_GATE_TAU)


def _proj(x, g, wqkv, wgla, wa, wup, ba):
    t = x.shape[0]
    nblk = PROJ_TM // MOBA_BLOCK

    def row(width):
        return pl.BlockSpec((PROJ_TM, width), lambda i: (i, 0))

    def out(width, dtype=jnp.bfloat16):
        return jax.ShapeDtypeStruct((t, width), dtype)

    return pl.pallas_call(
        _proj_kernel,
        grid=(t // PROJ_TM,),
        in_specs=[row(D_MODEL), _resident((1, D_MODEL)), _resident(wqkv.shape),
                  _resident(wgla.shape), _resident(wa.shape), _resident(wup.shape),
                  _resident(ba.shape)],
        out_specs=[row(MOBA_WIDTH), row(MOBA_WIDTH), row(MOBA_WIDTH),
                   pl.BlockSpec((nblk, 1, MOBA_WIDTH), lambda i: (i, 0, 0)),
                   row(GLA_K_WIDTH), row(GLA_K_WIDTH), row(GLA_V_WIDTH), row(GLA_V_WIDTH),
                   row(GLA_K_WIDTH)],
        out_shape=[out(MOBA_WIDTH), out(MOBA_WIDTH), out(MOBA_WIDTH),
                   jax.ShapeDtypeStruct((t // MOBA_BLOCK, 1, MOBA_WIDTH), jnp.float32),
                   out(GLA_K_WIDTH), out(GLA_K_WIDTH), out(GLA_V_WIDTH), out(GLA_V_WIDTH),
                   out(GLA_K_WIDTH, jnp.float32)],
        compiler_params=pltpu.CompilerParams(
            dimension_semantics=("arbitrary",), vmem_limit_bytes=VMEM_LIMIT),
        name="proj",
    )(x, g, wqkv, wgla, wa, wup, ba)


def _moba_kernel(q_ref, k_ref, v_ref, kmean_ref, o_ref, kaug_ref, *, n_blk):
    blk = MOBA_BLOCK
    dh = MOBA_HEAD_DIM
    seq = n_blk * blk
    q_t = q_ref[...].astype(jnp.float32).T * (dh ** -0.5)
    v_t = v_ref[...].astype(jnp.float32).T.astype(jnp.bfloat16)
    kmean = kmean_ref[0]
    km_hi = kmean.astype(jnp.bfloat16)
    km_lo = (kmean - km_hi.astype(jnp.float32)).astype(jnp.bfloat16)

    kaug_ref[:, 0:LANES] = k_ref[...]
    key_blk = lax.broadcasted_iota(jnp.int32, (seq, LANES), 0) // blk
    hot_col = lax.broadcasted_iota(jnp.int32, (seq, LANES), 1)
    kaug_ref[:, LANES:2 * LANES] = jnp.where(key_blk == hot_col, 1.0, 0.0).astype(jnp.bfloat16)

    feat = lax.broadcasted_iota(jnp.int32, q_t.shape, 0)
    blk_row = lax.broadcasted_iota(jnp.int32, (n_blk, blk), 0)
    key_i = lax.broadcasted_iota(jnp.int32, (blk, blk), 0)
    qry_i = lax.broadcasted_iota(jnp.int32, (blk, blk), 1)
    causal = key_i <= qry_i
    bias_pad = jnp.zeros((LANES - n_blk, blk), jnp.float32)

    out_t = [None] * n_blk
    for head in range(LANES // dh):
        own = (feat >= head * dh) & (feat < (head + 1) * dh)
        qh_t = jnp.where(own, q_t, 0.0)
        for i in range(n_blk):
            qi_f = qh_t[:, i * blk:(i + 1) * blk]
            qi_t = qi_f.astype(jnp.bfloat16)
            n_keys = (i + 1) * blk
            if i > MOBA_TOPK:
                gate = (jnp.dot(km_hi, qi_t, preferred_element_type=jnp.float32)
                        + jnp.dot(km_lo, qi_t, preferred_element_type=jnp.float32))
                rank = jnp.zeros((n_blk, blk), jnp.float32)
                for j in range(i):
                    gj = gate[j:j + 1, :]
                    beats = (gj > gate) | ((gj == gate) & (blk_row > j))
                    rank = rank + jnp.where(beats, 1.0, 0.0)
                keep = (rank < float(MOBA_TOPK)) | (blk_row >= i)
                bias = jnp.where(keep, 0.0, NEG_BIG)
                q_aug = jnp.concatenate([qi_f, bias, bias_pad], axis=0).astype(jnp.bfloat16)
                s = jnp.dot(kaug_ref[0:n_keys, :], q_aug, preferred_element_type=jnp.float32)
            else:
                s = jnp.dot(k_ref[0:n_keys, :], qi_t, preferred_element_type=jnp.float32)
            s_own = jnp.where(causal, s[i * blk:n_keys, :], NEG_BIG)
            sm = s_own if i == 0 else jnp.concatenate([s[0:i * blk, :], s_own], axis=0)
            m = jnp.max(sm, axis=0, keepdims=True)
            p = jnp.exp(sm - m)
            inv = 1.0 / jnp.sum(p, axis=0, keepdims=True)
            o_t = jnp.dot(v_t[:, 0:n_keys], p.astype(jnp.bfloat16),
                          preferred_element_type=jnp.float32) * inv
            rows = slice(head * dh, (head + 1) * dh)
            if head == 0:
                out_t[i] = o_t[rows, :]
            else:
                out_t[i] = jnp.concatenate([out_t[i], o_t[rows, :]], axis=0)
    for i in range(n_blk):
        o_ref[i * blk:(i + 1) * blk, :] = out_t[i].T.astype(o_ref.dtype)


def _moba(mq, mk, mv, kmean, batch, seq):
    n_blk = seq // MOBA_BLOCK
    n_col = MOBA_WIDTH // LANES
    tok = pl.BlockSpec((seq, LANES), lambda b, c: (b, c))
    return pl.pallas_call(
        functools.partial(_moba_kernel, n_blk=n_blk),
        grid=(batch, n_col),
        in_specs=[tok, tok, tok, pl.BlockSpec((1, n_blk, LANES), lambda b, c: (b, 0, c))],
        out_specs=tok,
        out_shape=jax.ShapeDtypeStruct((batch * seq, MOBA_WIDTH), jnp.bfloat16),
        scratch_shapes=[pltpu.VMEM((seq, 2 * LANES), jnp.bfloat16)],
        compiler_params=pltpu.CompilerParams(
            dimension_semantics=("arbitrary", "arbitrary"), vmem_limit_bytes=VMEM_LIMIT),
        name="moba",
    )(mq, mk, mv, kmean)


def _gla_kernel(q_ref, k_ref, v_ref, la_ref, o_ref, st_ref):
    c = GLA_TILE
    dk, dv = GLA_KEY_DIM, GLA_VALUE_DIM

    @pl.when(pl.program_id(1) == 0)
    def _():
        st_ref[...] = jnp.zeros_like(st_ref)

    la = la_ref[...]
    row = lax.broadcasted_iota(jnp.int32, la.shape, 0)
    lane = lax.broadcasted_iota(jnp.int32, la.shape, 1)
    odd_head = (lane & dk) != 0
    st_lane = lax.broadcasted_iota(jnp.int32, (dv, GLA_K_WIDTH), 1)

    b = la
    step = 1
    while step < c:
        b = b + jnp.where(row >= step, pltpu.roll(b, step, 0), 0.0)
        step *= 2

    qf = q_ref[...].astype(jnp.float32) * (dk ** -0.5)
    kf = k_ref[...].astype(jnp.float32)

    t_i = lax.broadcasted_iota(jnp.int32, (c, c), 0)
    s_i = lax.broadcasted_iota(jnp.int32, (c, c), 1)
    level = jnp.where(t_i >= s_i, 31 - lax.clz(t_i ^ s_i), -2)

    def split_heads(x):
        return (jnp.where(odd_head, 0.0, x).astype(jnp.bfloat16),
                jnp.where(odd_head, x, 0.0).astype(jnp.bfloat16))

    def add_level(acc, q_lvl, k_lvl, lvl):
        q_even, q_odd = split_heads(q_lvl)
        k_b = k_lvl.astype(jnp.bfloat16)
        hit = level == lvl
        for h in range(GLA_HEADS):
            cols = slice((h // 2) * LANES, (h // 2 + 1) * LANES)
            q_h = (q_odd if h % 2 else q_even)[:, cols]
            a = lax.dot_general(q_h, k_b[:, cols], _NT, preferred_element_type=jnp.float32)
            acc[h] = acc[h] + jnp.where(hit, a, 0.0)
        return acc

    acc = [jnp.zeros((c, c), jnp.float32) for _ in range(GLA_HEADS)]
    acc = add_level(acc, qf, kf, -1)
    b_end = b
    half = 1
    lvl = 0
    while half < c:
        b_before = pltpu.roll(b_end, half, 0)
        q_lvl = qf * jnp.exp(jnp.minimum(b - b_before, 0.0))
        k_lvl = kf * jnp.exp(jnp.minimum(b_end - b, 0.0))
        acc = add_level(acc, q_lvl, k_lvl, lvl)
        upper = (row & half) != 0
        b_end = jnp.where(upper, b_end, pltpu.roll(b_end, c - half, 0))
        half *= 2
        lvl += 1
    q_in = (qf * jnp.exp(b)).astype(jnp.bfloat16)
    k_out = (kf * jnp.exp(jnp.minimum(b_end - b, 0.0))).astype(jnp.bfloat16)
    tile_decay = jnp.exp(b_end[0:1, :])

    for h in range(GLA_HEADS):
        cols = slice((h // 2) * LANES, (h // 2 + 1) * LANES)
        v_h = v_ref[:, h * dv:(h + 1) * dv]
        st = st_ref[h]
        o_inter = lax.dot_general(q_in[:, cols], st[:, cols].astype(jnp.bfloat16), _NT,
                                  preferred_element_type=jnp.float32)
        o_intra = jnp.dot(acc[h].astype(jnp.bfloat16), v_h, preferred_element_type=jnp.float32)
        o_ref[:, h * dv:(h + 1) * dv] = (o_inter + o_intra).astype(o_ref.dtype)
        v_t = v_h.astype(jnp.float32).T.astype(jnp.bfloat16)
        upd = jnp.dot(v_t, k_out, preferred_element_type=jnp.float32)
        own = (st_lane >= h * dk) & (st_lane < (h + 1) * dk)
        st_ref[h] = st * tile_decay + jnp.where(own, upd, 0.0)


def _gla(gq, gk, gv, la, batch, seq):
    n_tile = seq // GLA_TILE

    def tok(width):
        return pl.BlockSpec((GLA_TILE, width), lambda b, t: (b * n_tile + t, 0))

    return pl.pallas_call(
        _gla_kernel,
        grid=(batch, n_tile),
        in_specs=[tok(GLA_K_WIDTH), tok(GLA_K_WIDTH), tok(GLA_V_WIDTH), tok(GLA_K_WIDTH)],
        out_specs=tok(GLA_V_WIDTH),
        out_shape=jax.ShapeDtypeStruct((batch * seq, GLA_V_WIDTH), jnp.bfloat16),
        scratch_shapes=[pltpu.VMEM((GLA_HEADS, GLA_VALUE_DIM, GLA_K_WIDTH), jnp.float32)],
        compiler_params=pltpu.CompilerParams(
            dimension_semantics=("arbitrary", "arbitrary"), vmem_limit_bytes=VMEM_LIMIT),
        name="gla",
    )(gq, gk, gv, la)


def _mixout_kernel(x_ref, om_ref, og_ref, gr_ref, mg_ref, gg_ref, wo_ref, post_ref, o_ref):
    om = _rms(om_ref[...].astype(jnp.float32), mg_ref[...]).astype(jnp.bfloat16)
    gate = _silu(gr_ref[...].astype(jnp.float32))
    og = og_ref[...].astype(jnp.float32)
    parts = [om]
    for h in range(GLA_HEADS):
        cols = slice(h * GLA_VALUE_DIM, (h + 1) * GLA_VALUE_DIM)
        parts.append((_rms(og[:, cols], gg_ref[:, cols]) * gate[:, cols]).astype(jnp.bfloat16))
    mix_in = jnp.concatenate(parts, axis=-1)
    mix = jnp.dot(mix_in, wo_ref[...], preferred_element_type=jnp.float32)
    o_ref[...] = x_ref[...] + _rms(mix, post_ref[...])


def _mixout(x, om, og, gr, mg, gg, wo, post_g):
    t = x.shape[0]

    def row(width):
        return pl.BlockSpec((OUT_TM, width), lambda i: (i, 0))

    return pl.pallas_call(
        _mixout_kernel,
        grid=(t // OUT_TM,),
        in_specs=[row(D_MODEL), row(MOBA_WIDTH), row(GLA_V_WIDTH), row(GLA_V_WIDTH),
                  _resident(mg.shape), _resident(gg.shape), _resident(wo.shape),
                  _resident(post_g.shape)],
        out_specs=row(D_MODEL),
        out_shape=jax.ShapeDtypeStruct((t, D_MODEL), jnp.float32),
        compiler_params=pltpu.CompilerParams(
            dimension_semantics=("arbitrary",), vmem_limit_bytes=VMEM_LIMIT),
        name="mixout",
    )(x, om, og, gr, mg, gg, wo, post_g)


def kernel(x, ffn1_pre_g, ffn1_w_gate, ffn1_w_up, ffn1_w_down, ffn1_post_g, mix_pre_g, w_in,
           gla_w_alpha_up, gla_b_alpha, moba_out_g, gla_out_g, w_out, mix_post_g, ffn2_pre_g,
           ffn2_w_gate, ffn2_w_up, ffn2_w_down, ffn2_post_g):
    batch, seq, d = x.shape
    assert d == D_MODEL and seq % MOBA_BLOCK == 0 and seq % GLA_TILE == 0
    assert (batch * seq) % FFN_TM == 0
    bf = jnp.bfloat16
    xt = x.reshape(batch * seq, d)
    for l in range(w_in.shape[0]):
        xt = _ffn(xt, ffn1_pre_g[l][None], ffn1_w_gate[l].astype(bf), ffn1_w_up[l].astype(bf),
                  ffn1_w_down[l].astype(bf), ffn1_post_g[l][None])

        n_qkv = 3 * MOBA_WIDTH
        n_gla = 2 * GLA_K_WIDTH + GLA_V_WIDTH
        w = w_in[l]
        wqkv = w[:, :n_qkv].astype(bf)
        wgla = jnp.concatenate(
            [w[:, n_qkv:n_qkv + n_gla], w[:, n_qkv + n_gla + GLA_GATE_RANK:]], axis=1).astype(bf)
        wa = jnp.pad(w[:, n_qkv + n_gla:n_qkv + n_gla + GLA_GATE_RANK],
                     ((0, 0), (0, GATE_PAD - GLA_GATE_RANK))).astype(bf)
        wup = jnp.pad(gla_w_alpha_up[l], ((0, GATE_PAD - GLA_GATE_RANK), (0, 0))).astype(bf)
        mq, mk, mv, kmean, gq, gk, gv, gr, la = _proj(
            xt, mix_pre_g[l][None], wqkv, wgla, wa, wup, gla_b_alpha[l][None])

        kmean = kmean.reshape(batch, seq // MOBA_BLOCK, MOBA_WIDTH)
        o_moba = _moba(mq, mk, mv, kmean, batch, seq)
        o_gla = _gla(gq, gk, gv, la, batch, seq)

        xt = _mixout(xt, o_moba, o_gla, gr, moba_out_g[l][None],
                     gla_out_g[l].reshape(1, GLA_V_WIDTH), w_out[l].astype(bf), mix_post_g[l][None])

        xt = _ffn(xt, ffn2_pre_g[l][None], ffn2_w_gate[l].astype(bf), ffn2_w_up[l].astype(bf),
                  ffn2_w_down[l].astype(bf), ffn2_post_g[l][None])
    return xt.reshape(batch, seq, d)
```

```python
import functools

import jax
import jax.numpy as jnp
from jax import lax
from jax.experimental import pallas as pl
from jax.experimental.pallas import tpu as pltpu

D_MODEL = 1024
D_FF = 2816
RMS_EPS = 1e-6

MOBA_WIDTH = 512
MOBA_HEAD_DIM = 64
MOBA_BLOCK = 256
MOBA_TOPK = 3
MOBA_Q_SCALE = MOBA_HEAD_DIM ** -0.5 * 1.4426950408889634

GLA_HEADS = 4
GLA_KEY_DIM = 64
GLA_VALUE_DIM = 128
GLA_K_WIDTH = GLA_HEADS * GLA_KEY_DIM
GLA_V_WIDTH = GLA_HEADS * GLA_VALUE_DIM
GLA_GATE_RANK = 16
GLA_GATE_TAU = 16.0
GLA_TILE = 256

LANES = 128
GATE_PAD = LANES
NEG_BIG = -1e30
SUM_ROWS = 16

FFN_TM = 512
FFN_TF = 256
PROJ_TM = 512
OUT_TM = 512
VMEM_LIMIT = 56 * 1024 * 1024

_NT = (((1,), (1,)), ((), ()))


def _rms(x, g):
    return x * lax.rsqrt(jnp.mean(x * x, axis=-1, keepdims=True) + RMS_EPS) * g


def _silu(x):
    return x / (1.0 + jnp.exp(-x))


def _resident(shape):
    return pl.BlockSpec(shape, lambda *_: (0,) * len(shape), pipeline_mode=pl.Buffered(1))


def _ffn_kernel(x_ref, pre_ref, wg_ref, wu_ref, wd_ref, post_ref, o_ref, h_ref):
    x = x_ref[...]
    xn = _rms(x, pre_ref[...]).astype(jnp.bfloat16)
    for c in range(D_FF // FFN_TF):
        cols = slice(c * FFN_TF, (c + 1) * FFN_TF)
        g = jnp.dot(xn, wg_ref[:, cols], preferred_element_type=jnp.float32)
        u = jnp.dot(xn, wu_ref[:, cols], preferred_element_type=jnp.float32)
        h_ref[:, cols] = (_silu(g) * u).astype(jnp.bfloat16)
    f = jnp.dot(h_ref[...], wd_ref[...], preferred_element_type=jnp.float32)
    o_ref[...] = x + 0.5 * _rms(f, post_ref[...])


def _ffn(x, pre_g, wg, wu, wd, post_g):
    t = x.shape[0]
    row = pl.BlockSpec((FFN_TM, D_MODEL), lambda i: (i, 0))
    return pl.pallas_call(
        _ffn_kernel,
        grid=(t // FFN_TM,),
        in_specs=[row, _resident((1, D_MODEL)), _resident((D_MODEL, D_FF)),
                  _resident((D_MODEL, D_FF)), _resident((D_FF, D_MODEL)),
                  _resident((1, D_MODEL))],
        out_specs=row,
        out_shape=jax.ShapeDtypeStruct((t, D_MODEL), jnp.float32),
        scratch_shapes=[pltpu.VMEM((FFN_TM, D_FF), jnp.bfloat16)],
        compiler_params=pltpu.CompilerParams(
            dimension_semantics=("arbitrary",), vmem_limit_bytes=VMEM_LIMIT),
        name="ffn",
    )(x, pre_g, wg, wu, wd, post_g)


def _proj_kernel(x_ref, g_ref, wqkv_ref, wgla_ref, wa_ref, wup_ref, ba_ref,
                 mq_ref, mk_ref, mv_ref, kmean_ref, gq_ref, gk_ref, gv_ref, gr_ref, la_ref):
    h = _rms(x_ref[...], g_ref[...]).astype(jnp.bfloat16)

    def proj(w_ref, lo, hi):
        return jnp.dot(h, w_ref[:, lo:hi], preferred_element_type=jnp.float32)

    w = MOBA_WIDTH
    mq_ref[...] = (proj(wqkv_ref, 0, w) * MOBA_Q_SCALE).astype(jnp.bfloat16)
    mk = proj(wqkv_ref, w, 2 * w)
    mk_ref[...] = mk.astype(jnp.bfloat16)
    for blk in range(PROJ_TM // MOBA_BLOCK):
        rows = slice(blk * MOBA_BLOCK, (blk + 1) * MOBA_BLOCK)
        kmean_ref[blk] = jnp.mean(mk[rows], axis=0, keepdims=True)
    mv_ref[...] = proj(wqkv_ref, 2 * w, 3 * w).astype(jnp.bfloat16)

    kw, vw = GLA_K_WIDTH, GLA_V_WIDTH
    gq_ref[...] = proj(wgla_ref, 0, kw).astype(jnp.bfloat16)
    gk_ref[...] = proj(wgla_ref, kw, 2 * kw).astype(jnp.bfloat16)
    gv_ref[...] = proj(wgla_ref, 2 * kw, 2 * kw + vw).astype(jnp.bfloat16)
    gr_ref[...] = proj(wgla_ref, 2 * kw + vw, 2 * kw + 2 * vw).astype(jnp.bfloat16)

    ga = jnp.dot(h, wa_ref[...], preferred_element_type=jnp.float32)
    z = jnp.dot(ga.astype(jnp.bfloat16), wup_ref[...],
                preferred_element_type=jnp.float32) + ba_ref[...]
    log_sig = jnp.minimum(z, 0.0) - jnp.log(1.0 + jnp.exp(-jnp.abs(z)))
    la_ref[...] = log_sig * (1.0 / GLA_GATE_TAU)


def _proj(x, g, wqkv, wgla, wa, wup, ba):
    t = x.shape[0]
    nblk = PROJ_TM // MOBA_BLOCK

    def row(width):
        return pl.BlockSpec((PROJ_TM, width), lambda i: (i, 0))

    def out(width, dtype=jnp.bfloat16):
        return jax.ShapeDtypeStruct((t, width), dtype)

    return pl.pallas_call(
        _proj_kernel,
        grid=(t // PROJ_TM,),
        in_specs=[row(D_MODEL), _resident((1, D_MODEL)), _resident(wqkv.shape),
                  _resident(wgla.shape), _resident(wa.shape), _resident(wup.shape),
                  _resident(ba.shape)],
        out_specs=[row(MOBA_WIDTH), row(MOBA_WIDTH), row(MOBA_WIDTH),
                   pl.BlockSpec((nblk, 1, MOBA_WIDTH), lambda i: (i, 0, 0)),
                   row(GLA_K_WIDTH), row(GLA_K_WIDTH), row(GLA_V_WIDTH), row(GLA_V_WIDTH),
                   row(GLA_K_WIDTH)],
        out_shape=[out(MOBA_WIDTH), out(MOBA_WIDTH), out(MOBA_WIDTH),
                   jax.ShapeDtypeStruct((t // MOBA_BLOCK, 1, MOBA_WIDTH), jnp.float32),
                   out(GLA_K_WIDTH), out(GLA_K_WIDTH), out(GLA_V_WIDTH), out(GLA_V_WIDTH),
                   out(GLA_K_WIDTH, jnp.float32)],
        compiler_params=pltpu.CompilerParams(
            dimension_semantics=("arbitrary",), vmem_limit_bytes=VMEM_LIMIT),
        name="proj",
    )(x, g, wqkv, wgla, wa, wup, ba)


def _moba_kernel(q_ref, k_ref, v_ref, kmean_ref, o_ref, kaug_ref, *, n_blk):
    blk = MOBA_BLOCK
    dh = MOBA_HEAD_DIM
    seq = n_blk * blk
    q_t = q_ref[...].astype(jnp.float32).T
    v_t = jnp.concatenate([v_ref[...].astype(jnp.float32).T, jnp.ones((SUM_ROWS, seq), jnp.float32)],
                          axis=0).astype(jnp.bfloat16)
    kmean = kmean_ref[0]
    km_hi = kmean.astype(jnp.bfloat16)
    km_lo = (kmean - km_hi.astype(jnp.float32)).astype(jnp.bfloat16)

    kaug_ref[:, 0:LANES] = k_ref[...]
    key_blk = lax.broadcasted_iota(jnp.int32, (seq, LANES), 0) // blk
    hot_col = lax.broadcasted_iota(jnp.int32, (seq, LANES), 1)
    kaug_ref[:, LANES:2 * LANES] = jnp.where(key_blk == hot_col, 1.0, 0.0).astype(jnp.bfloat16)

    n_head = LANES // dh
    wide = n_head * blk
    feat = lax.broadcasted_iota(jnp.int32, q_t.shape, 0)
    q_heads = [jnp.where((feat >= h * dh) & (feat < (h + 1) * dh), q_t, 0.0) for h in range(n_head)]
    blk_row = lax.broadcasted_iota(jnp.int32, (n_blk, wide), 0)
    key_i = lax.broadcasted_iota(jnp.int32, (blk, wide), 0)
    qry_i = lax.broadcasted_iota(jnp.int32, (blk, wide), 1) % blk
    causal = key_i <= qry_i
    bias_pad = jnp.zeros((LANES - n_blk, wide), jnp.float32)

    for i in range(n_blk):
        qi_f = jnp.concatenate([qh[:, i * blk:(i + 1) * blk] for qh in q_heads], axis=1)
        qi_t = qi_f.astype(jnp.bfloat16)
        n_keys = (i + 1) * blk
        if i > MOBA_TOPK:
            gate = (jnp.dot(km_hi, qi_t, preferred_element_type=jnp.float32)
                    + jnp.dot(km_lo, qi_t, preferred_element_type=jnp.float32))
            rank = jnp.zeros((n_blk, wide), jnp.float32)
            for j in range(i):
                gj = gate[j:j + 1, :]
                beats = (gj > gate) | ((gj == gate) & (blk_row > j))
                rank = rank + jnp.where(beats, 1.0, 0.0)
            keep = (rank < float(MOBA_TOPK)) | (blk_row >= i)
            bias = jnp.where(keep, 0.0, NEG_BIG)
            q_aug = jnp.concatenate([qi_f, bias, bias_pad], axis=0).astype(jnp.bfloat16)
            s = jnp.dot(kaug_ref[0:n_keys, :], q_aug, preferred_element_type=jnp.float32)
        else:
            s = jnp.dot(k_ref[0:n_keys, :], qi_t, preferred_element_type=jnp.float32)
        s_own = jnp.where(causal, s[i * blk:n_keys, :], NEG_BIG)
        sm = s_own if i == 0 else jnp.concatenate([s[0:i * blk, :], s_own], axis=0)
        m = jnp.max(sm, axis=0, keepdims=True)
        p = jnp.exp2(sm - m).astype(jnp.bfloat16)
        o_t = jnp.dot(v_t[:, 0:n_keys], p, preferred_element_type=jnp.float32)
        o_t = o_t[0:LANES, :] * (1.0 / o_t[LANES:LANES + 1, :])
        o_own = jnp.concatenate([o_t[h * dh:(h + 1) * dh, h * blk:(h + 1) * blk]
                                 for h in range(n_head)], axis=0)
        o_ref[i * blk:(i + 1) * blk, :] = o_own.T.astype(o_ref.dtype)


def _moba(mq, mk, mv, kmean, batch, seq):
    n_blk = seq // MOBA_BLOCK
    n_col = MOBA_WIDTH // LANES
    tok = pl.BlockSpec((seq, LANES), lambda b, c: (b, c))
    return pl.pallas_call(
        functools.partial(_moba_kernel, n_blk=n_blk),
        grid=(batch, n_col),
        in_specs=[tok, tok, tok, pl.BlockSpec((1, n_blk, LANES), lambda b, c: (b, 0, c))],
        out_specs=tok,
        out_shape=jax.ShapeDtypeStruct((batch * seq, MOBA_WIDTH), jnp.bfloat16),
        scratch_shapes=[pltpu.VMEM((seq, 2 * LANES), jnp.bfloat16)],
        compiler_params=pltpu.CompilerParams(
            dimension_semantics=("arbitrary", "arbitrary"), vmem_limit_bytes=VMEM_LIMIT),
        name="moba",
    )(mq, mk, mv, kmean)


def _gla_kernel(q_ref, k_ref, v_ref, la_ref, o_ref, st_ref):
    c = GLA_TILE
    dk, dv = GLA_KEY_DIM, GLA_VALUE_DIM

    @pl.when(pl.program_id(1) == 0)
    def _():
        st_ref[...] = jnp.zeros_like(st_ref)

    la = la_ref[...]
    row = lax.broadcasted_iota(jnp.int32, la.shape, 0)
    lane = lax.broadcasted_iota(jnp.int32, la.shape, 1)
    odd_head = (lane & dk) != 0
    st_lane = lax.broadcasted_iota(jnp.int32, (dv, GLA_K_WIDTH), 1)

    b = la
    step = 1
    while step < c:
        b = b + jnp.where(row >= step, pltpu.roll(b, step, 0), 0.0)
        step *= 2

    qf = q_ref[...].astype(jnp.float32) * (dk ** -0.5)
    kf = k_ref[...].astype(jnp.float32)

    t_i = lax.broadcasted_iota(jnp.int32, (c, c), 0)
    s_i = lax.broadcasted_iota(jnp.int32, (c, c), 1)
    level = jnp.where(t_i >= s_i, 31 - lax.clz(t_i ^ s_i), -2)

    def split_heads(x):
        return (jnp.where(odd_head, 0.0, x).astype(jnp.bfloat16),
                jnp.where(odd_head, x, 0.0).astype(jnp.bfloat16))

    def add_level(acc, q_lvl, k_lvl, lvl):
        q_even, q_odd = split_heads(q_lvl)
        k_b = k_lvl.astype(jnp.bfloat16)
        hit = level == lvl
        for h in range(GLA_HEADS):
            cols = slice((h // 2) * LANES, (h // 2 + 1) * LANES)
            q_h = (q_odd if h % 2 else q_even)[:, cols]
            a = lax.dot_general(q_h, k_b[:, cols], _NT, preferred_element_type=jnp.float32)
            acc[h] = acc[h] + jnp.where(hit, a, 0.0)
        return acc

    acc = [jnp.zeros((c, c), jnp.float32) for _ in range(GLA_HEADS)]
    acc = add_level(acc, qf, kf, -1)
    b_end = b
    half = 1
    lvl = 0
    while half < c:
        b_before = pltpu.roll(b_end, half, 0)
        q_lvl = qf * jnp.exp(jnp.minimum(b - b_before, 0.0))
        k_lvl = kf * jnp.exp(jnp.minimum(b_end - b, 0.0))
        acc = add_level(acc, q_lvl, k_lvl, lvl)
        upper = (row & half) != 0
        b_end = jnp.where(upper, b_end, pltpu.roll(b_end, c - half, 0))
        half *= 2
        lvl += 1
    q_in = (qf * jnp.exp(b)).astype(jnp.bfloat16)
    k_out = (kf * jnp.exp(jnp.minimum(b_end - b, 0.0))).astype(jnp.bfloat16)
    tile_decay = jnp.exp(b_end[0:1, :])

    for h in range(GLA_HEADS):
        cols = slice((h // 2) * LANES, (h // 2 + 1) * LANES)
        v_h = v_ref[:, h * dv:(h + 1) * dv]
        st = st_ref[h]
        o_inter = lax.dot_general(q_in[:, cols], st[:, cols].astype(jnp.bfloat16), _NT,
                                  preferred_element_type=jnp.float32)
        o_intra = jnp.dot(acc[h].astype(jnp.bfloat16), v_h, preferred_element_type=jnp.float32)
        o_ref[:, h * dv:(h + 1) * dv] = (o_inter + o_intra).astype(o_ref.dtype)
        v_t = v_h.astype(jnp.float32).T.astype(jnp.bfloat16)
        upd = jnp.dot(v_t, k_out, preferred_element_type=jnp.float32)
        own = (st_lane >= h * dk) & (st_lane < (h + 1) * dk)
        st_ref[h] = st * tile_decay + jnp.where(own, upd, 0.0)


def _gla(gq, gk, gv, la, batch, seq):
    n_tile = seq // GLA_TILE

    def tok(width):
        return pl.BlockSpec((GLA_TILE, width), lambda b, t: (b * n_tile + t, 0))

    return pl.pallas_call(
        _gla_kernel,
        grid=(batch, n_tile),
        in_specs=[tok(GLA_K_WIDTH), tok(GLA_K_WIDTH), tok(GLA_V_WIDTH), tok(GLA_K_WIDTH)],
        out_specs=tok(GLA_V_WIDTH),
        out_shape=jax.ShapeDtypeStruct((batch * seq, GLA_V_WIDTH), jnp.bfloat16),
        scratch_shapes=[pltpu.VMEM((GLA_HEADS, GLA_VALUE_DIM, GLA_K_WIDTH), jnp.float32)],
        compiler_params=pltpu.CompilerParams(
            dimension_semantics=("arbitrary", "arbitrary"), vmem_limit_bytes=VMEM_LIMIT),
        name="gla",
    )(gq, gk, gv, la)


def _mixout_kernel(x_ref, om_ref, og_ref, gr_ref, mg_ref, gg_ref, wo_ref, post_ref, o_ref):
    om = _rms(om_ref[...].astype(jnp.float32), mg_ref[...]).astype(jnp.bfloat16)
    gate = _silu(gr_ref[...].astype(jnp.float32))
    og = og_ref[...].astype(jnp.float32)
    parts = [om]
    for h in range(GLA_HEADS):
        cols = slice(h * GLA_VALUE_DIM, (h + 1) * GLA_VALUE_DIM)
        parts.append((_rms(og[:, cols], gg_ref[:, cols]) * gate[:, cols]).astype(jnp.bfloat16))
    mix_in = jnp.concatenate(parts, axis=-1)
    mix = jnp.dot(mix_in, wo_ref[...], preferred_element_type=jnp.float32)
    o_ref[...] = x_ref[...] + _rms(mix, post_ref[...])


def _mixout(x, om, og, gr, mg, gg, wo, post_g):
    t = x.shape[0]

    def row(width):
        return pl.BlockSpec((OUT_TM, width), lambda i: (i, 0))

    return pl.pallas_call(
        _mixout_kernel,
        grid=(t // OUT_TM,),
        in_specs=[row(D_MODEL), row(MOBA_WIDTH), row(GLA_V_WIDTH), row(GLA_V_WIDTH),
                  _resident(mg.shape), _resident(gg.shape), _resident(wo.shape),
                  _resident(post_g.shape)],
        out_specs=row(D_MODEL),
        out_shape=jax.ShapeDtypeStruct((t, D_MODEL), jnp.float32),
        compiler_params=pltpu.CompilerParams(
            dimension_semantics=("arbitrary",), vmem_limit_bytes=VMEM_LIMIT),
        name="mixout",
    )(x, om, og, gr, mg, gg, wo, post_g)


def kernel(x, ffn1_pre_g, ffn1_w_gate, ffn1_w_up, ffn1_w_down, ffn1_post_g, mix_pre_g, w_in,
           gla_w_alpha_up, gla_b_alpha, moba_out_g, gla_out_g, w_out, mix_post_g, ffn2_pre_g,
           ffn2_w_gate, ffn2_w_up, ffn2_w_down, ffn2_post_g):
    batch, seq, d = x.shape
    assert d == D_MODEL and seq % MOBA_BLOCK == 0 and seq % GLA_TILE == 0
    assert (batch * seq) % FFN_TM == 0
    bf = jnp.bfloat16
    xt = x.reshape(batch * seq, d)
    for l in range(w_in.shape[0]):
        xt = _ffn(xt, ffn1_pre_g[l][None], ffn1_w_gate[l].astype(bf), ffn1_w_up[l].astype(bf),
                  ffn1_w_down[l].astype(bf), ffn1_post_g[l][None])

        n_qkv = 3 * MOBA_WIDTH
        n_gla = 2 * GLA_K_WIDTH + GLA_V_WIDTH
        w = w_in[l]
        wqkv = w[:, :n_qkv].astype(bf)
        wgla = jnp.concatenate(
            [w[:, n_qkv:n_qkv + n_gla], w[:, n_qkv + n_gla + GLA_GATE_RANK:]], axis=1).astype(bf)
        wa = jnp.pad(w[:, n_qkv + n_gla:n_qkv + n_gla + GLA_GATE_RANK],
                     ((0, 0), (0, GATE_PAD - GLA_GATE_RANK))).astype(bf)
        wup = jnp.pad(gla_w_alpha_up[l], ((0, GATE_PAD - GLA_GATE_RANK), (0, 0))).astype(bf)
        mq, mk, mv, kmean, gq, gk, gv, gr, la = _proj(
            xt, mix_pre_g[l][None], wqkv, wgla, wa, wup, gla_b_alpha[l][None])

        kmean = kmean.reshape(batch, seq // MOBA_BLOCK, MOBA_WIDTH)
        o_moba = _moba(mq, mk, mv, kmean, batch, seq)
        o_gla = _gla(gq, gk, gv, la, batch, seq)

        xt = _mixout(xt, o_moba, o_gla, gr, moba_out_g[l][None],
                     gla_out_g[l].reshape(1, GLA_V_WIDTH), w_out[l].astype(bf), mix_post_g[l][None])

        xt = _ffn(xt, ffn2_pre_g[l][None], ffn2_w_gate[l].astype(bf), ffn2_w_up[l].astype(bf),
                  ffn2_w_down[l].astype(bf), ffn2_post_g[l][None])
    return xt.reshape(batch, seq, d)
```

```python
import functools

import jax
import jax.numpy as jnp
from jax import lax
from jax.experimental import pallas as pl
from jax.experimental.pallas import tpu as pltpu

D_MODEL = 1024
D_FF = 2816
RMS_EPS = 1e-6

MOBA_WIDTH = 512
MOBA_HEAD_DIM = 64
MOBA_BLOCK = 256
MOBA_TOPK = 3
LOG2_E = 1.4426950408889634
MOBA_Q_SCALE = MOBA_HEAD_DIM ** -0.5 * LOG2_E

GLA_HEADS = 4
GLA_KEY_DIM = 64
GLA_VALUE_DIM = 128
GLA_K_WIDTH = GLA_HEADS * GLA_KEY_DIM
GLA_V_WIDTH = GLA_HEADS * GLA_VALUE_DIM
GLA_GATE_RANK = 16
GLA_GATE_TAU = 16.0
GLA_TILE = 256

LANES = 128
GATE_PAD = LANES
NEG_BIG = -1e30
SUM_ROWS = 16

FFN_TM = 512
FFN_TF = 256
VMEM_LIMIT = 56 * 1024 * 1024

_NT = (((1,), (1,)), ((), ()))


def _rms(x, g):
    return x * lax.rsqrt(jnp.mean(x * x, axis=-1, keepdims=True) + RMS_EPS) * g


def _silu(x):
    return x / (1.0 + jnp.exp(-x))


def _resident(shape):
    return pl.BlockSpec(shape, lambda *_: (0,) * len(shape), pipeline_mode=pl.Buffered(1))


def _ffn_tile(x, pre_ref, wg_ref, wu_ref, wd_ref, post_ref, h_ref):
    xn = _rms(x, pre_ref[...]).astype(jnp.bfloat16)
    for c in range(D_FF // FFN_TF):
        cols = slice(c * FFN_TF, (c + 1) * FFN_TF)
        g = jnp.dot(xn, wg_ref[:, cols], preferred_element_type=jnp.float32)
        u = jnp.dot(xn, wu_ref[:, cols], preferred_element_type=jnp.float32)
        h_ref[:, cols] = (_silu(g) * u).astype(jnp.bfloat16)
    f = jnp.dot(h_ref[...], wd_ref[...], preferred_element_type=jnp.float32)
    return x + 0.5 * _rms(f, post_ref[...])


def _proj_tile(x, g_ref, wqkv_ref, wgla_ref, wa_ref, wup_ref, ba_ref,
               mq_ref, mk_ref, mv_ref, kmean_ref, gq_ref, gk_ref, gv_ref, gr_ref, la_ref):
    h = _rms(x, g_ref[...]).astype(jnp.bfloat16)

    def proj(w_ref, lo, hi):
        return jnp.dot(h, w_ref[:, lo:hi], preferred_element_type=jnp.float32)

    w = MOBA_WIDTH
    mq_ref[...] = (proj(wqkv_ref, 0, w) * MOBA_Q_SCALE).astype(jnp.bfloat16)
    mk = proj(wqkv_ref, w, 2 * w)
    mk_ref[...] = mk.astype(jnp.bfloat16)
    for blk in range(FFN_TM // MOBA_BLOCK):
        rows = slice(blk * MOBA_BLOCK, (blk + 1) * MOBA_BLOCK)
        kmean_ref[blk] = jnp.mean(mk[rows], axis=0, keepdims=True)
    mv_ref[...] = proj(wqkv_ref, 2 * w, 3 * w).astype(jnp.bfloat16)

    kw, vw = GLA_K_WIDTH, GLA_V_WIDTH
    gq_ref[...] = proj(wgla_ref, 0, kw).astype(jnp.bfloat16)
    gk_ref[...] = proj(wgla_ref, kw, 2 * kw).astype(jnp.bfloat16)
    gv_ref[...] = proj(wgla_ref, 2 * kw, 2 * kw + vw).astype(jnp.bfloat16)
    gr_ref[...] = proj(wgla_ref, 2 * kw + vw, 2 * kw + 2 * vw).astype(jnp.bfloat16)

    ga = jnp.dot(h, wa_ref[...], preferred_element_type=jnp.float32)
    z = jnp.dot(ga.astype(jnp.bfloat16), wup_ref[...],
                preferred_element_type=jnp.float32) + ba_ref[...]
    log_sig = jnp.minimum(z, 0.0) - jnp.log(1.0 + jnp.exp(-jnp.abs(z)))
    la_ref[...] = log_sig * (LOG2_E / GLA_GATE_TAU)


def _mix_tile(x, om_ref, og_ref, gr_ref, mg_ref, gg_ref, wo_ref, post_ref):
    om = _rms(om_ref[...].astype(jnp.float32), mg_ref[...]).astype(jnp.bfloat16)
    gate = _silu(gr_ref[...].astype(jnp.float32))
    og = og_ref[...].astype(jnp.float32)
    parts = [om]
    for h in range(GLA_HEADS):
        cols = slice(h * GLA_VALUE_DIM, (h + 1) * GLA_VALUE_DIM)
        parts.append((_rms(og[:, cols], gg_ref[:, cols]) * gate[:, cols]).astype(jnp.bfloat16))
    mix_in = jnp.concatenate(parts, axis=-1)
    mix = jnp.dot(mix_in, wo_ref[...], preferred_element_type=jnp.float32)
    return x + _rms(mix, post_ref[...])


def _row(width):
    return pl.BlockSpec((FFN_TM, width), lambda i: (i, 0))


_ROW_PARAMS = dict(
    compiler_params=pltpu.CompilerParams(
        dimension_semantics=("arbitrary",), vmem_limit_bytes=VMEM_LIMIT))


def _ffn_proj_kernel(x_ref, pre_ref, wg_ref, wu_ref, wd_ref, post_ref,
                     g_ref, wqkv_ref, wgla_ref, wa_ref, wup_ref, ba_ref,
                     x1_ref, mq_ref, mk_ref, mv_ref, kmean_ref, gq_ref, gk_ref, gv_ref, gr_ref, la_ref,
                     h_ref):
    x1 = _ffn_tile(x_ref[...], pre_ref, wg_ref, wu_ref, wd_ref, post_ref, h_ref)
    x1_ref[...] = x1
    _proj_tile(x1, g_ref, wqkv_ref, wgla_ref, wa_ref, wup_ref, ba_ref,
               mq_ref, mk_ref, mv_ref, kmean_ref, gq_ref, gk_ref, gv_ref, gr_ref, la_ref)


def _ffn_proj(x, pre_g, wg, wu, wd, post_g, g, wqkv, wgla, wa, wup, ba):
    t = x.shape[0]
    nblk = FFN_TM // MOBA_BLOCK
    params = (pre_g, wg, wu, wd, post_g, g, wqkv, wgla, wa, wup, ba)

    def out(width, dtype=jnp.bfloat16):
        return jax.ShapeDtypeStruct((t, width), dtype)

    return pl.pallas_call(
        _ffn_proj_kernel,
        grid=(t // FFN_TM,),
        in_specs=[_row(D_MODEL)] + [_resident(p.shape) for p in params],
        out_specs=[_row(D_MODEL), _row(MOBA_WIDTH), _row(MOBA_WIDTH), _row(MOBA_WIDTH),
                   pl.BlockSpec((nblk, 1, MOBA_WIDTH), lambda i: (i, 0, 0)),
                   _row(GLA_K_WIDTH), _row(GLA_K_WIDTH), _row(GLA_V_WIDTH), _row(GLA_V_WIDTH),
                   _row(GLA_K_WIDTH)],
        out_shape=[out(D_MODEL, jnp.float32), out(MOBA_WIDTH), out(MOBA_WIDTH), out(MOBA_WIDTH),
                   jax.ShapeDtypeStruct((t // MOBA_BLOCK, 1, MOBA_WIDTH), jnp.float32),
                   out(GLA_K_WIDTH), out(GLA_K_WIDTH), out(GLA_V_WIDTH), out(GLA_V_WIDTH),
                   out(GLA_K_WIDTH, jnp.float32)],
        scratch_shapes=[pltpu.VMEM((FFN_TM, D_FF), jnp.bfloat16)],
        name="ffn_proj", **_ROW_PARAMS,
    )(x, *params)


def _mix_ffn_kernel(x_ref, om_ref, og_ref, gr_ref, mg_ref, gg_ref, wo_ref, mpost_ref,
                    pre_ref, wg_ref, wu_ref, wd_ref, post_ref, o_ref, h_ref):
    x2 = _mix_tile(x_ref[...], om_ref, og_ref, gr_ref, mg_ref, gg_ref, wo_ref, mpost_ref)
    o_ref[...] = _ffn_tile(x2, pre_ref, wg_ref, wu_ref, wd_ref, post_ref, h_ref)


def _mix_ffn(x, om, og, gr, mg, gg, wo, mpost_g, pre_g, wg, wu, wd, post_g):
    t = x.shape[0]
    params = (mg, gg, wo, mpost_g, pre_g, wg, wu, wd, post_g)
    return pl.pallas_call(
        _mix_ffn_kernel,
        grid=(t // FFN_TM,),
        in_specs=[_row(D_MODEL), _row(MOBA_WIDTH), _row(GLA_V_WIDTH), _row(GLA_V_WIDTH)]
        + [_resident(p.shape) for p in params],
        out_specs=_row(D_MODEL),
        out_shape=jax.ShapeDtypeStruct((t, D_MODEL), jnp.float32),
        scratch_shapes=[pltpu.VMEM((FFN_TM, D_FF), jnp.bfloat16)],
        name="mix_ffn", **_ROW_PARAMS,
    )(x, om, og, gr, *params)


def _moba_kernel(q_ref, k_ref, v_ref, kmean_ref, o_ref, kaug_ref, *, n_blk):
    blk = MOBA_BLOCK
    dh = MOBA_HEAD_DIM
    seq = n_blk * blk
    q_t = q_ref[...].astype(jnp.float32).T
    v_t = jnp.concatenate([v_ref[...].astype(jnp.float32).T, jnp.ones((SUM_ROWS, seq), jnp.float32)],
                          axis=0).astype(jnp.bfloat16)
    kmean = kmean_ref[0]
    km_hi = kmean.astype(jnp.bfloat16)
    km_lo = (kmean - km_hi.astype(jnp.float32)).astype(jnp.bfloat16)

    kaug_ref[:, 0:LANES] = k_ref[...]
    key_blk = lax.broadcasted_iota(jnp.int32, (seq, LANES), 0) // blk
    hot_col = lax.broadcasted_iota(jnp.int32, (seq, LANES), 1)
    kaug_ref[:, LANES:2 * LANES] = jnp.where(key_blk == hot_col, 1.0, 0.0).astype(jnp.bfloat16)

    n_head = LANES // dh
    wide = n_head * blk
    feat = lax.broadcasted_iota(jnp.int32, q_t.shape, 0)
    q_heads = [jnp.where((feat >= h * dh) & (feat < (h + 1) * dh), q_t, 0.0) for h in range(n_head)]
    blk_row = lax.broadcasted_iota(jnp.int32, (n_blk, wide), 0)
    key_i = lax.broadcasted_iota(jnp.int32, (blk, wide), 0)
    qry_i = lax.broadcasted_iota(jnp.int32, (blk, wide), 1) % blk
    causal = key_i <= qry_i
    bias_pad = jnp.zeros((LANES - n_blk, wide), jnp.float32)

    for i in range(n_blk):
        qi_f = jnp.concatenate([qh[:, i * blk:(i + 1) * blk] for qh in q_heads], axis=1)
        qi_t = qi_f.astype(jnp.bfloat16)
        n_keys = (i + 1) * blk
        if i > MOBA_TOPK:
            gate = (jnp.dot(km_hi, qi_t, preferred_element_type=jnp.float32)
                    + jnp.dot(km_lo, qi_t, preferred_element_type=jnp.float32))
            rank = jnp.zeros((n_blk, wide), jnp.float32)
            for j in range(i):
                gj = gate[j:j + 1, :]
                beats = (gj > gate) | ((gj == gate) & (blk_row > j))
                rank = rank + jnp.where(beats, 1.0, 0.0)
            keep = (rank < float(MOBA_TOPK)) | (blk_row >= i)
            bias = jnp.where(keep, 0.0, NEG_BIG)
            q_aug = jnp.concatenate([qi_f, bias, bias_pad], axis=0).astype(jnp.bfloat16)
            s = jnp.dot(kaug_ref[0:n_keys, :], q_aug, preferred_element_type=jnp.float32)
        else:
            s = jnp.dot(k_ref[0:n_keys, :], qi_t, preferred_element_type=jnp.float32)
        s_own = jnp.where(causal, s[i * blk:n_keys, :], NEG_BIG)
        sm = s_own if i == 0 else jnp.concatenate([s[0:i * blk, :], s_own], axis=0)
        m = jnp.max(sm, axis=0, keepdims=True)
        p = jnp.exp2(sm - m).astype(jnp.bfloat16)
        o_t = jnp.dot(v_t[:, 0:n_keys], p, preferred_element_type=jnp.float32)
        o_t = o_t[0:LANES, :] * (1.0 / o_t[LANES:LANES + 1, :])
        o_own = jnp.concatenate([o_t[h * dh:(h + 1) * dh, h * blk:(h + 1) * blk]
                                 for h in range(n_head)], axis=0)
        o_ref[i * blk:(i + 1) * blk, :] = o_own.T.astype(o_ref.dtype)


def _moba(mq, mk, mv, kmean, batch, seq):
    n_blk = seq // MOBA_BLOCK
    n_col = MOBA_WIDTH // LANES
    tok = pl.BlockSpec((seq, LANES), lambda b, c: (b, c))
    return pl.pallas_call(
        functools.partial(_moba_kernel, n_blk=n_blk),
        grid=(batch, n_col),
        in_specs=[tok, tok, tok, pl.BlockSpec((1, n_blk, LANES), lambda b, c: (b, 0, c))],
        out_specs=tok,
        out_shape=jax.ShapeDtypeStruct((batch * seq, MOBA_WIDTH), jnp.bfloat16),
        scratch_shapes=[pltpu.VMEM((seq, 2 * LANES), jnp.bfloat16)],
        compiler_params=pltpu.CompilerParams(
            dimension_semantics=("arbitrary", "arbitrary"), vmem_limit_bytes=VMEM_LIMIT),
        name="moba",
    )(mq, mk, mv, kmean)


def _gla_kernel(q_ref, k_ref, v_ref, la_ref, o_ref, st_ref):
    c = GLA_TILE
    dk, dv = GLA_KEY_DIM, GLA_VALUE_DIM

    @pl.when(pl.program_id(1) == 0)
    def _():
        st_ref[...] = jnp.zeros_like(st_ref)

    la = la_ref[...]
    row = lax.broadcasted_iota(jnp.int32, la.shape, 0)
    lane = lax.broadcasted_iota(jnp.int32, la.shape, 1)
    odd_head = (lane & dk) != 0
    st_lane = lax.broadcasted_iota(jnp.int32, (dv, GLA_K_WIDTH), 1)

    b = la
    step = 1
    while step < c:
        b = b + jnp.where(row >= step, pltpu.roll(b, step, 0), 0.0)
        step *= 2

    qf = q_ref[...].astype(jnp.float32) * (dk ** -0.5)
    kf = k_ref[...].astype(jnp.float32)

    sub = c // 2
    t_i = lax.broadcasted_iota(jnp.int32, (sub, sub), 0)
    s_i = lax.broadcasted_iota(jnp.int32, (sub, sub), 1)
    level = jnp.where(t_i >= s_i, 31 - lax.clz(t_i ^ s_i), -2)

    def split_heads(x):
        return (jnp.where(odd_head, 0.0, x).astype(jnp.bfloat16),
                jnp.where(odd_head, x, 0.0).astype(jnp.bfloat16))

    def level_products(q_lvl, k_lvl, row_blk, col_blk):
        q_even, q_odd = split_heads(q_lvl)
        k_b = k_lvl.astype(jnp.bfloat16)
        rows = slice(row_blk * sub, (row_blk + 1) * sub)
        keys = slice(col_blk * sub, (col_blk + 1) * sub)
        out = []
        for h in range(GLA_HEADS):
            cols = slice((h // 2) * LANES, (h // 2 + 1) * LANES)
            q_h = (q_odd if h % 2 else q_even)[rows, cols]
            out.append(lax.dot_general(q_h, k_b[keys, cols], _NT, preferred_element_type=jnp.float32))
        return out

    def add_level(acc, q_lvl, k_lvl, lvl):
        hit = level == lvl
        for r in range(2):
            for h, a in enumerate(level_products(q_lvl, k_lvl, r, r)):
                acc[r][h] = jnp.where(hit, a, acc[r][h])
        return acc

    acc = [[jnp.zeros((sub, sub), jnp.float32) for _ in range(GLA_HEADS)] for _ in range(2)]
    acc = add_level(acc, qf, kf, -1)
    a_cross = None
    b_end = b
    half = 1
    lvl = 0
    while half < c:
        b_before = pltpu.roll(b_end, half, 0)
        q_lvl = qf * jnp.exp2(jnp.minimum(b - b_before, 0.0))
        k_lvl = kf * jnp.exp2(b_end - b)
        if half < sub:
            acc = add_level(acc, q_lvl, k_lvl, lvl)
        else:
            a_cross = level_products(q_lvl, k_lvl, 1, 0)
        upper = (row & half) != 0
        b_end = jnp.where(upper, b_end, pltpu.roll(b_end, c - half, 0))
        half *= 2
        lvl += 1
    q_in = (qf * jnp.exp2(b)).astype(jnp.bfloat16)
    k_out = (kf * jnp.exp2(b_end - b)).astype(jnp.bfloat16)
    tile_decay = jnp.exp2(b_end[0:1, :])

    for h in range(GLA_HEADS):
        cols = slice((h // 2) * LANES, (h // 2 + 1) * LANES)
        v_h = v_ref[:, h * dv:(h + 1) * dv]
        st = st_ref[h]
        o_inter = lax.dot_general(q_in[:, cols], st[:, cols].astype(jnp.bfloat16), _NT,
                                  preferred_element_type=jnp.float32)
        a_lo = acc[0][h].astype(jnp.bfloat16)
        a_hi = jnp.concatenate([a_cross[h], acc[1][h]], axis=1).astype(jnp.bfloat16)
        o_intra = jnp.concatenate(
            [jnp.dot(a_lo, v_h[0:sub, :], preferred_element_type=jnp.float32),
             jnp.dot(a_hi, v_h, preferred_element_type=jnp.float32)], axis=0)
        o_ref[:, h * dv:(h + 1) * dv] = (o_inter + o_intra).astype(o_ref.dtype)
        v_t = v_h.astype(jnp.float32).T.astype(jnp.bfloat16)
        upd = jnp.dot(v_t, k_out, preferred_element_type=jnp.float32)
        own = (st_lane >= h * dk) & (st_lane < (h + 1) * dk)
        st_ref[h] = st * tile_decay + jnp.where(own, upd, 0.0)


def _gla(gq, gk, gv, la, batch, seq):
    n_tile = seq // GLA_TILE

    def tok(width):
        return pl.BlockSpec((GLA_TILE, width), lambda b, t: (b * n_tile + t, 0))

    return pl.pallas_call(
        _gla_kernel,
        grid=(batch, n_tile),
        in_specs=[tok(GLA_K_WIDTH), tok(GLA_K_WIDTH), tok(GLA_V_WIDTH), tok(GLA_K_WIDTH)],
        out_specs=tok(GLA_V_WIDTH),
        out_shape=jax.ShapeDtypeStruct((batch * seq, GLA_V_WIDTH), jnp.bfloat16),
        scratch_shapes=[pltpu.VMEM((GLA_HEADS, GLA_VALUE_DIM, GLA_K_WIDTH), jnp.float32)],
        compiler_params=pltpu.CompilerParams(
            dimension_semantics=("arbitrary", "arbitrary"), vmem_limit_bytes=VMEM_LIMIT),
        name="gla",
    )(gq, gk, gv, la)


def kernel(x, ffn1_pre_g, ffn1_w_gate, ffn1_w_up, ffn1_w_down, ffn1_post_g, mix_pre_g, w_in,
           gla_w_alpha_up, gla_b_alpha, moba_out_g, gla_out_g, w_out, mix_post_g, ffn2_pre_g,
           ffn2_w_gate, ffn2_w_up, ffn2_w_down, ffn2_post_g):
    batch, seq, d = x.shape
    assert d == D_MODEL and seq % MOBA_BLOCK == 0 and seq % GLA_TILE == 0
    assert (batch * seq) % FFN_TM == 0
    bf = jnp.bfloat16
    xt = x.reshape(batch * seq, d)
    for l in range(w_in.shape[0]):
        n_qkv = 3 * MOBA_WIDTH
        n_gla = 2 * GLA_K_WIDTH + GLA_V_WIDTH
        w = w_in[l]
        wqkv = w[:, :n_qkv].astype(bf)
        wgla = jnp.concatenate(
            [w[:, n_qkv:n_qkv + n_gla], w[:, n_qkv + n_gla + GLA_GATE_RANK:]], axis=1).astype(bf)
        wa = jnp.pad(w[:, n_qkv + n_gla:n_qkv + n_gla + GLA_GATE_RANK],
                     ((0, 0), (0, GATE_PAD - GLA_GATE_RANK))).astype(bf)
        wup = jnp.pad(gla_w_alpha_up[l], ((0, GATE_PAD - GLA_GATE_RANK), (0, 0))).astype(bf)
        xt, mq, mk, mv, kmean, gq, gk, gv, gr, la = _ffn_proj(
            xt, ffn1_pre_g[l][None], ffn1_w_gate[l].astype(bf), ffn1_w_up[l].astype(bf),
            ffn1_w_down[l].astype(bf), ffn1_post_g[l][None],
            mix_pre_g[l][None], wqkv, wgla, wa, wup, gla_b_alpha[l][None])

        kmean = kmean.reshape(batch, seq // MOBA_BLOCK, MOBA_WIDTH)
        o_moba = _moba(mq, mk, mv, kmean, batch, seq)
        o_gla = _gla(gq, gk, gv, la, batch, seq)

        xt = _mix_ffn(xt, o_moba, o_gla, gr, moba_out_g[l][None],
                      gla_out_g[l].reshape(1, GLA_V_WIDTH), w_out[l].astype(bf), mix_post_g[l][None],
                      ffn2_pre_g[l][None], ffn2_w_gate[l].astype(bf), ffn2_w_up[l].astype(bf),
                      ffn2_w_down[l].astype(bf), ffn2_post_g[l][None])
    return xt.reshape(batch, seq, d)
```

```python
import functools

import jax
import jax.numpy as jnp
from jax import lax
from jax.experimental import pallas as pl
from jax.experimental.pallas import tpu as pltpu

D_MODEL = 1024
D_FF = 2816
RMS_EPS = 1e-6

MOBA_WIDTH = 512
MOBA_HEAD_DIM = 64
MOBA_BLOCK = 256
MOBA_TOPK = 3
LOG2_E = 1.4426950408889634
MOBA_Q_SCALE = MOBA_HEAD_DIM ** -0.5 * LOG2_E

GLA_HEADS = 4
GLA_KEY_DIM = 64
GLA_VALUE_DIM = 128
GLA_K_WIDTH = GLA_HEADS * GLA_KEY_DIM
GLA_V_WIDTH = GLA_HEADS * GLA_VALUE_DIM
GLA_GATE_RANK = 16
GLA_GATE_TAU = 16.0
GLA_TILE = 256
GLA_STEP = 1024

LANES = 128
GATE_PAD = LANES
NEG_BIG = -1e30
SUM_ROWS = 16

FFN_TM = 512
FFN_TF = 256
MIX_TM = 1024
MIX_SUB = 256
VMEM_LIMIT = 56 * 1024 * 1024

_NT = (((1,), (1,)), ((), ()))


def _rms(x, g):
    return x * lax.rsqrt(jnp.mean(x * x, axis=-1, keepdims=True) + RMS_EPS) * g


def _silu(x):
    return x / (1.0 + jnp.exp(-x))


def _resident(shape):
    return pl.BlockSpec(shape, lambda *_: (0,) * len(shape), pipeline_mode=pl.Buffered(1))


def _interleave(stage_lists, lag):
    n_slots = max(len(st) + j * lag for j, st in enumerate(stage_lists))
    for k in range(n_slots):
        for j, st in enumerate(stage_lists):
            if 0 <= k - j * lag < len(st):
                st[k - j * lag]()


def _row(width, rows=FFN_TM):
    return pl.BlockSpec((rows, width), lambda i: (i, 0))


_ROW_PARAMS = dict(
    compiler_params=pltpu.CompilerParams(
        dimension_semantics=("arbitrary",), vmem_limit_bytes=VMEM_LIMIT))


def _ffn_proj_stages(r, x_ref, pre_ref, wg_ref, wu_ref, wd_ref, post_ref,
                     g_ref, wqkv_ref, wgla_ref, wa_ref, wup_ref, ba_ref,
                     x1_ref, mq_ref, mk_ref, mv_ref, kmean_ref, gq_ref, gk_ref, gv_ref, gr_ref, la_ref,
                     h_ref):
    rows = slice(r * MOBA_BLOCK, (r + 1) * MOBA_BLOCK)
    st = {}

    def prenorm():
        st["xn"] = _rms(x_ref[rows, :], pre_ref[...]).astype(jnp.bfloat16)

    def up(c):
        def run():
            cols = slice(c * FFN_TF, (c + 1) * FFN_TF)
            g = jnp.dot(st["xn"], wg_ref[:, cols], preferred_element_type=jnp.float32)
            u = jnp.dot(st["xn"], wu_ref[:, cols], preferred_element_type=jnp.float32)
            h_ref[rows, cols] = (_silu(g) * u).astype(jnp.bfloat16)
        return run

    def down():
        st["f"] = jnp.dot(h_ref[rows, :], wd_ref[...], preferred_element_type=jnp.float32)

    def mid():
        x1 = x_ref[rows, :] + 0.5 * _rms(st.pop("f"), post_ref[...])
        x1_ref[rows, :] = x1
        st["h"] = _rms(x1, g_ref[...]).astype(jnp.bfloat16)

    def proj(w_ref, lo, hi):
        return jnp.dot(st["h"], w_ref[:, lo:hi], preferred_element_type=jnp.float32)

    w, kw, vw = MOBA_WIDTH, GLA_K_WIDTH, GLA_V_WIDTH

    def moba_q():
        mq_ref[rows, :] = (proj(wqkv_ref, 0, w) * MOBA_Q_SCALE).astype(jnp.bfloat16)

    def moba_k():
        mk = proj(wqkv_ref, w, 2 * w)
        mk_ref[rows, :] = mk.astype(jnp.bfloat16)
        kmean_ref[r] = jnp.mean(mk, axis=0, keepdims=True)

    def moba_v():
        mv_ref[rows, :] = proj(wqkv_ref, 2 * w, 3 * w).astype(jnp.bfloat16)

    def gla_qk():
        gq_ref[rows, :] = proj(wgla_ref, 0, kw).astype(jnp.bfloat16)
        gk_ref[rows, :] = proj(wgla_ref, kw, 2 * kw).astype(jnp.bfloat16)

    def gla_v():
        gv_ref[rows, :] = proj(wgla_ref, 2 * kw, 2 * kw + vw).astype(jnp.bfloat16)

    def gla_gate():
        gr_ref[rows, :] = proj(wgla_ref, 2 * kw + vw, 2 * kw + 2 * vw).astype(jnp.bfloat16)

    def gla_decay():
        ga = jnp.dot(st["h"], wa_ref[...], preferred_element_type=jnp.float32)
        z = jnp.dot(ga.astype(jnp.bfloat16), wup_ref[...],
                    preferred_element_type=jnp.float32) + ba_ref[...]
        log_sig = jnp.minimum(z, 0.0) - jnp.log(1.0 + jnp.exp(-jnp.abs(z)))
        la_ref[rows, :] = log_sig * (LOG2_E / GLA_GATE_TAU)

    return ([prenorm] + [up(c) for c in range(D_FF // FFN_TF)] + [down, mid]
            + [moba_q, moba_k, moba_v, gla_qk, gla_v, gla_gate, gla_decay])


def _ffn_proj_kernel(*refs):
    _interleave([_ffn_proj_stages(r, *refs) for r in range(FFN_TM // MOBA_BLOCK)], lag=4)


def _ffn_proj(x, pre_g, wg, wu, wd, post_g, g, wqkv, wgla, wa, wup, ba):
    t = x.shape[0]
    nblk = FFN_TM // MOBA_BLOCK
    params = (pre_g, wg, wu, wd, post_g, g, wqkv, wgla, wa, wup, ba)

    def out(width, dtype=jnp.bfloat16):
        return jax.ShapeDtypeStruct((t, width), dtype)

    return pl.pallas_call(
        _ffn_proj_kernel,
        grid=(t // FFN_TM,),
        in_specs=[_row(D_MODEL)] + [_resident(p.shape) for p in params],
        out_specs=[_row(D_MODEL), _row(MOBA_WIDTH), _row(MOBA_WIDTH), _row(MOBA_WIDTH),
                   pl.BlockSpec((nblk, 1, MOBA_WIDTH), lambda i: (i, 0, 0)),
                   _row(GLA_K_WIDTH), _row(GLA_K_WIDTH), _row(GLA_V_WIDTH), _row(GLA_V_WIDTH),
                   _row(GLA_K_WIDTH)],
        out_shape=[out(D_MODEL, jnp.float32), out(MOBA_WIDTH), out(MOBA_WIDTH), out(MOBA_WIDTH),
                   jax.ShapeDtypeStruct((t // MOBA_BLOCK, 1, MOBA_WIDTH), jnp.float32),
                   out(GLA_K_WIDTH), out(GLA_K_WIDTH), out(GLA_V_WIDTH), out(GLA_V_WIDTH),
                   out(GLA_K_WIDTH, jnp.float32)],
        scratch_shapes=[pltpu.VMEM((FFN_TM, D_FF), jnp.bfloat16)],
        name="ffn_proj", **_ROW_PARAMS,
    )(x, *params)


def _mix_ffn_stages(rows, x_ref, om_ref, og_ref, gr_ref, mg_ref, gg_ref, wo_ref, mpost_ref,
                    pre_ref, wg_ref, wu_ref, wd_ref, post_ref, o_ref, h_ref):
    st = {}

    def mix_in():
        om = _rms(om_ref[rows, :].astype(jnp.float32), mg_ref[...]).astype(jnp.bfloat16)
        gate = _silu(gr_ref[rows, :].astype(jnp.float32))
        og = og_ref[rows, :].astype(jnp.float32)
        parts = [om]
        for h in range(GLA_HEADS):
            cols = slice(h * GLA_VALUE_DIM, (h + 1) * GLA_VALUE_DIM)
            parts.append((_rms(og[:, cols], gg_ref[:, cols]) * gate[:, cols]).astype(jnp.bfloat16))
        st["mix_in"] = jnp.concatenate(parts, axis=-1)

    def mix_out():
        st["mix"] = jnp.dot(st.pop("mix_in"), wo_ref[...], preferred_element_type=jnp.float32)

    def norms():
        x2 = x_ref[rows, :] + _rms(st.pop("mix"), mpost_ref[...])
        st["x2"] = x2
        st["xn"] = _rms(x2, pre_ref[...]).astype(jnp.bfloat16)

    def up(c):
        def run():
            cols = slice(c * FFN_TF, (c + 1) * FFN_TF)
            g = jnp.dot(st["xn"], wg_ref[:, cols], preferred_element_type=jnp.float32)
            u = jnp.dot(st["xn"], wu_ref[:, cols], preferred_element_type=jnp.float32)
            h_ref[rows, cols] = (_silu(g) * u).astype(jnp.bfloat16)
        return run

    def down():
        st["f"] = jnp.dot(h_ref[rows, :], wd_ref[...], preferred_element_type=jnp.float32)

    def finish():
        o_ref[rows, :] = st.pop("x2") + 0.5 * _rms(st.pop("f"), post_ref[...])

    return [mix_in, mix_out, norms] + [up(c) for c in range(D_FF // FFN_TF)] + [down, finish]


def _mix_ffn_kernel(*refs):
    _interleave([_mix_ffn_stages(slice(r * MIX_SUB, (r + 1) * MIX_SUB), *refs)
                 for r in range(MIX_TM // MIX_SUB)], lag=4)


def _mix_ffn(x, om, og, gr, mg, gg, wo, mpost_g, pre_g, wg, wu, wd, post_g):
    t = x.shape[0]
    params = (mg, gg, wo, mpost_g, pre_g, wg, wu, wd, post_g)
    return pl.pallas_call(
        _mix_ffn_kernel,
        grid=(t // MIX_TM,),
        in_specs=[_row(D_MODEL, MIX_TM), _row(MOBA_WIDTH, MIX_TM), _row(GLA_V_WIDTH, MIX_TM),
                  _row(GLA_V_WIDTH, MIX_TM)]
        + [_resident(p.shape) for p in params],
        out_specs=_row(D_MODEL, MIX_TM),
        out_shape=jax.ShapeDtypeStruct((t, D_MODEL), jnp.float32),
        scratch_shapes=[pltpu.VMEM((MIX_TM, D_FF), jnp.bfloat16)],
        name="mix_ffn", **_ROW_PARAMS,
    )(x, om, og, gr, *params)


def _moba_kernel(q_ref, k_ref, v_ref, kmean_ref, o_ref, kaug_ref, *, n_blk):
    blk = MOBA_BLOCK
    dh = MOBA_HEAD_DIM
    seq = n_blk * blk
    q_t = q_ref[...].astype(jnp.float32).T
    v_t = jnp.concatenate([v_ref[...].astype(jnp.float32).T, jnp.ones((SUM_ROWS, seq), jnp.float32)],
                          axis=0).astype(jnp.bfloat16)
    kmean = kmean_ref[0]
    km_hi = kmean.astype(jnp.bfloat16)
    km_lo = (kmean - km_hi.astype(jnp.float32)).astype(jnp.bfloat16)

    kaug_ref[:, 0:LANES] = k_ref[...]
    key_blk = lax.broadcasted_iota(jnp.int32, (seq, LANES), 0) // blk
    hot_col = lax.broadcasted_iota(jnp.int32, (seq, LANES), 1)
    kaug_ref[:, LANES:2 * LANES] = jnp.where(key_blk == hot_col, 1.0, 0.0).astype(jnp.bfloat16)

    n_head = LANES // dh
    wide = n_head * blk
    feat = lax.broadcasted_iota(jnp.int32, q_t.shape, 0)
    q_heads = [jnp.where((feat >= h * dh) & (feat < (h + 1) * dh), q_t, 0.0) for h in range(n_head)]
    blk_row = lax.broadcasted_iota(jnp.int32, (n_blk, wide), 0)
    key_i = lax.broadcasted_iota(jnp.int32, (blk, wide), 0)
    qry_i = lax.broadcasted_iota(jnp.int32, (blk, wide), 1) % blk
    causal = key_i <= qry_i
    bias_pad = jnp.zeros((LANES - n_blk, wide), jnp.float32)

    for i in range(n_blk):
        qi_f = jnp.concatenate([qh[:, i * blk:(i + 1) * blk] for qh in q_heads], axis=1)
        qi_t = qi_f.astype(jnp.bfloat16)
        n_keys = (i + 1) * blk
        if i > MOBA_TOPK:
            gate = (jnp.dot(km_hi, qi_t, preferred_element_type=jnp.float32)
                    + jnp.dot(km_lo, qi_t, preferred_element_type=jnp.float32))
            rank = jnp.zeros((n_blk, wide), jnp.float32)
            for j in range(i):
                gj = gate[j:j + 1, :]
                beats = (gj > gate) | ((gj == gate) & (blk_row > j))
                rank = rank + jnp.where(beats, 1.0, 0.0)
            keep = (rank < float(MOBA_TOPK)) | (blk_row >= i)
            bias = jnp.where(keep, 0.0, NEG_BIG)
            q_aug = jnp.concatenate([qi_f, bias, bias_pad], axis=0).astype(jnp.bfloat16)
            s = jnp.dot(kaug_ref[0:n_keys, :], q_aug, preferred_element_type=jnp.float32)
        else:
            s = jnp.dot(k_ref[0:n_keys, :], qi_t, preferred_element_type=jnp.float32)
        s_own = jnp.where(causal, s[i * blk:n_keys, :], NEG_BIG)
        sm = s_own if i == 0 else jnp.concatenate([s[0:i * blk, :], s_own], axis=0)
        m = jnp.max(sm, axis=0, keepdims=True)
        p = jnp.exp2(sm - m).astype(jnp.bfloat16)
        o_t = jnp.dot(v_t[:, 0:n_keys], p, preferred_element_type=jnp.float32)
        o_t = o_t[0:LANES, :] * (1.0 / o_t[LANES:LANES + 1, :])
        o_own = jnp.concatenate([o_t[h * dh:(h + 1) * dh, h * blk:(h + 1) * blk]
                                 for h in range(n_head)], axis=0)
        o_ref[i * blk:(i + 1) * blk, :] = o_own.T.astype(o_ref.dtype)


def _moba(mq, mk, mv, kmean, batch, seq):
    n_blk = seq // MOBA_BLOCK
    n_col = MOBA_WIDTH // LANES
    tok = pl.BlockSpec((seq, LANES), lambda b, c: (b, c))
    return pl.pallas_call(
        functools.partial(_moba_kernel, n_blk=n_blk),
        grid=(batch, n_col),
        in_specs=[tok, tok, tok, pl.BlockSpec((1, n_blk, LANES), lambda b, c: (b, 0, c))],
        out_specs=tok,
        out_shape=jax.ShapeDtypeStruct((batch * seq, MOBA_WIDTH), jnp.bfloat16),
        scratch_shapes=[pltpu.VMEM((seq, 2 * LANES), jnp.bfloat16)],
        compiler_params=pltpu.CompilerParams(
            dimension_semantics=("arbitrary", "arbitrary"), vmem_limit_bytes=VMEM_LIMIT),
        name="moba",
    )(mq, mk, mv, kmean)


def _gla_kernel(q_ref, k_ref, v_ref, la_ref, o_ref, st_ref):
    @pl.when(pl.program_id(1) == 0)
    def _():
        st_ref[...] = jnp.zeros_like(st_ref)

    for r in range(GLA_STEP // GLA_TILE):
        rows = slice(r * GLA_TILE, (r + 1) * GLA_TILE)
        _gla_tile(q_ref[rows, :], k_ref[rows, :], v_ref.at[rows, :], la_ref[rows, :],
                  o_ref.at[rows, :], st_ref)


def _gla_tile(q, k, v_ref, la, o_ref, st_ref):
    c = GLA_TILE
    dk, dv = GLA_KEY_DIM, GLA_VALUE_DIM
    row = lax.broadcasted_iota(jnp.int32, la.shape, 0)
    lane = lax.broadcasted_iota(jnp.int32, la.shape, 1)
    odd_head = (lane & dk) != 0
    st_lane = lax.broadcasted_iota(jnp.int32, (dv, GLA_K_WIDTH), 1)

    b = la
    step = 1
    while step < c:
        b = b + jnp.where(row >= step, pltpu.roll(b, step, 0), 0.0)
        step *= 2

    qf = q.astype(jnp.float32) * (dk ** -0.5)
    kf = k.astype(jnp.float32)

    sub = c // 2
    t_i = lax.broadcasted_iota(jnp.int32, (sub, sub), 0)
    s_i = lax.broadcasted_iota(jnp.int32, (sub, sub), 1)
    level = jnp.where(t_i >= s_i, 31 - lax.clz(t_i ^ s_i), -2)

    def split_heads(x):
        return (jnp.where(odd_head, 0.0, x).astype(jnp.bfloat16),
                jnp.where(odd_head, x, 0.0).astype(jnp.bfloat16))

    def level_products(q_lvl, k_lvl, row_blk, col_blk):
        q_even, q_odd = split_heads(q_lvl)
        k_b = k_lvl.astype(jnp.bfloat16)
        rows = slice(row_blk * sub, (row_blk + 1) * sub)
        keys = slice(col_blk * sub, (col_blk + 1) * sub)
        out = []
        for h in range(GLA_HEADS):
            cols = slice((h // 2) * LANES, (h // 2 + 1) * LANES)
            q_h = (q_odd if h % 2 else q_even)[rows, cols]
            out.append(lax.dot_general(q_h, k_b[keys, cols], _NT, preferred_element_type=jnp.float32))
        return out

    def add_level(acc, q_lvl, k_lvl, lvl):
        hit = level == lvl
        for r in range(2):
            for h, a in enumerate(level_products(q_lvl, k_lvl, r, r)):
                acc[r][h] = jnp.where(hit, a, acc[r][h])
        return acc

    acc = [[jnp.zeros((sub, sub), jnp.float32) for _ in range(GLA_HEADS)] for _ in range(2)]
    acc = add_level(acc, qf, kf, -1)
    a_cross = None
    b_end = b
    half = 1
    lvl = 0
    while half < c:
        b_before = pltpu.roll(b_end, half, 0)
        q_lvl = qf * jnp.exp2(jnp.minimum(b - b_before, 0.0))
        k_lvl = kf * jnp.exp2(b_end - b)
        if half < sub:
            acc = add_level(acc, q_lvl, k_lvl, lvl)
        else:
            a_cross = level_products(q_lvl, k_lvl, 1, 0)
        upper = (row & half) != 0
        b_end = jnp.where(upper, b_end, pltpu.roll(b_end, c - half, 0))
        half *= 2
        lvl += 1
    q_in = (qf * jnp.exp2(b)).astype(jnp.bfloat16)
    k_out = (kf * jnp.exp2(b_end - b)).astype(jnp.bfloat16)
    tile_decay = jnp.exp2(b_end[0:1, :])

    for h in range(GLA_HEADS):
        cols = slice((h // 2) * LANES, (h // 2 + 1) * LANES)
        v_h = v_ref[:, h * dv:(h + 1) * dv]
        st = st_ref[h]
        o_inter = lax.dot_general(q_in[:, cols], st[:, cols].astype(jnp.bfloat16), _NT,
                                  preferred_element_type=jnp.float32)
        a_lo = acc[0][h].astype(jnp.bfloat16)
        a_hi = jnp.concatenate([a_cross[h], acc[1][h]], axis=1).astype(jnp.bfloat16)
        o_intra = jnp.concatenate(
            [jnp.dot(a_lo, v_h[0:sub, :], preferred_element_type=jnp.float32),
             jnp.dot(a_hi, v_h, preferred_element_type=jnp.float32)], axis=0)
        o_ref[:, h * dv:(h + 1) * dv] = (o_inter + o_intra).astype(o_ref.dtype)
        v_t = v_h.astype(jnp.float32).T.astype(jnp.bfloat16)
        upd = jnp.dot(v_t, k_out, preferred_element_type=jnp.float32)
        own = (st_lane >= h * dk) & (st_lane < (h + 1) * dk)
        st_ref[h] = st * tile_decay + jnp.where(own, upd, 0.0)


def _gla(gq, gk, gv, la, batch, seq):
    n_tile = seq // GLA_STEP

    def tok(width):
        return pl.BlockSpec((GLA_STEP, width), lambda b, t: (b * n_tile + t, 0))

    return pl.pallas_call(
        _gla_kernel,
        grid=(batch, n_tile),
        in_specs=[tok(GLA_K_WIDTH), tok(GLA_K_WIDTH), tok(GLA_V_WIDTH), tok(GLA_K_WIDTH)],
        out_specs=tok(GLA_V_WIDTH),
        out_shape=jax.ShapeDtypeStruct((batch * seq, GLA_V_WIDTH), jnp.bfloat16),
        scratch_shapes=[pltpu.VMEM((GLA_HEADS, GLA_VALUE_DIM, GLA_K_WIDTH), jnp.float32)],
        compiler_params=pltpu.CompilerParams(
            dimension_semantics=("arbitrary", "arbitrary"), vmem_limit_bytes=VMEM_LIMIT),
        name="gla",
    )(gq, gk, gv, la)


def kernel(x, ffn1_pre_g, ffn1_w_gate, ffn1_w_up, ffn1_w_down, ffn1_post_g, mix_pre_g, w_in,
           gla_w_alpha_up, gla_b_alpha, moba_out_g, gla_out_g, w_out, mix_post_g, ffn2_pre_g,
           ffn2_w_gate, ffn2_w_up, ffn2_w_down, ffn2_post_g):
    batch, seq, d = x.shape
    assert d == D_MODEL and seq % MOBA_BLOCK == 0 and seq % GLA_STEP == 0
    assert (batch * seq) % FFN_TM == 0 and (batch * seq) % MIX_TM == 0
    bf = jnp.bfloat16
    xt = x.reshape(batch * seq, d)
    for l in range(w_in.shape[0]):
        n_qkv = 3 * MOBA_WIDTH
        n_gla = 2 * GLA_K_WIDTH + GLA_V_WIDTH
        w = w_in[l]
        wqkv = w[:, :n_qkv].astype(bf)
        wgla = jnp.concatenate(
            [w[:, n_qkv:n_qkv + n_gla], w[:, n_qkv + n_gla + GLA_GATE_RANK:]], axis=1).astype(bf)
        wa = jnp.pad(w[:, n_qkv + n_gla:n_qkv + n_gla + GLA_GATE_RANK],
                     ((0, 0), (0, GATE_PAD - GLA_GATE_RANK))).astype(bf)
        wup = jnp.pad(gla_w_alpha_up[l], ((0, GATE_PAD - GLA_GATE_RANK), (0, 0))).astype(bf)
        xt, mq, mk, mv, kmean, gq, gk, gv, gr, la = _ffn_proj(
            xt, ffn1_pre_g[l][None], ffn1_w_gate[l].astype(bf), ffn1_w_up[l].astype(bf),
            ffn1_w_down[l].astype(bf), ffn1_post_g[l][None],
            mix_pre_g[l][None], wqkv, wgla, wa, wup, gla_b_alpha[l][None])

        kmean = kmean.reshape(batch, seq // MOBA_BLOCK, MOBA_WIDTH)
        o_moba = _moba(mq, mk, mv, kmean, batch, seq)
        o_gla = _gla(gq, gk, gv, la, batch, seq)

        xt = _mix_ffn(xt, o_moba, o_gla, gr, moba_out_g[l][None],
                      gla_out_g[l].reshape(1, GLA_V_WIDTH), w_out[l].astype(bf), mix_post_g[l][None],
                      ffn2_pre_g[l][None], ffn2_w_gate[l].astype(bf), ffn2_w_up[l].astype(bf),
                      ffn2_w_down[l].astype(bf), ffn2_post_g[l][None])
    return xt.reshape(batch, seq, d)
```

```python
import functools

import jax
import jax.numpy as jnp
from jax import lax
from jax.experimental import pallas as pl
from jax.experimental.pallas import tpu as pltpu

D_MODEL = 1024
D_FF = 2816
RMS_EPS = 1e-6

MOBA_WIDTH = 512
MOBA_HEAD_DIM = 64
MOBA_BLOCK = 256
MOBA_TOPK = 3
MOBA_COLS = 2
LOG2_E = 1.4426950408889634
MOBA_Q_SCALE = MOBA_HEAD_DIM ** -0.5 * LOG2_E

GLA_HEADS = 4
GLA_KEY_DIM = 64
GLA_VALUE_DIM = 128
GLA_K_WIDTH = GLA_HEADS * GLA_KEY_DIM
GLA_V_WIDTH = GLA_HEADS * GLA_VALUE_DIM
GLA_GATE_RANK = 16
GLA_GATE_TAU = 16.0
GLA_TILE = 256
GLA_STEP = 1024

LANES = 128
GATE_PAD = LANES
NEG_BIG = -1e30
SUM_ROWS = 16

FFN_TM = 512
FFN_TF = 256
MIX_TM = 1024
MIX_SUB = 256
VMEM_LIMIT = 56 * 1024 * 1024

_NT = (((1,), (1,)), ((), ()))


def _rms(x, g):
    return x * lax.rsqrt(jnp.mean(x * x, axis=-1, keepdims=True) + RMS_EPS) * g


def _silu(x):
    return x / (1.0 + jnp.exp(-x))


def _resident(shape):
    return pl.BlockSpec(shape, lambda *_: (0,) * len(shape), pipeline_mode=pl.Buffered(1))


def _interleave(stage_lists, lag):
    n_slots = max(len(st) + j * lag for j, st in enumerate(stage_lists))
    for k in range(n_slots):
        for j, st in enumerate(stage_lists):
            if 0 <= k - j * lag < len(st):
                st[k - j * lag]()


def _row(width, rows=FFN_TM):
    return pl.BlockSpec((rows, width), lambda i: (i, 0))


_ROW_PARAMS = dict(
    compiler_params=pltpu.CompilerParams(
        dimension_semantics=("arbitrary",), vmem_limit_bytes=VMEM_LIMIT))


def _ffn_proj_stages(r, x_ref, pre_ref, wg_ref, wu_ref, wd_ref, post_ref,
                     g_ref, wqkv_ref, wgla_ref, wa_ref, wup_ref, ba_ref,
                     x1_ref, mq_ref, mk_ref, mv_ref, kmean_ref, gq_ref, gk_ref, gv_ref, gr_ref, la_ref,
                     h_ref):
    rows = slice(r * MOBA_BLOCK, (r + 1) * MOBA_BLOCK)
    st = {}

    def prenorm():
        st["xn"] = _rms(x_ref[rows, :], pre_ref[...]).astype(jnp.bfloat16)

    def up(c):
        def run():
            cols = slice(c * FFN_TF, (c + 1) * FFN_TF)
            g = jnp.dot(st["xn"], wg_ref[:, cols], preferred_element_type=jnp.float32)
            u = jnp.dot(st["xn"], wu_ref[:, cols], preferred_element_type=jnp.float32)
            h_ref[rows, cols] = (_silu(g) * u).astype(jnp.bfloat16)
        return run

    def down():
        st["f"] = jnp.dot(h_ref[rows, :], wd_ref[...], preferred_element_type=jnp.float32)

    def mid():
        x1 = x_ref[rows, :] + 0.5 * _rms(st.pop("f"), post_ref[...])
        x1_ref[rows, :] = x1
        st["h"] = _rms(x1, g_ref[...]).astype(jnp.bfloat16)

    def proj(w_ref, lo, hi):
        return jnp.dot(st["h"], w_ref[:, lo:hi], preferred_element_type=jnp.float32)

    w, kw, vw = MOBA_WIDTH, GLA_K_WIDTH, GLA_V_WIDTH

    def moba_q():
        mq_ref[rows, :] = (proj(wqkv_ref, 0, w) * MOBA_Q_SCALE).astype(jnp.bfloat16)

    def moba_k():
        mk = proj(wqkv_ref, w, 2 * w)
        mk_ref[rows, :] = mk.astype(jnp.bfloat16)
        kmean_ref[r] = jnp.mean(mk, axis=0, keepdims=True)

    def moba_v():
        mv_ref[rows, :] = proj(wqkv_ref, 2 * w, 3 * w).astype(jnp.bfloat16)

    def gla_qk():
        gq_ref[rows, :] = proj(wgla_ref, 0, kw).astype(jnp.bfloat16)
        gk_ref[rows, :] = proj(wgla_ref, kw, 2 * kw).astype(jnp.bfloat16)

    def gla_v():
        gv_ref[rows, :] = proj(wgla_ref, 2 * kw, 2 * kw + vw).astype(jnp.bfloat16)

    def gla_gate():
        gr_ref[rows, :] = proj(wgla_ref, 2 * kw + vw, 2 * kw + 2 * vw).astype(jnp.bfloat16)

    def gla_decay():
        ga = jnp.dot(st["h"], wa_ref[...], preferred_element_type=jnp.float32)
        z = jnp.dot(ga.astype(jnp.bfloat16), wup_ref[...],
                    preferred_element_type=jnp.float32) + ba_ref[...]
        log_sig = jnp.minimum(z, 0.0) - jnp.log(1.0 + jnp.exp(-jnp.abs(z)))
        la_ref[rows, :] = log_sig * (LOG2_E / GLA_GATE_TAU)

    return ([prenorm] + [up(c) for c in range(D_FF // FFN_TF)] + [down, mid]
            + [moba_q, moba_k, moba_v, gla_qk, gla_v, gla_gate, gla_decay])


def _ffn_proj_kernel(*refs):
    _interleave([_ffn_proj_stages(r, *refs) for r in range(FFN_TM // MOBA_BLOCK)], lag=4)


def _ffn_proj(x, pre_g, wg, wu, wd, post_g, g, wqkv, wgla, wa, wup, ba):
    t = x.shape[0]
    nblk = FFN_TM // MOBA_BLOCK
    params = (pre_g, wg, wu, wd, post_g, g, wqkv, wgla, wa, wup, ba)

    def out(width, dtype=jnp.bfloat16):
        return jax.ShapeDtypeStruct((t, width), dtype)

    return pl.pallas_call(
        _ffn_proj_kernel,
        grid=(t // FFN_TM,),
        in_specs=[_row(D_MODEL)] + [_resident(p.shape) for p in params],
        out_specs=[_row(D_MODEL), _row(MOBA_WIDTH), _row(MOBA_WIDTH), _row(MOBA_WIDTH),
                   pl.BlockSpec((nblk, 1, MOBA_WIDTH), lambda i: (i, 0, 0)),
                   _row(GLA_K_WIDTH), _row(GLA_K_WIDTH), _row(GLA_V_WIDTH), _row(GLA_V_WIDTH),
                   _row(GLA_K_WIDTH)],
        out_shape=[out(D_MODEL, jnp.float32), out(MOBA_WIDTH), out(MOBA_WIDTH), out(MOBA_WIDTH),
                   jax.ShapeDtypeStruct((t // MOBA_BLOCK, 1, MOBA_WIDTH), jnp.float32),
                   out(GLA_K_WIDTH), out(GLA_K_WIDTH), out(GLA_V_WIDTH), out(GLA_V_WIDTH),
                   out(GLA_K_WIDTH, jnp.float32)],
        scratch_shapes=[pltpu.VMEM((FFN_TM, D_FF), jnp.bfloat16)],
        name="ffn_proj", **_ROW_PARAMS,
    )(x, *params)


def _mix_ffn_stages(rows, x_ref, om_ref, og_ref, gr_ref, mg_ref, gg_ref, wo_ref, mpost_ref,
                    pre_ref, wg_ref, wu_ref, wd_ref, post_ref, o_ref, h_ref):
    st = {}

    def mix_in():
        om = _rms(om_ref[rows, :].astype(jnp.float32), mg_ref[...]).astype(jnp.bfloat16)
        gate = _silu(gr_ref[rows, :].astype(jnp.float32))
        og = og_ref[rows, :].astype(jnp.float32)
        parts = [om]
        for h in range(GLA_HEADS):
            cols = slice(h * GLA_VALUE_DIM, (h + 1) * GLA_VALUE_DIM)
            parts.append((_rms(og[:, cols], gg_ref[:, cols]) * gate[:, cols]).astype(jnp.bfloat16))
        st["mix_in"] = jnp.concatenate(parts, axis=-1)

    def mix_out():
        st["mix"] = jnp.dot(st.pop("mix_in"), wo_ref[...], preferred_element_type=jnp.float32)

    def norms():
        x2 = x_ref[rows, :] + _rms(st.pop("mix"), mpost_ref[...])
        st["x2"] = x2
        st["xn"] = _rms(x2, pre_ref[...]).astype(jnp.bfloat16)

    def up(c):
        def run():
            cols = slice(c * FFN_TF, (c + 1) * FFN_TF)
            g = jnp.dot(st["xn"], wg_ref[:, cols], preferred_element_type=jnp.float32)
            u = jnp.dot(st["xn"], wu_ref[:, cols], preferred_element_type=jnp.float32)
            h_ref[rows, cols] = (_silu(g) * u).astype(jnp.bfloat16)
        return run

    def down():
        st["f"] = jnp.dot(h_ref[rows, :], wd_ref[...], preferred_element_type=jnp.float32)

    def finish():
        o_ref[rows, :] = st.pop("x2") + 0.5 * _rms(st.pop("f"), post_ref[...])

    return [mix_in, mix_out, norms] + [up(c) for c in range(D_FF // FFN_TF)] + [down, finish]


def _mix_ffn_kernel(*refs):
    _interleave([_mix_ffn_stages(slice(r * MIX_SUB, (r + 1) * MIX_SUB), *refs)
                 for r in range(MIX_TM // MIX_SUB)], lag=4)


def _mix_ffn(x, om, og, gr, mg, gg, wo, mpost_g, pre_g, wg, wu, wd, post_g):
    t = x.shape[0]
    params = (mg, gg, wo, mpost_g, pre_g, wg, wu, wd, post_g)
    return pl.pallas_call(
        _mix_ffn_kernel,
        grid=(t // MIX_TM,),
        in_specs=[_row(D_MODEL, MIX_TM), _row(MOBA_WIDTH, MIX_TM), _row(GLA_V_WIDTH, MIX_TM),
                  _row(GLA_V_WIDTH, MIX_TM)]
        + [_resident(p.shape) for p in params],
        out_specs=_row(D_MODEL, MIX_TM),
        out_shape=jax.ShapeDtypeStruct((t, D_MODEL), jnp.float32),
        scratch_shapes=[pltpu.VMEM((MIX_TM, D_FF), jnp.bfloat16)],
        name="mix_ffn", **_ROW_PARAMS,
    )(x, om, og, gr, *params)


def _moba_kernel(q_ref, k_ref, v_ref, kmean_ref, o_ref, kaug_ref, *, n_blk):
    blk = MOBA_BLOCK
    dh = MOBA_HEAD_DIM
    seq = n_blk * blk
    n_head = LANES // dh
    wide = n_head * blk

    key_blk = lax.broadcasted_iota(jnp.int32, (seq, LANES), 0) // blk
    hot_col = lax.broadcasted_iota(jnp.int32, (seq, LANES), 1)
    one_hot = jnp.where(key_blk == hot_col, 1.0, 0.0).astype(jnp.bfloat16)
    feat = lax.broadcasted_iota(jnp.int32, (LANES, blk), 0)
    head_rows = [(feat >= h * dh) & (feat < (h + 1) * dh) for h in range(n_head)]
    blk_row = lax.broadcasted_iota(jnp.int32, (n_blk, wide), 0)
    key_i = lax.broadcasted_iota(jnp.int32, (blk, wide), 0)
    qry_i = lax.broadcasted_iota(jnp.int32, (blk, wide), 1) % blk
    causal = key_i <= qry_i
    bias_pad = jnp.zeros((LANES - n_blk, wide), jnp.float32)
    ones = jnp.ones((SUM_ROWS, blk), jnp.float32)

    def column(c):
        lanes = slice(c * LANES, (c + 1) * LANES)
        kmean = kmean_ref[0][:, lanes]
        km_hi = kmean.astype(jnp.bfloat16)
        km_lo = (kmean - km_hi.astype(jnp.float32)).astype(jnp.bfloat16)
        kaug_ref[c, :, 0:LANES] = k_ref[:, lanes]
        kaug_ref[c, :, LANES:2 * LANES] = one_hot
        v_t = [None] * n_blk

        def value_block(j):
            if v_t[j] is None:
                v_j = v_ref[j * blk:(j + 1) * blk, lanes].astype(jnp.float32).T
                v_t[j] = jnp.concatenate([v_j, ones], axis=0).astype(jnp.bfloat16)
            return v_t[j]

        def section(i):
            st = {}
            n_keys = (i + 1) * blk

            def prep():
                q_i = q_ref[i * blk:(i + 1) * blk, lanes].astype(jnp.float32).T
                qi_f = jnp.concatenate([jnp.where(rows, q_i, 0.0) for rows in head_rows], axis=1)
                qi_t = qi_f.astype(jnp.bfloat16)
                if i > MOBA_TOPK:
                    gate = (jnp.dot(km_hi, qi_t, preferred_element_type=jnp.float32)
                            + jnp.dot(km_lo, qi_t, preferred_element_type=jnp.float32))
                    rank = jnp.zeros((n_blk, wide), jnp.float32)
                    for j in range(i):
                        gj = gate[j:j + 1, :]
                        beats = (gj > gate) | ((gj == gate) & (blk_row > j))
                        rank = rank + jnp.where(beats, 1.0, 0.0)
                    keep = (rank < float(MOBA_TOPK)) | (blk_row >= i)
                    bias = jnp.where(keep, 0.0, NEG_BIG)
                    st["q"] = jnp.concatenate([qi_f, bias, bias_pad], axis=0).astype(jnp.bfloat16)
                else:
                    st["q"] = qi_t

            def scores():
                if i > MOBA_TOPK:
                    s = jnp.dot(kaug_ref[c, 0:n_keys, :], st.pop("q"),
                                preferred_element_type=jnp.float32)
                else:
                    s = jnp.dot(k_ref[0:n_keys, lanes], st.pop("q"), preferred_element_type=jnp.float32)
                s_own = jnp.where(causal, s[i * blk:n_keys, :], NEG_BIG)
                sm = s_own if i == 0 else jnp.concatenate([s[0:i * blk, :], s_own], axis=0)
                st["s"] = sm
                st["m"] = jnp.max(sm, axis=0, keepdims=True)

            def probs():
                st["p"] = jnp.exp2(st.pop("s") - st.pop("m")).astype(jnp.bfloat16)

            def values():
                v_i = jnp.concatenate([value_block(j) for j in range(i + 1)], axis=1)
                o_t = jnp.dot(v_i, st.pop("p"), preferred_element_type=jnp.float32)
                o_t = o_t[0:LANES, :] * (1.0 / o_t[LANES:LANES + 1, :])
                o_own = jnp.concatenate([o_t[h * dh:(h + 1) * dh, h * blk:(h + 1) * blk]
                                         for h in range(n_head)], axis=0)
                o_ref[i * blk:(i + 1) * blk, lanes] = o_own.T.astype(o_ref.dtype)

            return [prep, scores, probs, values]

        return [section(j) + section(n_blk - 1 - j) for j in range(n_blk // 2)]

    _interleave([stream for c in range(MOBA_COLS) for stream in column(c)], lag=1)


def _moba(mq, mk, mv, kmean, batch, seq):
    n_blk = seq // MOBA_BLOCK
    width = MOBA_COLS * LANES
    tok = pl.BlockSpec((seq, width), lambda b, c: (b, c))
    return pl.pallas_call(
        functools.partial(_moba_kernel, n_blk=n_blk),
        grid=(batch, MOBA_WIDTH // width),
        in_specs=[tok, tok, tok, pl.BlockSpec((1, n_blk, width), lambda b, c: (b, 0, c))],
        out_specs=tok,
        out_shape=jax.ShapeDtypeStruct((batch * seq, MOBA_WIDTH), jnp.bfloat16),
        scratch_shapes=[pltpu.VMEM((MOBA_COLS, seq, 2 * LANES), jnp.bfloat16)],
        compiler_params=pltpu.CompilerParams(
            dimension_semantics=("arbitrary", "arbitrary"), vmem_limit_bytes=VMEM_LIMIT),
        name="moba",
    )(mq, mk, mv, kmean)


def _gla_kernel(q_ref, k_ref, v_ref, la_ref, o_ref, st_ref):
    @pl.when(pl.program_id(1) == 0)
    def _():
        st_ref[...] = jnp.zeros_like(st_ref)

    for r in range(GLA_STEP // GLA_TILE):
        rows = slice(r * GLA_TILE, (r + 1) * GLA_TILE)
        _gla_tile(q_ref[rows, :], k_ref[rows, :], v_ref.at[rows, :], la_ref[rows, :],
                  o_ref.at[rows, :], st_ref)


def _gla_tile(q, k, v_ref, la, o_ref, st_ref):
    c = GLA_TILE
    dk, dv = GLA_KEY_DIM, GLA_VALUE_DIM
    row = lax.broadcasted_iota(jnp.int32, la.shape, 0)
    lane = lax.broadcasted_iota(jnp.int32, la.shape, 1)
    odd_head = (lane & dk) != 0
    st_lane = lax.broadcasted_iota(jnp.int32, (dv, GLA_K_WIDTH), 1)

    b = la
    step = 1
    while step < c:
        b = b + jnp.where(row >= step, pltpu.roll(b, step, 0), 0.0)
        step *= 2

    qf = q.astype(jnp.float32) * (dk ** -0.5)
    kf = k.astype(jnp.float32)

    sub = c // 2
    t_i = lax.broadcasted_iota(jnp.int32, (sub, sub), 0)
    s_i = lax.broadcasted_iota(jnp.int32, (sub, sub), 1)
    level = jnp.where(t_i >= s_i, 31 - lax.clz(t_i ^ s_i), -2)

    def split_heads(x):
        return (jnp.where(odd_head, 0.0, x).astype(jnp.bfloat16),
                jnp.where(odd_head, x, 0.0).astype(jnp.bfloat16))

    def level_products(q_lvl, k_lvl, row_blk, col_blk):
        q_even, q_odd = split_heads(q_lvl)
        k_b = k_lvl.astype(jnp.bfloat16)
        rows = slice(row_blk * sub, (row_blk + 1) * sub)
        keys = slice(col_blk * sub, (col_blk + 1) * sub)
        out = []
        for h in range(GLA_HEADS):
            cols = slice((h // 2) * LANES, (h // 2 + 1) * LANES)
            q_h = (q_odd if h % 2 else q_even)[rows, cols]
            out.append(lax.dot_general(q_h, k_b[keys, cols], _NT, preferred_element_type=jnp.float32))
        return out

    def add_level(acc, q_lvl, k_lvl, lvl):
        hit = level == lvl
        for r in range(2):
            for h, a in enumerate(level_products(q_lvl, k_lvl, r, r)):
                acc[r][h] = jnp.where(hit, a, acc[r][h])
        return acc

    acc = [[jnp.zeros((sub, sub), jnp.float32) for _ in range(GLA_HEADS)] for _ in range(2)]
    acc = add_level(acc, qf, kf, -1)
    a_cross = None
    b_end = b
    half = 1
    lvl = 0
    while half < c:
        b_before = pltpu.roll(b_end, half, 0)
        q_lvl = qf * jnp.exp2(jnp.minimum(b - b_before, 0.0))
        k_lvl = kf * jnp.exp2(b_end - b)
        if half < sub:
            acc = add_level(acc, q_lvl, k_lvl, lvl)
        else:
            a_cross = level_products(q_lvl, k_lvl, 1, 0)
        upper = (row & half) != 0
        b_end = jnp.where(upper, b_end, pltpu.roll(b_end, c - half, 0))
        half *= 2
        lvl += 1
    q_in = (qf * jnp.exp2(b)).astype(jnp.bfloat16)
    k_out = (kf * jnp.exp2(b_end - b)).astype(jnp.bfloat16)
    tile_decay = jnp.exp2(b_end[0:1, :])

    for h in range(GLA_HEADS):
        cols = slice((h // 2) * LANES, (h // 2 + 1) * LANES)
        v_h = v_ref[:, h * dv:(h + 1) * dv]
        st = st_ref[h]
        o_inter = lax.dot_general(q_in[:, cols], st[:, cols].astype(jnp.bfloat16), _NT,
                                  preferred_element_type=jnp.float32)
        a_lo = acc[0][h].astype(jnp.bfloat16)
        a_hi = jnp.concatenate([a_cross[h], acc[1][h]], axis=1).astype(jnp.bfloat16)
        o_intra = jnp.concatenate(
            [jnp.dot(a_lo, v_h[0:sub, :], preferred_element_type=jnp.float32),
             jnp.dot(a_hi, v_h, preferred_element_type=jnp.float32)], axis=0)
        o_ref[:, h * dv:(h + 1) * dv] = (o_inter + o_intra).astype(o_ref.dtype)
        v_t = v_h.astype(jnp.float32).T.astype(jnp.bfloat16)
        upd = jnp.dot(v_t, k_out, preferred_element_type=jnp.float32)
        own = (st_lane >= h * dk) & (st_lane < (h + 1) * dk)
        st_ref[h] = st * tile_decay + jnp.where(own, upd, 0.0)


def _gla(gq, gk, gv, la, batch, seq):
    n_tile = seq // GLA_STEP

    def tok(width):
        return pl.BlockSpec((GLA_STEP, width), lambda b, t: (b * n_tile + t, 0))

    return pl.pallas_call(
        _gla_kernel,
        grid=(batch, n_tile),
        in_specs=[tok(GLA_K_WIDTH), tok(GLA_K_WIDTH), tok(GLA_V_WIDTH), tok(GLA_K_WIDTH)],
        out_specs=tok(GLA_V_WIDTH),
        out_shape=jax.ShapeDtypeStruct((batch * seq, GLA_V_WIDTH), jnp.bfloat16),
        scratch_shapes=[pltpu.VMEM((GLA_HEADS, GLA_VALUE_DIM, GLA_K_WIDTH), jnp.float32)],
        compiler_params=pltpu.CompilerParams(
            dimension_semantics=("arbitrary", "arbitrary"), vmem_limit_bytes=VMEM_LIMIT),
        name="gla",
    )(gq, gk, gv, la)


def kernel(x, ffn1_pre_g, ffn1_w_gate, ffn1_w_up, ffn1_w_down, ffn1_post_g, mix_pre_g, w_in,
           gla_w_alpha_up, gla_b_alpha, moba_out_g, gla_out_g, w_out, mix_post_g, ffn2_pre_g,
           ffn2_w_gate, ffn2_w_up, ffn2_w_down, ffn2_post_g):
    batch, seq, d = x.shape
    assert d == D_MODEL and seq % MOBA_BLOCK == 0 and seq % GLA_STEP == 0
    assert (batch * seq) % FFN_TM == 0 and (batch * seq) % MIX_TM == 0
    bf = jnp.bfloat16
    xt = x.reshape(batch * seq, d)
    for l in range(w_in.shape[0]):
        n_qkv = 3 * MOBA_WIDTH
        n_gla = 2 * GLA_K_WIDTH + GLA_V_WIDTH
        w = w_in[l]
        wqkv = w[:, :n_qkv].astype(bf)
        wgla = jnp.concatenate(
            [w[:, n_qkv:n_qkv + n_gla], w[:, n_qkv + n_gla + GLA_GATE_RANK:]], axis=1).astype(bf)
        wa = jnp.pad(w[:, n_qkv + n_gla:n_qkv + n_gla + GLA_GATE_RANK],
                     ((0, 0), (0, GATE_PAD - GLA_GATE_RANK))).astype(bf)
        wup = jnp.pad(gla_w_alpha_up[l], ((0, GATE_PAD - GLA_GATE_RANK), (0, 0))).astype(bf)
        xt, mq, mk, mv, kmean, gq, gk, gv, gr, la = _ffn_proj(
            xt, ffn1_pre_g[l][None], ffn1_w_gate[l].astype(bf), ffn1_w_up[l].astype(bf),
            ffn1_w_down[l].astype(bf), ffn1_post_g[l][None],
            mix_pre_g[l][None], wqkv, wgla, wa, wup, gla_b_alpha[l][None])

        kmean = kmean.reshape(batch, seq // MOBA_BLOCK, MOBA_WIDTH)
        o_moba = _moba(mq, mk, mv, kmean, batch, seq)
        o_gla = _gla(gq, gk, gv, la, batch, seq)

        xt = _mix_ffn(xt, o_moba, o_gla, gr, moba_out_g[l][None],
                      gla_out_g[l].reshape(1, GLA_V_WIDTH), w_out[l].astype(bf), mix_post_g[l][None],
                      ffn2_pre_g[l][None], ffn2_w_gate[l].astype(bf), ffn2_w_up[l].astype(bf),
                      ffn2_w_down[l].astype(bf), ffn2_post_g[l][None])
    return xt.reshape(batch, seq, d)
```

```python
import functools

import jax
import jax.numpy as jnp
from jax import lax
from jax.experimental import pallas as pl
from jax.experimental.pallas import tpu as pltpu

D_MODEL = 1024
D_FF = 2816
RMS_EPS = 1e-6

MOBA_WIDTH = 512
MOBA_HEAD_DIM = 64
MOBA_BLOCK = 256
MOBA_TOPK = 3
MOBA_COLS = 2
LOG2_E = 1.4426950408889634
MOBA_Q_SCALE = MOBA_HEAD_DIM ** -0.5 * LOG2_E

GLA_HEADS = 4
GLA_KEY_DIM = 64
GLA_VALUE_DIM = 128
GLA_K_WIDTH = GLA_HEADS * GLA_KEY_DIM
GLA_V_WIDTH = GLA_HEADS * GLA_VALUE_DIM
GLA_GATE_RANK = 16
GLA_GATE_TAU = 16.0
GLA_TILE = 256
GLA_STEP = 1024

LANES = 128
GATE_PAD = LANES
NEG_BIG = -1e30
SUM_ROWS = 16

FFN_TM = 512
FFN_TF = 256
CAST_ROWS = 128
PROJ_MAIN = 3 * MOBA_WIDTH + 2 * GLA_K_WIDTH + GLA_V_WIDTH
MIX_TM = 1024
MIX_SUB = 256
VMEM_LIMIT = 56 * 1024 * 1024

_NT = (((1,), (1,)), ((), ()))


def _rms(x, g):
    return x * lax.rsqrt(jnp.mean(x * x, axis=-1, keepdims=True) + RMS_EPS) * g


def _silu(x):
    return x / (1.0 + jnp.exp(-x))


def _resident(shape):
    return pl.BlockSpec(shape, lambda *_: (0,) * len(shape), pipeline_mode=pl.Buffered(1))


def _interleave(stage_lists, lag):
    n_slots = max(len(st) + j * lag for j, st in enumerate(stage_lists))
    for k in range(n_slots):
        for j, st in enumerate(stage_lists):
            if 0 <= k - j * lag < len(st):
                st[k - j * lag]()


def _cast_weight(src_hbm, dst_ref, stage_ref, sem, col0=0):
    rows, cols = dst_ref.shape
    n_chunk = rows // CAST_ROWS

    def copy(c):
        slot = c % 2
        return pltpu.make_async_copy(
            src_hbm.at[pl.ds(c * CAST_ROWS, CAST_ROWS), pl.ds(col0, cols)],
            stage_ref.at[slot, :, pl.ds(0, cols)], sem.at[slot])

    copy(0).start()
    for c in range(n_chunk):
        if c + 1 < n_chunk:
            copy(c + 1).start()
        copy(c).wait()
        dst_ref[c * CAST_ROWS:(c + 1) * CAST_ROWS, :] = stage_ref[c % 2, :, 0:cols].astype(jnp.bfloat16)


_HBM = pl.BlockSpec(memory_space=pl.ANY)
_CAST_SCRATCH = [pltpu.VMEM((2, CAST_ROWS, D_FF), jnp.float32), pltpu.SemaphoreType.DMA((2,))]


def _row(width, rows=FFN_TM):
    return pl.BlockSpec((rows, width), lambda i: (i, 0))


_ROW_PARAMS = dict(
    compiler_params=pltpu.CompilerParams(
        dimension_semantics=("arbitrary",), vmem_limit_bytes=VMEM_LIMIT))


def _ffn_proj_stages(r, x_ref, pre_ref, post_ref, g_ref, wgr_ref, wa_ref, wup_ref, ba_ref,
                     x1_ref, mq_ref, mk_ref, mv_ref, kmean_ref, gq_ref, gk_ref, gv_ref, gr_ref, la_ref,
                     h_ref, wg_ref, wu_ref, wd_ref, wmain_ref):
    rows = slice(r * MOBA_BLOCK, (r + 1) * MOBA_BLOCK)
    st = {}

    def prenorm():
        st["xn"] = _rms(x_ref[rows, :], pre_ref[...]).astype(jnp.bfloat16)

    def up(c):
        def run():
            cols = slice(c * FFN_TF, (c + 1) * FFN_TF)
            g = jnp.dot(st["xn"], wg_ref[:, cols], preferred_element_type=jnp.float32)
            u = jnp.dot(st["xn"], wu_ref[:, cols], preferred_element_type=jnp.float32)
            h_ref[rows, cols] = (_silu(g) * u).astype(jnp.bfloat16)
        return run

    def down():
        st["f"] = jnp.dot(h_ref[rows, :], wd_ref[...], preferred_element_type=jnp.float32)

    def mid():
        x1 = x_ref[rows, :] + 0.5 * _rms(st.pop("f"), post_ref[...])
        x1_ref[rows, :] = x1
        st["h"] = _rms(x1, g_ref[...]).astype(jnp.bfloat16)

    def proj(w_ref, lo, hi):
        return jnp.dot(st["h"], w_ref[:, lo:hi], preferred_element_type=jnp.float32)

    w, kw, vw = MOBA_WIDTH, GLA_K_WIDTH, GLA_V_WIDTH

    def moba_q():
        mq_ref[rows, :] = (proj(wmain_ref, 0, w) * MOBA_Q_SCALE).astype(jnp.bfloat16)

    def moba_k():
        mk = proj(wmain_ref, w, 2 * w)
        mk_ref[rows, :] = mk.astype(jnp.bfloat16)
        kmean_ref[r] = jnp.mean(mk, axis=0, keepdims=True)

    def moba_v():
        mv_ref[rows, :] = proj(wmain_ref, 2 * w, 3 * w).astype(jnp.bfloat16)

    def gla_qk():
        gq_ref[rows, :] = proj(wmain_ref, 3 * w, 3 * w + kw).astype(jnp.bfloat16)
        gk_ref[rows, :] = proj(wmain_ref, 3 * w + kw, 3 * w + 2 * kw).astype(jnp.bfloat16)

    def gla_v():
        gv_ref[rows, :] = proj(wmain_ref, 3 * w + 2 * kw, PROJ_MAIN).astype(jnp.bfloat16)

    def gla_gate():
        gr_ref[rows, :] = proj(wgr_ref, 0, vw).astype(jnp.bfloat16)

    def gla_decay():
        ga = jnp.dot(st["h"], wa_ref[...], preferred_element_type=jnp.float32)
        z = jnp.dot(ga.astype(jnp.bfloat16), wup_ref[...],
                    preferred_element_type=jnp.float32) + ba_ref[...]
        log_sig = jnp.minimum(z, 0.0) - jnp.log(1.0 + jnp.exp(-jnp.abs(z)))
        la_ref[rows, :] = log_sig * (LOG2_E / GLA_GATE_TAU)

    return ([prenorm] + [up(c) for c in range(D_FF // FFN_TF)] + [down, mid]
            + [moba_q, moba_k, moba_v, gla_qk, gla_v, gla_gate, gla_decay])


def _ffn_proj_kernel(x_ref, pre_ref, wg_hbm, wu_hbm, wd_hbm, post_ref, g_ref, win_hbm, *rest):
    wg_ref, wu_ref, wd_ref, wmain_ref, stage_ref, sem = rest[-6:]

    @pl.when(pl.program_id(0) == 0)
    def _():
        _cast_weight(wg_hbm, wg_ref, stage_ref, sem)
        _cast_weight(wu_hbm, wu_ref, stage_ref, sem)
        _cast_weight(wd_hbm, wd_ref, stage_ref, sem)
        _cast_weight(win_hbm, wmain_ref, stage_ref, sem)

    refs = (x_ref, pre_ref, post_ref, g_ref) + rest[:-2]
    _interleave([_ffn_proj_stages(r, *refs) for r in range(FFN_TM // MOBA_BLOCK)], lag=4)


def _ffn_proj(x, pre_g, wg, wu, wd, post_g, g, w_in, wgr, wa, wup, ba):
    t = x.shape[0]
    nblk = FFN_TM // MOBA_BLOCK
    small = (wgr, wa, wup, ba)

    def out(width, dtype=jnp.bfloat16):
        return jax.ShapeDtypeStruct((t, width), dtype)

    return pl.pallas_call(
        _ffn_proj_kernel,
        grid=(t // FFN_TM,),
        in_specs=[_row(D_MODEL), _resident(pre_g.shape), _HBM, _HBM, _HBM, _resident(post_g.shape),
                  _resident(g.shape), _HBM] + [_resident(p.shape) for p in small],
        out_specs=[_row(D_MODEL), _row(MOBA_WIDTH), _row(MOBA_WIDTH), _row(MOBA_WIDTH),
                   pl.BlockSpec((nblk, 1, MOBA_WIDTH), lambda i: (i, 0, 0)),
                   _row(GLA_K_WIDTH), _row(GLA_K_WIDTH), _row(GLA_V_WIDTH), _row(GLA_V_WIDTH),
                   _row(GLA_K_WIDTH)],
        out_shape=[out(D_MODEL, jnp.float32), out(MOBA_WIDTH), out(MOBA_WIDTH), out(MOBA_WIDTH),
                   jax.ShapeDtypeStruct((t // MOBA_BLOCK, 1, MOBA_WIDTH), jnp.float32),
                   out(GLA_K_WIDTH), out(GLA_K_WIDTH), out(GLA_V_WIDTH), out(GLA_V_WIDTH),
                   out(GLA_K_WIDTH, jnp.float32)],
        scratch_shapes=[pltpu.VMEM((FFN_TM, D_FF), jnp.bfloat16),
                        pltpu.VMEM((D_MODEL, D_FF), jnp.bfloat16), pltpu.VMEM((D_MODEL, D_FF), jnp.bfloat16),
                        pltpu.VMEM((D_FF, D_MODEL), jnp.bfloat16),
                        pltpu.VMEM((D_MODEL, PROJ_MAIN), jnp.bfloat16)] + _CAST_SCRATCH,
        name="ffn_proj", **_ROW_PARAMS,
    )(x, pre_g, wg, wu, wd, post_g, g, w_in, *small)


def _mix_ffn_stages(rows, x_ref, om_ref, og_ref, gr_ref, mg_ref, gg_ref, mpost_ref, pre_ref, post_ref,
                    o_ref, h_ref, wo_ref, wg_ref, wu_ref, wd_ref):
    st = {}

    def mix_in():
        om = _rms(om_ref[rows, :].astype(jnp.float32), mg_ref[...]).astype(jnp.bfloat16)
        gate = _silu(gr_ref[rows, :].astype(jnp.float32))
        og = og_ref[rows, :].astype(jnp.float32)
        parts = [om]
        for h in range(GLA_HEADS):
            cols = slice(h * GLA_VALUE_DIM, (h + 1) * GLA_VALUE_DIM)
            parts.append((_rms(og[:, cols], gg_ref[:, cols]) * gate[:, cols]).astype(jnp.bfloat16))
        st["mix_in"] = jnp.concatenate(parts, axis=-1)

    def mix_out():
        st["mix"] = jnp.dot(st.pop("mix_in"), wo_ref[...], preferred_element_type=jnp.float32)

    def norms():
        x2 = x_ref[rows, :] + _rms(st.pop("mix"), mpost_ref[...])
        st["x2"] = x2
        st["xn"] = _rms(x2, pre_ref[...]).astype(jnp.bfloat16)

    def up(c):
        def run():
            cols = slice(c * FFN_TF, (c + 1) * FFN_TF)
            g = jnp.dot(st["xn"], wg_ref[:, cols], preferred_element_type=jnp.float32)
            u = jnp.dot(st["xn"], wu_ref[:, cols], preferred_element_type=jnp.float32)
            h_ref[rows, cols] = (_silu(g) * u).astype(jnp.bfloat16)
        return run

    def down():
        st["f"] = jnp.dot(h_ref[rows, :], wd_ref[...], preferred_element_type=jnp.float32)

    def finish():
        o_ref[rows, :] = st.pop("x2") + 0.5 * _rms(st.pop("f"), post_ref[...])

    return [mix_in, mix_out, norms] + [up(c) for c in range(D_FF // FFN_TF)] + [down, finish]


def _mix_ffn_kernel(x_ref, om_ref, og_ref, gr_ref, mg_ref, gg_ref, wo_hbm, mpost_ref,
                    pre_ref, wg_hbm, wu_hbm, wd_hbm, post_ref, o_ref,
                    h_ref, wo_ref, wg_ref, wu_ref, wd_ref, stage_ref, sem):
    @pl.when(pl.program_id(0) == 0)
    def _():
        _cast_weight(wo_hbm, wo_ref, stage_ref, sem)
        _cast_weight(wg_hbm, wg_ref, stage_ref, sem)
        _cast_weight(wu_hbm, wu_ref, stage_ref, sem)
        _cast_weight(wd_hbm, wd_ref, stage_ref, sem)

    refs = (x_ref, om_ref, og_ref, gr_ref, mg_ref, gg_ref, mpost_ref, pre_ref, post_ref,
            o_ref, h_ref, wo_ref, wg_ref, wu_ref, wd_ref)
    _interleave([_mix_ffn_stages(slice(r * MIX_SUB, (r + 1) * MIX_SUB), *refs)
                 for r in range(MIX_TM // MIX_SUB)], lag=4)


def _mix_ffn(x, om, og, gr, mg, gg, wo, mpost_g, pre_g, wg, wu, wd, post_g):
    t = x.shape[0]
    return pl.pallas_call(
        _mix_ffn_kernel,
        grid=(t // MIX_TM,),
        in_specs=[_row(D_MODEL, MIX_TM), _row(MOBA_WIDTH, MIX_TM), _row(GLA_V_WIDTH, MIX_TM),
                  _row(GLA_V_WIDTH, MIX_TM), _resident(mg.shape), _resident(gg.shape), _HBM,
                  _resident(mpost_g.shape), _resident(pre_g.shape), _HBM, _HBM, _HBM,
                  _resident(post_g.shape)],
        out_specs=_row(D_MODEL, MIX_TM),
        out_shape=jax.ShapeDtypeStruct((t, D_MODEL), jnp.float32),
        scratch_shapes=[pltpu.VMEM((MIX_TM, D_FF), jnp.bfloat16),
                        pltpu.VMEM((D_MODEL, D_MODEL), jnp.bfloat16),
                        pltpu.VMEM((D_MODEL, D_FF), jnp.bfloat16), pltpu.VMEM((D_MODEL, D_FF), jnp.bfloat16),
                        pltpu.VMEM((D_FF, D_MODEL), jnp.bfloat16)] + _CAST_SCRATCH,
        name="mix_ffn", **_ROW_PARAMS,
    )(x, om, og, gr, mg, gg, wo, mpost_g, pre_g, wg, wu, wd, post_g)


def _moba_kernel(q_ref, k_ref, v_ref, kmean_ref, o_ref, kaug_ref, *, n_blk):
    blk = MOBA_BLOCK
    dh = MOBA_HEAD_DIM
    seq = n_blk * blk
    n_head = LANES // dh
    wide = n_head * blk

    key_blk = lax.broadcasted_iota(jnp.int32, (seq, LANES), 0) // blk
    hot_col = lax.broadcasted_iota(jnp.int32, (seq, LANES), 1)
    one_hot = jnp.where(key_blk == hot_col, 1.0, 0.0).astype(jnp.bfloat16)
    feat = lax.broadcasted_iota(jnp.int32, (LANES, blk), 0)
    head_rows = [(feat >= h * dh) & (feat < (h + 1) * dh) for h in range(n_head)]
    blk_row = lax.broadcasted_iota(jnp.int32, (n_blk, wide), 0)
    key_i = lax.broadcasted_iota(jnp.int32, (blk, wide), 0)
    qry_i = lax.broadcasted_iota(jnp.int32, (blk, wide), 1) % blk
    causal = key_i <= qry_i
    bias_pad = jnp.zeros((LANES - n_blk, wide), jnp.float32)
    ones = jnp.ones((SUM_ROWS, blk), jnp.float32)

    def column(c):
        lanes = slice(c * LANES, (c + 1) * LANES)
        kmean = kmean_ref[0][:, lanes]
        km_hi = kmean.astype(jnp.bfloat16)
        km_lo = (kmean - km_hi.astype(jnp.float32)).astype(jnp.bfloat16)
        kaug_ref[c, :, 0:LANES] = k_ref[:, lanes]
        kaug_ref[c, :, LANES:2 * LANES] = one_hot
        v_t = [None] * n_blk

        def value_block(j):
            if v_t[j] is None:
                v_j = v_ref[j * blk:(j + 1) * blk, lanes].astype(jnp.float32).T
                v_t[j] = jnp.concatenate([v_j, ones], axis=0).astype(jnp.bfloat16)
            return v_t[j]

        def section(i):
            st = {}
            n_keys = (i + 1) * blk

            def prep():
                q_i = q_ref[i * blk:(i + 1) * blk, lanes].astype(jnp.float32).T
                qi_f = jnp.concatenate([jnp.where(rows, q_i, 0.0) for rows in head_rows], axis=1)
                qi_t = qi_f.astype(jnp.bfloat16)
                if i > MOBA_TOPK:
                    gate = (jnp.dot(km_hi, qi_t, preferred_element_type=jnp.float32)
                            + jnp.dot(km_lo, qi_t, preferred_element_type=jnp.float32))
                    rank = jnp.zeros((n_blk, wide), jnp.float32)
                    for j in range(i):
                        gj = gate[j:j + 1, :]
                        beats = (gj > gate) | ((gj == gate) & (blk_row > j))
                        rank = rank + jnp.where(beats, 1.0, 0.0)
                    keep = (rank < float(MOBA_TOPK)) | (blk_row >= i)
                    bias = jnp.where(keep, 0.0, NEG_BIG)
                    st["q"] = jnp.concatenate([qi_f, bias, bias_pad], axis=0).astype(jnp.bfloat16)
                else:
                    st["q"] = qi_t

            def scores():
                if i > MOBA_TOPK:
                    s = jnp.dot(kaug_ref[c, 0:n_keys, :], st.pop("q"),
                                preferred_element_type=jnp.float32)
                else:
                    s = jnp.dot(k_ref[0:n_keys, lanes], st.pop("q"), preferred_element_type=jnp.float32)
                s_own = jnp.where(causal, s[i * blk:n_keys, :], NEG_BIG)
                sm = s_own if i == 0 else jnp.concatenate([s[0:i * blk, :], s_own], axis=0)
                st["s"] = sm
                st["m"] = jnp.max(sm, axis=0, keepdims=True)

            def probs():
                st["p"] = jnp.exp2(st.pop("s") - st.pop("m")).astype(jnp.bfloat16)

            def values():
                v_i = jnp.concatenate([value_block(j) for j in range(i + 1)], axis=1)
                o_t = jnp.dot(v_i, st.pop("p"), preferred_element_type=jnp.float32)
                o_t = o_t[0:LANES, :] * (1.0 / o_t[LANES:LANES + 1, :])
                o_own = jnp.concatenate([o_t[h * dh:(h + 1) * dh, h * blk:(h + 1) * blk]
                                         for h in range(n_head)], axis=0)
                o_ref[i * blk:(i + 1) * blk, lanes] = o_own.T.astype(o_ref.dtype)

            return [prep, scores, probs, values]

        return [section(j) + section(n_blk - 1 - j) for j in range(n_blk // 2)]

    _interleave([stream for c in range(MOBA_COLS) for stream in column(c)], lag=1)


def _moba(mq, mk, mv, kmean, batch, seq):
    n_blk = seq // MOBA_BLOCK
    width = MOBA_COLS * LANES
    tok = pl.BlockSpec((seq, width), lambda b, c: (b, c))
    return pl.pallas_call(
        functools.partial(_moba_kernel, n_blk=n_blk),
        grid=(batch, MOBA_WIDTH // width),
        in_specs=[tok, tok, tok, pl.BlockSpec((1, n_blk, width), lambda b, c: (b, 0, c))],
        out_specs=tok,
        out_shape=jax.ShapeDtypeStruct((batch * seq, MOBA_WIDTH), jnp.bfloat16),
        scratch_shapes=[pltpu.VMEM((MOBA_COLS, seq, 2 * LANES), jnp.bfloat16)],
        compiler_params=pltpu.CompilerParams(
            dimension_semantics=("arbitrary", "arbitrary"), vmem_limit_bytes=VMEM_LIMIT),
        name="moba",
    )(mq, mk, mv, kmean)


def _gla_kernel(q_ref, k_ref, v_ref, la_ref, o_ref, st_ref):
    @pl.when(pl.program_id(1) == 0)
    def _():
        st_ref[...] = jnp.zeros_like(st_ref)

    for r in range(GLA_STEP // GLA_TILE):
        rows = slice(r * GLA_TILE, (r + 1) * GLA_TILE)
        _gla_tile(q_ref[rows, :], k_ref[rows, :], v_ref.at[rows, :], la_ref[rows, :],
                  o_ref.at[rows, :], st_ref)


def _gla_tile(q, k, v_ref, la, o_ref, st_ref):
    c = GLA_TILE
    dk, dv = GLA_KEY_DIM, GLA_VALUE_DIM
    row = lax.broadcasted_iota(jnp.int32, la.shape, 0)
    lane = lax.broadcasted_iota(jnp.int32, la.shape, 1)
    odd_head = (lane & dk) != 0
    st_lane = lax.broadcasted_iota(jnp.int32, (dv, GLA_K_WIDTH), 1)

    b = la
    step = 1
    while step < c:
        b = b + jnp.where(row >= step, pltpu.roll(b, step, 0), 0.0)
        step *= 2

    qf = q.astype(jnp.float32) * (dk ** -0.5)
    kf = k.astype(jnp.float32)

    sub = c // 2
    t_i = lax.broadcasted_iota(jnp.int32, (sub, sub), 0)
    s_i = lax.broadcasted_iota(jnp.int32, (sub, sub), 1)
    level = jnp.where(t_i >= s_i, 31 - lax.clz(t_i ^ s_i), -2)

    def split_heads(x):
        return (jnp.where(odd_head, 0.0, x).astype(jnp.bfloat16),
                jnp.where(odd_head, x, 0.0).astype(jnp.bfloat16))

    def level_products(q_lvl, k_lvl, row_blk, col_blk):
        q_b = q_lvl.astype(jnp.bfloat16)
        k_even, k_odd = split_heads(k_lvl)
        rows = slice(row_blk * sub, (row_blk + 1) * sub)
        keys = slice(col_blk * sub, (col_blk + 1) * sub)
        out = []
        for pair in range(GLA_HEADS // 2):
            cols = slice(pair * LANES, (pair + 1) * LANES)
            k_pair = jnp.concatenate([k_even[keys, cols], k_odd[keys, cols]], axis=0)
            a = lax.dot_general(q_b[rows, cols], k_pair, _NT, preferred_element_type=jnp.float32)
            out += [a[:, 0:sub], a[:, sub:2 * sub]]
        return out

    def add_level(acc, q_lvl, k_lvl, lvl):
        hit = level == lvl
        for r in range(2):
            for h, a in enumerate(level_products(q_lvl, k_lvl, r, r)):
                acc[r][h] = jnp.where(hit, a, acc[r][h])
        return acc

    acc = [[jnp.zeros((sub, sub), jnp.float32) for _ in range(GLA_HEADS)] for _ in range(2)]
    acc = add_level(acc, qf, kf, -1)
    a_cross = None
    b_end = b
    half = 1
    lvl = 0
    while half < c:
        b_before = pltpu.roll(b_end, half, 0)
        q_lvl = qf * jnp.exp2(jnp.minimum(b - b_before, 0.0))
        k_lvl = kf * jnp.exp2(b_end - b)
        if half < sub:
            acc = add_level(acc, q_lvl, k_lvl, lvl)
        else:
            a_cross = level_products(q_lvl, k_lvl, 1, 0)
        upper = (row & half) != 0
        b_end = jnp.where(upper, b_end, pltpu.roll(b_end, c - half, 0))
        half *= 2
        lvl += 1
    q_in = (qf * jnp.exp2(b)).astype(jnp.bfloat16)
    k_out = (kf * jnp.exp2(b_end - b)).astype(jnp.bfloat16)
    tile_decay = jnp.exp2(b_end[0:1, :])

    for h in range(GLA_HEADS):
        cols = slice((h // 2) * LANES, (h // 2 + 1) * LANES)
        v_h = v_ref[:, h * dv:(h + 1) * dv]
        st = st_ref[h]
        o_inter = lax.dot_general(q_in[:, cols], st[:, cols].astype(jnp.bfloat16), _NT,
                                  preferred_element_type=jnp.float32)
        a_lo = acc[0][h].astype(jnp.bfloat16)
        a_hi = jnp.concatenate([a_cross[h], acc[1][h]], axis=1).astype(jnp.bfloat16)
        o_intra = jnp.concatenate(
            [jnp.dot(a_lo, v_h[0:sub, :], preferred_element_type=jnp.float32),
             jnp.dot(a_hi, v_h, preferred_element_type=jnp.float32)], axis=0)
        o_ref[:, h * dv:(h + 1) * dv] = (o_inter + o_intra).astype(o_ref.dtype)
        v_t = v_h.astype(jnp.float32).T.astype(jnp.bfloat16)
        upd = jnp.dot(v_t, k_out, preferred_element_type=jnp.float32)
        own = (st_lane >= h * dk) & (st_lane < (h + 1) * dk)
        st_ref[h] = st * tile_decay + jnp.where(own, upd, 0.0)


def _gla(gq, gk, gv, la, batch, seq):
    n_tile = seq // GLA_STEP

    def tok(width):
        return pl.BlockSpec((GLA_STEP, width), lambda b, t: (b * n_tile + t, 0))

    return pl.pallas_call(
        _gla_kernel,
        grid=(batch, n_tile),
        in_specs=[tok(GLA_K_WIDTH), tok(GLA_K_WIDTH), tok(GLA_V_WIDTH), tok(GLA_K_WIDTH)],
        out_specs=tok(GLA_V_WIDTH),
        out_shape=jax.ShapeDtypeStruct((batch * seq, GLA_V_WIDTH), jnp.bfloat16),
        scratch_shapes=[pltpu.VMEM((GLA_HEADS, GLA_VALUE_DIM, GLA_K_WIDTH), jnp.float32)],
        compiler_params=pltpu.CompilerParams(
            dimension_semantics=("arbitrary", "arbitrary"), vmem_limit_bytes=VMEM_LIMIT),
        name="gla",
    )(gq, gk, gv, la)


def kernel(x, ffn1_pre_g, ffn1_w_gate, ffn1_w_up, ffn1_w_down, ffn1_post_g, mix_pre_g, w_in,
           gla_w_alpha_up, gla_b_alpha, moba_out_g, gla_out_g, w_out, mix_post_g, ffn2_pre_g,
           ffn2_w_gate, ffn2_w_up, ffn2_w_down, ffn2_post_g):
    batch, seq, d = x.shape
    assert d == D_MODEL and seq % MOBA_BLOCK == 0 and seq % GLA_STEP == 0
    assert (batch * seq) % FFN_TM == 0 and (batch * seq) % MIX_TM == 0
    bf = jnp.bfloat16
    xt = x.reshape(batch * seq, d)
    for l in range(w_in.shape[0]):
        w = w_in[l]
        wgr = w[:, PROJ_MAIN + GLA_GATE_RANK:].astype(bf)
        wa = jnp.pad(w[:, PROJ_MAIN:PROJ_MAIN + GLA_GATE_RANK],
                     ((0, 0), (0, GATE_PAD - GLA_GATE_RANK))).astype(bf)
        wup = jnp.pad(gla_w_alpha_up[l], ((0, GATE_PAD - GLA_GATE_RANK), (0, 0))).astype(bf)
        xt, mq, mk, mv, kmean, gq, gk, gv, gr, la = _ffn_proj(
            xt, ffn1_pre_g[l][None], ffn1_w_gate[l], ffn1_w_up[l], ffn1_w_down[l], ffn1_post_g[l][None],
            mix_pre_g[l][None], w, wgr, wa, wup, gla_b_alpha[l][None])

        kmean = kmean.reshape(batch, seq // MOBA_BLOCK, MOBA_WIDTH)
        o_moba = _moba(mq, mk, mv, kmean, batch, seq)
        o_gla = _gla(gq, gk, gv, la, batch, seq)

        xt = _mix_ffn(xt, o_moba, o_gla, gr, moba_out_g[l][None],
                      gla_out_g[l].reshape(1, GLA_V_WIDTH), w_out[l], mix_post_g[l][None],
                      ffn2_pre_g[l][None], ffn2_w_gate[l], ffn2_w_up[l], ffn2_w_down[l],
                      ffn2_post_g[l][None])
    return xt.reshape(batch, seq, d)
```

```python
import functools

import jax
import jax.numpy as jnp
from jax import lax
from jax.experimental import pallas as pl
from jax.experimental.pallas import tpu as pltpu

D_MODEL = 1024
D_FF = 2816
RMS_EPS = 1e-6

MOBA_WIDTH = 512
MOBA_HEAD_DIM = 64
MOBA_BLOCK = 256
MOBA_TOPK = 3
MOBA_COLS = 2
LOG2_E = 1.4426950408889634
MOBA_Q_SCALE = MOBA_HEAD_DIM ** -0.5 * LOG2_E

GLA_HEADS = 4
GLA_KEY_DIM = 64
GLA_VALUE_DIM = 128
GLA_K_WIDTH = GLA_HEADS * GLA_KEY_DIM
GLA_V_WIDTH = GLA_HEADS * GLA_VALUE_DIM
GLA_GATE_RANK = 16
GLA_GATE_TAU = 16.0
GLA_TILE = 256
GLA_STEP = 1024

LANES = 128
GATE_PAD = LANES
NEG_BIG = -1e30
SUM_ROWS = 16

FFN_TM = 512
FFN_TF = 256
CAST_ROWS = 64
CAST_SLOTS = 4
PROJ_MAIN = 3 * MOBA_WIDTH + 2 * GLA_K_WIDTH + GLA_V_WIDTH
PROJ_TAIL = GLA_GATE_RANK + GLA_V_WIDTH
MIX_TM = 1024
MIX_SUB = 256
VMEM_LIMIT = 56 * 1024 * 1024

_NT = (((1,), (1,)), ((), ()))


def _rms(x, g):
    return x * lax.rsqrt(jnp.mean(x * x, axis=-1, keepdims=True) + RMS_EPS) * g


def _silu(x):
    return x / (1.0 + jnp.exp(-x))


def _resident(shape):
    return pl.BlockSpec(shape, lambda *_: (0,) * len(shape), pipeline_mode=pl.Buffered(1))


def _interleave(stage_lists, lag):
    n_slots = max(len(st) + j * lag for j, st in enumerate(stage_lists))
    for k in range(n_slots):
        for j, st in enumerate(stage_lists):
            if 0 <= k - j * lag < len(st):
                st[k - j * lag]()


def _cast_weight(src_hbm, stage_ref, sem, store, n_rows, col0, n_cols):
    n_chunk = n_rows // CAST_ROWS

    def copy(c):
        slot = c % CAST_SLOTS
        return pltpu.make_async_copy(
            src_hbm.at[pl.ds(c * CAST_ROWS, CAST_ROWS), pl.ds(col0, n_cols)],
            stage_ref.at[slot] if n_cols == stage_ref.shape[2] else stage_ref.at[slot, :, pl.ds(0, n_cols)],
            sem.at[slot])

    for c in range(min(CAST_SLOTS - 1, n_chunk)):
        copy(c).start()
    for c in range(n_chunk):
        copy(c).wait()
        store(slice(c * CAST_ROWS, (c + 1) * CAST_ROWS), stage_ref[c % CAST_SLOTS, :, 0:n_cols])
        if c + CAST_SLOTS - 1 < n_chunk:
            copy(c + CAST_SLOTS - 1).start()


def _cast_into(src_hbm, dst_ref, stage_ref, sem, col0=0):
    def store(rows, chunk):
        dst_ref[rows, :] = chunk.astype(jnp.bfloat16)

    _cast_weight(src_hbm, stage_ref, sem, store, dst_ref.shape[0], col0, dst_ref.shape[1])


_HBM = pl.BlockSpec(memory_space=pl.ANY)
_CAST_SCRATCH = [pltpu.VMEM((CAST_SLOTS, CAST_ROWS, D_FF), jnp.float32),
                 pltpu.SemaphoreType.DMA((CAST_SLOTS,))]


def _row(width, rows=FFN_TM):
    return pl.BlockSpec((rows, width), lambda i: (i, 0))


_ROW_PARAMS = dict(
    compiler_params=pltpu.CompilerParams(
        dimension_semantics=("arbitrary",), vmem_limit_bytes=VMEM_LIMIT))


def _ffn_proj_stages(r, x_ref, pre_ref, post_ref, g_ref, wup_ref, ba_ref,
                     x1_ref, mq_ref, mk_ref, mv_ref, kmean_ref, gq_ref, gk_ref, gv_ref, gr_ref, la_ref,
                     h_ref, wg_ref, wu_ref, wd_ref, wmain_ref, wgr_ref, wa_ref):
    rows = slice(r * MOBA_BLOCK, (r + 1) * MOBA_BLOCK)
    st = {}

    def prenorm():
        st["xn"] = _rms(x_ref[rows, :], pre_ref[...]).astype(jnp.bfloat16)

    def up(c):
        def run():
            cols = slice(c * FFN_TF, (c + 1) * FFN_TF)
            g = jnp.dot(st["xn"], wg_ref[:, cols], preferred_element_type=jnp.float32)
            u = jnp.dot(st["xn"], wu_ref[:, cols], preferred_element_type=jnp.float32)
            h_ref[rows, cols] = (_silu(g) * u).astype(jnp.bfloat16)
        return run

    def down():
        st["f"] = jnp.dot(h_ref[rows, :], wd_ref[...], preferred_element_type=jnp.float32)

    def mid():
        x1 = x_ref[rows, :] + 0.5 * _rms(st.pop("f"), post_ref[...])
        x1_ref[rows, :] = x1
        st["h"] = _rms(x1, g_ref[...]).astype(jnp.bfloat16)

    def proj(w_ref, lo, hi):
        return jnp.dot(st["h"], w_ref[:, lo:hi], preferred_element_type=jnp.float32)

    w, kw, vw = MOBA_WIDTH, GLA_K_WIDTH, GLA_V_WIDTH

    def moba_q():
        mq_ref[rows, :] = (proj(wmain_ref, 0, w) * MOBA_Q_SCALE).astype(jnp.bfloat16)

    def moba_k():
        mk = proj(wmain_ref, w, 2 * w)
        mk_ref[rows, :] = mk.astype(jnp.bfloat16)
        kmean_ref[r] = jnp.mean(mk, axis=0, keepdims=True)

    def moba_v():
        mv_ref[rows, :] = proj(wmain_ref, 2 * w, 3 * w).astype(jnp.bfloat16)

    def gla_qk():
        gq_ref[rows, :] = proj(wmain_ref, 3 * w, 3 * w + kw).astype(jnp.bfloat16)
        gk_ref[rows, :] = proj(wmain_ref, 3 * w + kw, 3 * w + 2 * kw).astype(jnp.bfloat16)

    def gla_v():
        gv_ref[rows, :] = proj(wmain_ref, 3 * w + 2 * kw, PROJ_MAIN).astype(jnp.bfloat16)

    def gla_gate():
        gr_ref[rows, :] = proj(wgr_ref, 0, vw).astype(jnp.bfloat16)

    def gla_decay():
        ga = jnp.dot(st["h"], wa_ref[...], preferred_element_type=jnp.float32)
        z = jnp.dot(ga.astype(jnp.bfloat16), wup_ref[...],
                    preferred_element_type=jnp.float32) + ba_ref[...]
        log_sig = jnp.minimum(z, 0.0) - jnp.log(1.0 + jnp.exp(-jnp.abs(z)))
        la_ref[rows, :] = log_sig * (LOG2_E / GLA_GATE_TAU)

    return ([prenorm] + [up(c) for c in range(D_FF // FFN_TF)] + [down, mid]
            + [moba_q, moba_k, moba_v, gla_qk, gla_v, gla_gate, gla_decay])


def _ffn_proj_kernel(x_ref, pre_ref, wg_hbm, wu_hbm, wd_hbm, post_ref, g_ref, win_hbm, *rest):
    wg_ref, wu_ref, wd_ref, wmain_ref, wgr_ref, wa_ref, tail_ref, stage_ref, sem = rest[-9:]

    @pl.when(pl.program_id(0) == 0)
    def _():
        _cast_into(wg_hbm, wg_ref, stage_ref, sem)
        _cast_into(wu_hbm, wu_ref, stage_ref, sem)
        _cast_into(wd_hbm, wd_ref, stage_ref, sem)
        _cast_into(win_hbm, wmain_ref, stage_ref, sem)

        def store_tail(rows, chunk):
            wa_ref[rows, :] = chunk[:, 0:GATE_PAD].astype(jnp.bfloat16)
            wgr_ref[rows, :] = chunk[:, GLA_GATE_RANK:PROJ_TAIL].astype(jnp.bfloat16)

        _cast_weight(win_hbm, tail_ref, sem, store_tail, D_MODEL, PROJ_MAIN, PROJ_TAIL)

    refs = (x_ref, pre_ref, post_ref, g_ref) + rest[:-3]
    _interleave([_ffn_proj_stages(r, *refs) for r in range(FFN_TM // MOBA_BLOCK)], lag=4)


def _ffn_proj(x, pre_g, wg, wu, wd, post_g, g, w_in, wup, ba):
    t = x.shape[0]
    nblk = FFN_TM // MOBA_BLOCK
    small = (wup, ba)

    def out(width, dtype=jnp.bfloat16):
        return jax.ShapeDtypeStruct((t, width), dtype)

    return pl.pallas_call(
        _ffn_proj_kernel,
        grid=(t // FFN_TM,),
        in_specs=[_row(D_MODEL), _resident(pre_g.shape), _HBM, _HBM, _HBM, _resident(post_g.shape),
                  _resident(g.shape), _HBM] + [_resident(p.shape) for p in small],
        out_specs=[_row(D_MODEL), _row(MOBA_WIDTH), _row(MOBA_WIDTH), _row(MOBA_WIDTH),
                   pl.BlockSpec((nblk, 1, MOBA_WIDTH), lambda i: (i, 0, 0)),
                   _row(GLA_K_WIDTH), _row(GLA_K_WIDTH), _row(GLA_V_WIDTH), _row(GLA_V_WIDTH),
                   _row(GLA_K_WIDTH)],
        out_shape=[out(D_MODEL, jnp.float32), out(MOBA_WIDTH), out(MOBA_WIDTH), out(MOBA_WIDTH),
                   jax.ShapeDtypeStruct((t // MOBA_BLOCK, 1, MOBA_WIDTH), jnp.float32),
                   out(GLA_K_WIDTH), out(GLA_K_WIDTH), out(GLA_V_WIDTH), out(GLA_V_WIDTH),
                   out(GLA_K_WIDTH, jnp.float32)],
        scratch_shapes=[pltpu.VMEM((FFN_TM, D_FF), jnp.bfloat16),
                        pltpu.VMEM((D_MODEL, D_FF), jnp.bfloat16), pltpu.VMEM((D_MODEL, D_FF), jnp.bfloat16),
                        pltpu.VMEM((D_FF, D_MODEL), jnp.bfloat16),
                        pltpu.VMEM((D_MODEL, PROJ_MAIN), jnp.bfloat16),
                        pltpu.VMEM((D_MODEL, GLA_V_WIDTH), jnp.bfloat16),
                        pltpu.VMEM((D_MODEL, GATE_PAD), jnp.bfloat16),
                        pltpu.VMEM((CAST_SLOTS, CAST_ROWS, PROJ_TAIL), jnp.float32)] + _CAST_SCRATCH,
        name="ffn_proj", **_ROW_PARAMS,
    )(x, pre_g, wg, wu, wd, post_g, g, w_in, *small)


def _mix_ffn_stages(rows, x_ref, om_ref, og_ref, gr_ref, mg_ref, gg_ref, mpost_ref, pre_ref, post_ref,
                    o_ref, h_ref, wo_ref, wg_ref, wu_ref, wd_ref):
    st = {}

    def mix_in():
        om = _rms(om_ref[rows, :].astype(jnp.float32), mg_ref[...]).astype(jnp.bfloat16)
        gate = _silu(gr_ref[rows, :].astype(jnp.float32))
        og = og_ref[rows, :].astype(jnp.float32)
        parts = [om]
        for h in range(GLA_HEADS):
            cols = slice(h * GLA_VALUE_DIM, (h + 1) * GLA_VALUE_DIM)
            parts.append((_rms(og[:, cols], gg_ref[:, cols]) * gate[:, cols]).astype(jnp.bfloat16))
        st["mix_in"] = jnp.concatenate(parts, axis=-1)

    def mix_out():
        st["mix"] = jnp.dot(st.pop("mix_in"), wo_ref[...], preferred_element_type=jnp.float32)

    def norms():
        x2 = x_ref[rows, :] + _rms(st.pop("mix"), mpost_ref[...])
        st["x2"] = x2
        st["xn"] = _rms(x2, pre_ref[...]).astype(jnp.bfloat16)

    def up(c):
        def run():
            cols = slice(c * FFN_TF, (c + 1) * FFN_TF)
            g = jnp.dot(st["xn"], wg_ref[:, cols], preferred_element_type=jnp.float32)
            u = jnp.dot(st["xn"], wu_ref[:, cols], preferred_element_type=jnp.float32)
            h_ref[rows, cols] = (_silu(g) * u).astype(jnp.bfloat16)
        return run

    def down():
        st["f"] = jnp.dot(h_ref[rows, :], wd_ref[...], preferred_element_type=jnp.float32)

    def finish():
        o_ref[rows, :] = st.pop("x2") + 0.5 * _rms(st.pop("f"), post_ref[...])

    return [mix_in, mix_out, norms] + [up(c) for c in range(D_FF // FFN_TF)] + [down, finish]


def _mix_ffn_kernel(x_ref, om_ref, og_ref, gr_ref, mg_ref, gg_ref, wo_hbm, mpost_ref,
                    pre_ref, wg_hbm, wu_hbm, wd_hbm, post_ref, o_ref,
                    h_ref, wo_ref, wg_ref, wu_ref, wd_ref, stage_ref, sem):
    @pl.when(pl.program_id(0) == 0)
    def _():
        _cast_into(wo_hbm, wo_ref, stage_ref, sem)
        _cast_into(wg_hbm, wg_ref, stage_ref, sem)
        _cast_into(wu_hbm, wu_ref, stage_ref, sem)
        _cast_into(wd_hbm, wd_ref, stage_ref, sem)

    refs = (x_ref, om_ref, og_ref, gr_ref, mg_ref, gg_ref, mpost_ref, pre_ref, post_ref,
            o_ref, h_ref, wo_ref, wg_ref, wu_ref, wd_ref)
    _interleave([_mix_ffn_stages(slice(r * MIX_SUB, (r + 1) * MIX_SUB), *refs)
                 for r in range(MIX_TM // MIX_SUB)], lag=4)


def _mix_ffn(x, om, og, gr, mg, gg, wo, mpost_g, pre_g, wg, wu, wd, post_g):
    t = x.shape[0]
    return pl.pallas_call(
        _mix_ffn_kernel,
        grid=(t // MIX_TM,),
        in_specs=[_row(D_MODEL, MIX_TM), _row(MOBA_WIDTH, MIX_TM), _row(GLA_V_WIDTH, MIX_TM),
                  _row(GLA_V_WIDTH, MIX_TM), _resident(mg.shape), _resident(gg.shape), _HBM,
                  _resident(mpost_g.shape), _resident(pre_g.shape), _HBM, _HBM, _HBM,
                  _resident(post_g.shape)],
        out_specs=_row(D_MODEL, MIX_TM),
        out_shape=jax.ShapeDtypeStruct((t, D_MODEL), jnp.float32),
        scratch_shapes=[pltpu.VMEM((MIX_TM, D_FF), jnp.bfloat16),
                        pltpu.VMEM((D_MODEL, D_MODEL), jnp.bfloat16),
                        pltpu.VMEM((D_MODEL, D_FF), jnp.bfloat16), pltpu.VMEM((D_MODEL, D_FF), jnp.bfloat16),
                        pltpu.VMEM((D_FF, D_MODEL), jnp.bfloat16)] + _CAST_SCRATCH,
        name="mix_ffn", **_ROW_PARAMS,
    )(x, om, og, gr, mg, gg, wo, mpost_g, pre_g, wg, wu, wd, post_g)


def _moba_kernel(q_ref, k_ref, v_ref, kmean_ref, o_ref, kaug_ref, *, n_blk):
    blk = MOBA_BLOCK
    dh = MOBA_HEAD_DIM
    seq = n_blk * blk
    n_head = LANES // dh
    wide = n_head * blk

    key_blk = lax.broadcasted_iota(jnp.int32, (seq, LANES), 0) // blk
    hot_col = lax.broadcasted_iota(jnp.int32, (seq, LANES), 1)
    one_hot = jnp.where(key_blk == hot_col, 1.0, 0.0).astype(jnp.bfloat16)
    feat = lax.broadcasted_iota(jnp.int32, (LANES, blk), 0)
    head_rows = [(feat >= h * dh) & (feat < (h + 1) * dh) for h in range(n_head)]
    blk_row = lax.broadcasted_iota(jnp.int32, (n_blk, wide), 0)
    key_i = lax.broadcasted_iota(jnp.int32, (blk, wide), 0)
    qry_i = lax.broadcasted_iota(jnp.int32, (blk, wide), 1) % blk
    causal = key_i <= qry_i
    bias_pad = jnp.zeros((LANES - n_blk, wide), jnp.float32)
    ones = jnp.ones((SUM_ROWS, blk), jnp.float32)

    def column(c):
        lanes = slice(c * LANES, (c + 1) * LANES)
        kmean = kmean_ref[0][:, lanes]
        km_hi = kmean.astype(jnp.bfloat16)
        km_lo = (kmean - km_hi.astype(jnp.float32)).astype(jnp.bfloat16)
        kaug_ref[c, :, 0:LANES] = k_ref[:, lanes]
        kaug_ref[c, :, LANES:2 * LANES] = one_hot
        v_t = [None] * n_blk

        def value_block(j):
            if v_t[j] is None:
                v_j = v_ref[j * blk:(j + 1) * blk, lanes].astype(jnp.float32).T
                v_t[j] = jnp.concatenate([v_j, ones], axis=0).astype(jnp.bfloat16)
            return v_t[j]

        def section(i):
            st = {}
            n_keys = (i + 1) * blk

            def prep():
                q_i = q_ref[i * blk:(i + 1) * blk, lanes].astype(jnp.float32).T
                qi_f = jnp.concatenate([jnp.where(rows, q_i, 0.0) for rows in head_rows], axis=1)
                qi_t = qi_f.astype(jnp.bfloat16)
                if i > MOBA_TOPK:
                    gate = (jnp.dot(km_hi, qi_t, preferred_element_type=jnp.float32)
                            + jnp.dot(km_lo, qi_t, preferred_element_type=jnp.float32))
                    rank = jnp.zeros((n_blk, wide), jnp.float32)
                    for j in range(i):
                        gj = gate[j:j + 1, :]
                        beats = (gj > gate) | ((gj == gate) & (blk_row > j))
                        rank = rank + jnp.where(beats, 1.0, 0.0)
                    keep = (rank < float(MOBA_TOPK)) | (blk_row >= i)
                    bias = jnp.where(keep, 0.0, NEG_BIG)
                    st["q"] = jnp.concatenate([qi_f, bias, bias_pad], axis=0).astype(jnp.bfloat16)
                else:
                    st["q"] = qi_t

            def scores():
                if i > MOBA_TOPK:
                    s = jnp.dot(kaug_ref[c, 0:n_keys, :], st.pop("q"),
                                preferred_element_type=jnp.float32)
                else:
                    s = jnp.dot(k_ref[0:n_keys, lanes], st.pop("q"), preferred_element_type=jnp.float32)
                s_own = jnp.where(causal, s[i * blk:n_keys, :], NEG_BIG)
                sm = s_own if i == 0 else jnp.concatenate([s[0:i * blk, :], s_own], axis=0)
                st["s"] = sm
                st["m"] = jnp.max(sm, axis=0, keepdims=True)

            def probs():
                st["p"] = jnp.exp2(st.pop("s") - st.pop("m")).astype(jnp.bfloat16)

            def values():
                v_i = jnp.concatenate([value_block(j) for j in range(i + 1)], axis=1)
                o_t = jnp.dot(v_i, st.pop("p"), preferred_element_type=jnp.float32)
                o_t = o_t[0:LANES, :] * (1.0 / o_t[LANES:LANES + 1, :])
                o_own = jnp.concatenate([o_t[h * dh:(h + 1) * dh, h * blk:(h + 1) * blk]
                                         for h in range(n_head)], axis=0)
                o_ref[i * blk:(i + 1) * blk, lanes] = o_own.T.astype(o_ref.dtype)

            return [prep, scores, probs, values]

        return [section(j) + section(n_blk - 1 - j) for j in range(n_blk // 2)]

    _interleave([stream for c in range(MOBA_COLS) for stream in column(c)], lag=1)


def _moba(mq, mk, mv, kmean, batch, seq):
    n_blk = seq // MOBA_BLOCK
    width = MOBA_COLS * LANES
    tok = pl.BlockSpec((seq, width), lambda b, c: (b, c))
    return pl.pallas_call(
        functools.partial(_moba_kernel, n_blk=n_blk),
        grid=(batch, MOBA_WIDTH // width),
        in_specs=[tok, tok, tok, pl.BlockSpec((1, n_blk, width), lambda b, c: (b, 0, c))],
        out_specs=tok,
        out_shape=jax.ShapeDtypeStruct((batch * seq, MOBA_WIDTH), jnp.bfloat16),
        scratch_shapes=[pltpu.VMEM((MOBA_COLS, seq, 2 * LANES), jnp.bfloat16)],
        compiler_params=pltpu.CompilerParams(
            dimension_semantics=("arbitrary", "arbitrary"), vmem_limit_bytes=VMEM_LIMIT),
        name="moba",
    )(mq, mk, mv, kmean)


def _gla_kernel(q_ref, k_ref, v_ref, la_ref, o_ref, st_ref):
    @pl.when(pl.program_id(1) == 0)
    def _():
        st_ref[...] = jnp.zeros_like(st_ref)

    for r in range(GLA_STEP // GLA_TILE):
        rows = slice(r * GLA_TILE, (r + 1) * GLA_TILE)
        _gla_tile(q_ref[rows, :], k_ref[rows, :], v_ref.at[rows, :], la_ref[rows, :],
                  o_ref.at[rows, :], st_ref)


def _gla_tile(q, k, v_ref, la, o_ref, st_ref):
    c = GLA_TILE
    dk, dv = GLA_KEY_DIM, GLA_VALUE_DIM
    row = lax.broadcasted_iota(jnp.int32, la.shape, 0)
    lane = lax.broadcasted_iota(jnp.int32, la.shape, 1)
    odd_head = (lane & dk) != 0
    st_lane = lax.broadcasted_iota(jnp.int32, (dv, GLA_K_WIDTH), 1)

    b = la
    step = 1
    while step < c:
        b = b + jnp.where(row >= step, pltpu.roll(b, step, 0), 0.0)
        step *= 2

    qf = q.astype(jnp.float32) * (dk ** -0.5)
    kf = k.astype(jnp.float32)

    sub = c // 2
    t_i = lax.broadcasted_iota(jnp.int32, (sub, sub), 0)
    s_i = lax.broadcasted_iota(jnp.int32, (sub, sub), 1)
    level = jnp.where(t_i >= s_i, 31 - lax.clz(t_i ^ s_i), -2)

    def split_heads(x):
        return (jnp.where(odd_head, 0.0, x).astype(jnp.bfloat16),
                jnp.where(odd_head, x, 0.0).astype(jnp.bfloat16))

    def level_products(q_lvl, k_lvl, row_blk, col_blk):
        q_b = q_lvl.astype(jnp.bfloat16)
        k_even, k_odd = split_heads(k_lvl)
        rows = slice(row_blk * sub, (row_blk + 1) * sub)
        keys = slice(col_blk * sub, (col_blk + 1) * sub)
        out = []
        for pair in range(GLA_HEADS // 2):
            cols = slice(pair * LANES, (pair + 1) * LANES)
            k_pair = jnp.concatenate([k_even[keys, cols], k_odd[keys, cols]], axis=0)
            a = lax.dot_general(q_b[rows, cols], k_pair, _NT, preferred_element_type=jnp.float32)
            out += [a[:, 0:sub], a[:, sub:2 * sub]]
        return out

    def add_level(acc, q_lvl, k_lvl, lvl):
        hit = level == lvl
        for r in range(2):
            for h, a in enumerate(level_products(q_lvl, k_lvl, r, r)):
                acc[r][h] = jnp.where(hit, a, acc[r][h])
        return acc

    acc = [[jnp.zeros((sub, sub), jnp.float32) for _ in range(GLA_HEADS)] for _ in range(2)]
    acc = add_level(acc, qf, kf, -1)
    a_cross = None
    b_end = b
    half = 1
    lvl = 0
    while half < c:
        b_before = pltpu.roll(b_end, half, 0)
        q_lvl = qf * jnp.exp2(jnp.minimum(b - b_before, 0.0))
        k_lvl = kf * jnp.exp2(b_end - b)
        if half < sub:
            acc = add_level(acc, q_lvl, k_lvl, lvl)
        else:
            a_cross = level_products(q_lvl, k_lvl, 1, 0)
        upper = (row & half) != 0
        b_end = jnp.where(upper, b_end, pltpu.roll(b_end, c - half, 0))
        half *= 2
        lvl += 1
    q_in = (qf * jnp.exp2(b)).astype(jnp.bfloat16)
    k_out = (kf * jnp.exp2(b_end - b)).astype(jnp.bfloat16)
    tile_decay = jnp.exp2(b_end[0:1, :])

    for h in range(GLA_HEADS):
        cols = slice((h // 2) * LANES, (h // 2 + 1) * LANES)
        v_h = v_ref[:, h * dv:(h + 1) * dv]
        st = st_ref[h]
        o_inter = lax.dot_general(q_in[:, cols], st[:, cols].astype(jnp.bfloat16), _NT,
                                  preferred_element_type=jnp.float32)
        a_lo = acc[0][h].astype(jnp.bfloat16)
        a_hi = jnp.concatenate([a_cross[h], acc[1][h]], axis=1).astype(jnp.bfloat16)
        o_intra = jnp.concatenate(
            [jnp.dot(a_lo, v_h[0:sub, :], preferred_element_type=jnp.float32),
             jnp.dot(a_hi, v_h, preferred_element_type=jnp.float32)], axis=0)
        o_ref[:, h * dv:(h + 1) * dv] = (o_inter + o_intra).astype(o_ref.dtype)
        v_t = v_h.astype(jnp.float32).T.astype(jnp.bfloat16)
        upd = jnp.dot(v_t, k_out, preferred_element_type=jnp.float32)
        own = (st_lane >= h * dk) & (st_lane < (h + 1) * dk)
        st_ref[h] = st * tile_decay + jnp.where(own, upd, 0.0)


def _gla(gq, gk, gv, la, batch, seq):
    n_tile = seq // GLA_STEP

    def tok(width):
        return pl.BlockSpec((GLA_STEP, width), lambda b, t: (b * n_tile + t, 0))

    return pl.pallas_call(
        _gla_kernel,
        grid=(batch, n_tile),
        in_specs=[tok(GLA_K_WIDTH), tok(GLA_K_WIDTH), tok(GLA_V_WIDTH), tok(GLA_K_WIDTH)],
        out_specs=tok(GLA_V_WIDTH),
        out_shape=jax.ShapeDtypeStruct((batch * seq, GLA_V_WIDTH), jnp.bfloat16),
        scratch_shapes=[pltpu.VMEM((GLA_HEADS, GLA_VALUE_DIM, GLA_K_WIDTH), jnp.float32)],
        compiler_params=pltpu.CompilerParams(
            dimension_semantics=("arbitrary", "arbitrary"), vmem_limit_bytes=VMEM_LIMIT),
        name="gla",
    )(gq, gk, gv, la)


def kernel(x, ffn1_pre_g, ffn1_w_gate, ffn1_w_up, ffn1_w_down, ffn1_post_g, mix_pre_g, w_in,
           gla_w_alpha_up, gla_b_alpha, moba_out_g, gla_out_g, w_out, mix_post_g, ffn2_pre_g,
           ffn2_w_gate, ffn2_w_up, ffn2_w_down, ffn2_post_g):
    batch, seq, d = x.shape
    assert d == D_MODEL and seq % MOBA_BLOCK == 0 and seq % GLA_STEP == 0
    assert (batch * seq) % FFN_TM == 0 and (batch * seq) % MIX_TM == 0
    bf = jnp.bfloat16
    xt = x.reshape(batch * seq, d)
    for l in range(w_in.shape[0]):
        wup = jnp.pad(gla_w_alpha_up[l], ((0, GATE_PAD - GLA_GATE_RANK), (0, 0))).astype(bf)
        xt, mq, mk, mv, kmean, gq, gk, gv, gr, la = _ffn_proj(
            xt, ffn1_pre_g[l][None], ffn1_w_gate[l], ffn1_w_up[l], ffn1_w_down[l], ffn1_post_g[l][None],
            mix_pre_g[l][None], w_in[l], wup, gla_b_alpha[l][None])

        kmean = kmean.reshape(batch, seq // MOBA_BLOCK, MOBA_WIDTH)
        o_moba = _moba(mq, mk, mv, kmean, batch, seq)
        o_gla = _gla(gq, gk, gv, la, batch, seq)

        xt = _mix_ffn(xt, o_moba, o_gla, gr, moba_out_g[l][None],
                      gla_out_g[l].reshape(1, GLA_V_WIDTH), w_out[l], mix_post_g[l][None],
                      ffn2_pre_g[l][None], ffn2_w_gate[l], ffn2_w_up[l], ffn2_w_down[l],
                      ffn2_post_g[l][None])
    return xt.reshape(batch, seq, d)
```

```python
import functools

import jax
import jax.numpy as jnp
from jax import lax
from jax.experimental import pallas as pl
from jax.experimental.pallas import tpu as pltpu

D_MODEL = 1024
D_FF = 2816
RMS_EPS = 1e-6

MOBA_WIDTH = 512
MOBA_HEAD_DIM = 64
MOBA_BLOCK = 256
MOBA_TOPK = 3
MOBA_COLS = 2
LOG2_E = 1.4426950408889634
MOBA_Q_SCALE = MOBA_HEAD_DIM ** -0.5 * LOG2_E

GLA_HEADS = 4
GLA_KEY_DIM = 64
GLA_VALUE_DIM = 128
GLA_K_WIDTH = GLA_HEADS * GLA_KEY_DIM
GLA_V_WIDTH = GLA_HEADS * GLA_VALUE_DIM
GLA_GATE_RANK = 16
GLA_GATE_TAU = 16.0
GLA_TILE = 256
GLA_STEP = 1024

LANES = 128
GATE_PAD = LANES
NEG_BIG = -1e30
SUM_ROWS = 16

FFN_TM = 512
FFN_TF = 256
CAST_ROWS = 256
CAST_SLOTS = 2
PROJ_MAIN = 3 * MOBA_WIDTH + 2 * GLA_K_WIDTH + GLA_V_WIDTH
MIX_TM = 1024
MIX_SUB = 256
VMEM_LIMIT = 56 * 1024 * 1024

_NT = (((1,), (1,)), ((), ()))


def _rms(x, g):
    return x * lax.rsqrt(jnp.mean(x * x, axis=-1, keepdims=True) + RMS_EPS) * g


def _silu(x):
    return x / (1.0 + jnp.exp(-x))


def _resident(shape):
    return pl.BlockSpec(shape, lambda *_: (0,) * len(shape), pipeline_mode=pl.Buffered(1))


def _interleave(stage_lists, lag):
    n_slots = max(len(st) + j * lag for j, st in enumerate(stage_lists))
    for k in range(n_slots):
        for j, st in enumerate(stage_lists):
            if 0 <= k - j * lag < len(st):
                st[k - j * lag]()


def _cast_weight(src_hbm, stage_ref, sem, store, n_rows, col0, n_cols, row0=0, chunk_rows=CAST_ROWS):
    n_chunk = n_rows // chunk_rows

    def copy(c):
        slot = c % CAST_SLOTS
        return pltpu.make_async_copy(
            src_hbm.at[pl.ds(row0 + c * chunk_rows, chunk_rows), pl.ds(col0, n_cols)],
            stage_ref.at[slot, pl.ds(0, chunk_rows), pl.ds(0, n_cols)], sem.at[slot])

    for c in range(min(CAST_SLOTS - 1, n_chunk)):
        copy(c).start()
    for c in range(n_chunk):
        copy(c).wait()
        store(slice(c * chunk_rows, (c + 1) * chunk_rows),
              stage_ref[c % CAST_SLOTS, 0:chunk_rows, 0:n_cols])
        if c + CAST_SLOTS - 1 < n_chunk:
            copy(c + CAST_SLOTS - 1).start()


def _cast_into(src_hbm, dst_ref, stage_ref, sem, col0=0):
    def store(rows, chunk):
        dst_ref[rows, :] = chunk.astype(jnp.bfloat16)

    _cast_weight(src_hbm, stage_ref, sem, store, dst_ref.shape[0], col0, dst_ref.shape[1])


_HBM = pl.BlockSpec(memory_space=pl.ANY)
_CAST_SCRATCH = [pltpu.VMEM((CAST_SLOTS, CAST_ROWS, D_FF), jnp.float32),
                 pltpu.SemaphoreType.DMA((CAST_SLOTS,))]


def _row(width, rows=FFN_TM):
    return pl.BlockSpec((rows, width), lambda i: (i, 0))


_ROW_PARAMS = dict(
    compiler_params=pltpu.CompilerParams(
        dimension_semantics=("arbitrary",), vmem_limit_bytes=VMEM_LIMIT))


def _ffn_proj_stages(r, x_ref, pre_ref, post_ref, g_ref, wup_ref, ba_ref,
                     x1_ref, mq_ref, mk_ref, mv_ref, kmean_ref, gq_ref, gk_ref, gv_ref, gr_ref, la_ref,
                     h_ref, wg_ref, wu_ref, wd_ref, wmain_ref, wgr_ref, wat_ref):
    rows = slice(r * MOBA_BLOCK, (r + 1) * MOBA_BLOCK)
    st = {}

    def prenorm():
        st["xn"] = _rms(x_ref[rows, :], pre_ref[...]).astype(jnp.bfloat16)

    def up(c):
        def run():
            cols = slice(c * FFN_TF, (c + 1) * FFN_TF)
            g = jnp.dot(st["xn"], wg_ref[:, cols], preferred_element_type=jnp.float32)
            u = jnp.dot(st["xn"], wu_ref[:, cols], preferred_element_type=jnp.float32)
            h_ref[rows, cols] = (_silu(g) * u).astype(jnp.bfloat16)
        return run

    def down():
        st["f"] = jnp.dot(h_ref[rows, :], wd_ref[...], preferred_element_type=jnp.float32)

    def mid():
        x1 = x_ref[rows, :] + 0.5 * _rms(st.pop("f"), post_ref[...])
        x1_ref[rows, :] = x1
        st["h"] = _rms(x1, g_ref[...]).astype(jnp.bfloat16)

    def proj(w_ref, lo, hi):
        return jnp.dot(st["h"], w_ref[:, lo:hi], preferred_element_type=jnp.float32)

    w, kw, vw = MOBA_WIDTH, GLA_K_WIDTH, GLA_V_WIDTH

    def moba_q():
        mq_ref[rows, :] = (proj(wmain_ref, 0, w) * MOBA_Q_SCALE).astype(jnp.bfloat16)

    def moba_k():
        mk = proj(wmain_ref, w, 2 * w)
        mk_ref[rows, :] = mk.astype(jnp.bfloat16)
        kmean_ref[r] = jnp.mean(mk, axis=0, keepdims=True)

    def moba_v():
        mv_ref[rows, :] = proj(wmain_ref, 2 * w, 3 * w).astype(jnp.bfloat16)

    def gla_qk():
        gq_ref[rows, :] = proj(wmain_ref, 3 * w, 3 * w + kw).astype(jnp.bfloat16)
        gk_ref[rows, :] = proj(wmain_ref, 3 * w + kw, 3 * w + 2 * kw).astype(jnp.bfloat16)

    def gla_v():
        gv_ref[rows, :] = proj(wmain_ref, 3 * w + 2 * kw, PROJ_MAIN).astype(jnp.bfloat16)

    def gla_gate():
        gr_ref[rows, :] = proj(wgr_ref, 0, vw).astype(jnp.bfloat16)

    def gla_decay():
        ga = lax.dot_general(st["h"], wat_ref[...], _NT, preferred_element_type=jnp.float32)
        z = jnp.dot(ga.astype(jnp.bfloat16), wup_ref[...],
                    preferred_element_type=jnp.float32) + ba_ref[...]
        log_sig = jnp.minimum(z, 0.0) - jnp.log(1.0 + jnp.exp(-jnp.abs(z)))
        la_ref[rows, :] = log_sig * (LOG2_E / GLA_GATE_TAU)

    return ([prenorm] + [up(c) for c in range(D_FF // FFN_TF)] + [down, mid]
            + [moba_q, moba_k, moba_v, gla_qk, gla_v, gla_gate, gla_decay])


def _ffn_proj_kernel(x_ref, pre_ref, wg_hbm, wu_hbm, wd_hbm, post_ref, g_ref, wint_hbm, *rest):
    wg_ref, wu_ref, wd_ref, wmain_ref, wgr_ref, wat_ref, stage_ref, sem = rest[-8:]

    @pl.when(pl.program_id(0) == 0)
    def _():
        _cast_into(wg_hbm, wg_ref, stage_ref, sem)
        _cast_into(wu_hbm, wu_ref, stage_ref, sem)
        _cast_into(wd_hbm, wd_ref, stage_ref, sem)

        def store_main(rows, chunk):
            wmain_ref[:, rows] = chunk.T.astype(jnp.bfloat16)

        _cast_weight(wint_hbm, stage_ref, sem, store_main, PROJ_MAIN, 0, D_MODEL)

        def store_gate(rows, chunk):
            wgr_ref[:, rows] = chunk.T.astype(jnp.bfloat16)

        _cast_weight(wint_hbm, stage_ref, sem, store_gate, GLA_V_WIDTH, 0, D_MODEL,
                     row0=PROJ_MAIN + GLA_GATE_RANK)

        def store_rank(rows, chunk):
            wat_ref[rows, :] = chunk.astype(jnp.bfloat16)

        _cast_weight(wint_hbm, stage_ref, sem, store_rank, GATE_PAD, 0, D_MODEL, row0=PROJ_MAIN,
                     chunk_rows=GATE_PAD)

    refs = (x_ref, pre_ref, post_ref, g_ref) + rest[:-2]
    _interleave([_ffn_proj_stages(r, *refs) for r in range(FFN_TM // MOBA_BLOCK)], lag=4)


def _ffn_proj(x, pre_g, wg, wu, wd, post_g, g, w_in, wup, ba):
    t = x.shape[0]
    nblk = FFN_TM // MOBA_BLOCK
    small = (wup, ba)

    def out(width, dtype=jnp.bfloat16):
        return jax.ShapeDtypeStruct((t, width), dtype)

    return pl.pallas_call(
        _ffn_proj_kernel,
        grid=(t // FFN_TM,),
        in_specs=[_row(D_MODEL), _resident(pre_g.shape), _HBM, _HBM, _HBM, _resident(post_g.shape),
                  _resident(g.shape), _HBM] + [_resident(p.shape) for p in small],
        out_specs=[_row(D_MODEL), _row(MOBA_WIDTH), _row(MOBA_WIDTH), _row(MOBA_WIDTH),
                   pl.BlockSpec((nblk, 1, MOBA_WIDTH), lambda i: (i, 0, 0)),
                   _row(GLA_K_WIDTH), _row(GLA_K_WIDTH), _row(GLA_V_WIDTH), _row(GLA_V_WIDTH),
                   _row(GLA_K_WIDTH)],
        out_shape=[out(D_MODEL, jnp.float32), out(MOBA_WIDTH), out(MOBA_WIDTH), out(MOBA_WIDTH),
                   jax.ShapeDtypeStruct((t // MOBA_BLOCK, 1, MOBA_WIDTH), jnp.float32),
                   out(GLA_K_WIDTH), out(GLA_K_WIDTH), out(GLA_V_WIDTH), out(GLA_V_WIDTH),
                   out(GLA_K_WIDTH, jnp.float32)],
        scratch_shapes=[pltpu.VMEM((FFN_TM, D_FF), jnp.bfloat16),
                        pltpu.VMEM((D_MODEL, D_FF), jnp.bfloat16), pltpu.VMEM((D_MODEL, D_FF), jnp.bfloat16),
                        pltpu.VMEM((D_FF, D_MODEL), jnp.bfloat16),
                        pltpu.VMEM((D_MODEL, PROJ_MAIN), jnp.bfloat16),
                        pltpu.VMEM((D_MODEL, GLA_V_WIDTH), jnp.bfloat16),
                        pltpu.VMEM((GATE_PAD, D_MODEL), jnp.bfloat16)] + _CAST_SCRATCH,
        name="ffn_proj", **_ROW_PARAMS,
    )(x, pre_g, wg, wu, wd, post_g, g, w_in, *small)


def _mix_ffn_stages(rows, x_ref, om_ref, og_ref, gr_ref, mg_ref, gg_ref, mpost_ref, pre_ref, post_ref,
                    o_ref, h_ref, wo_ref, wg_ref, wu_ref, wd_ref):
    st = {}

    def mix_in():
        om = _rms(om_ref[rows, :].astype(jnp.float32), mg_ref[...]).astype(jnp.bfloat16)
        gate = _silu(gr_ref[rows, :].astype(jnp.float32))
        og = og_ref[rows, :].astype(jnp.float32)
        parts = [om]
        for h in range(GLA_HEADS):
            cols = slice(h * GLA_VALUE_DIM, (h + 1) * GLA_VALUE_DIM)
            parts.append((_rms(og[:, cols], gg_ref[:, cols]) * gate[:, cols]).astype(jnp.bfloat16))
        st["mix_in"] = jnp.concatenate(parts, axis=-1)

    def mix_out():
        st["mix"] = jnp.dot(st.pop("mix_in"), wo_ref[...], preferred_element_type=jnp.float32)

    def norms():
        x2 = x_ref[rows, :] + _rms(st.pop("mix"), mpost_ref[...])
        st["x2"] = x2
        st["xn"] = _rms(x2, pre_ref[...]).astype(jnp.bfloat16)

    def up(c):
        def run():
            cols = slice(c * FFN_TF, (c + 1) * FFN_TF)
            g = jnp.dot(st["xn"], wg_ref[:, cols], preferred_element_type=jnp.float32)
            u = jnp.dot(st["xn"], wu_ref[:, cols], preferred_element_type=jnp.float32)
            h_ref[rows, cols] = (_silu(g) * u).astype(jnp.bfloat16)
        return run

    def down():
        st["f"] = jnp.dot(h_ref[rows, :], wd_ref[...], preferred_element_type=jnp.float32)

    def finish():
        o_ref[rows, :] = st.pop("x2") + 0.5 * _rms(st.pop("f"), post_ref[...])

    return [mix_in, mix_out, norms] + [up(c) for c in range(D_FF // FFN_TF)] + [down, finish]


def _mix_ffn_kernel(x_ref, om_ref, og_ref, gr_ref, mg_ref, gg_ref, wo_hbm, mpost_ref,
                    pre_ref, wg_hbm, wu_hbm, wd_hbm, post_ref, o_ref,
                    h_ref, wo_ref, wg_ref, wu_ref, wd_ref, stage_ref, sem):
    @pl.when(pl.program_id(0) == 0)
    def _():
        _cast_into(wo_hbm, wo_ref, stage_ref, sem)
        _cast_into(wg_hbm, wg_ref, stage_ref, sem)
        _cast_into(wu_hbm, wu_ref, stage_ref, sem)
        _cast_into(wd_hbm, wd_ref, stage_ref, sem)

    refs = (x_ref, om_ref, og_ref, gr_ref, mg_ref, gg_ref, mpost_ref, pre_ref, post_ref,
            o_ref, h_ref, wo_ref, wg_ref, wu_ref, wd_ref)
    _interleave([_mix_ffn_stages(slice(r * MIX_SUB, (r + 1) * MIX_SUB), *refs)
                 for r in range(MIX_TM // MIX_SUB)], lag=4)


def _mix_ffn(x, om, og, gr, mg, gg, wo, mpost_g, pre_g, wg, wu, wd, post_g):
    t = x.shape[0]
    return pl.pallas_call(
        _mix_ffn_kernel,
        grid=(t // MIX_TM,),
        in_specs=[_row(D_MODEL, MIX_TM), _row(MOBA_WIDTH, MIX_TM), _row(GLA_V_WIDTH, MIX_TM),
                  _row(GLA_V_WIDTH, MIX_TM), _resident(mg.shape), _resident(gg.shape), _HBM,
                  _resident(mpost_g.shape), _resident(pre_g.shape), _HBM, _HBM, _HBM,
                  _resident(post_g.shape)],
        out_specs=_row(D_MODEL, MIX_TM),
        out_shape=jax.ShapeDtypeStruct((t, D_MODEL), jnp.float32),
        scratch_shapes=[pltpu.VMEM((MIX_TM, D_FF), jnp.bfloat16),
                        pltpu.VMEM((D_MODEL, D_MODEL), jnp.bfloat16),
                        pltpu.VMEM((D_MODEL, D_FF), jnp.bfloat16), pltpu.VMEM((D_MODEL, D_FF), jnp.bfloat16),
                        pltpu.VMEM((D_FF, D_MODEL), jnp.bfloat16)] + _CAST_SCRATCH,
        name="mix_ffn", **_ROW_PARAMS,
    )(x, om, og, gr, mg, gg, wo, mpost_g, pre_g, wg, wu, wd, post_g)


def _moba_kernel(q_ref, k_ref, v_ref, kmean_ref, o_ref, kaug_ref, *, n_blk):
    blk = MOBA_BLOCK
    dh = MOBA_HEAD_DIM
    seq = n_blk * blk
    n_head = LANES // dh
    wide = n_head * blk

    key_blk = lax.broadcasted_iota(jnp.int32, (seq, LANES), 0) // blk
    hot_col = lax.broadcasted_iota(jnp.int32, (seq, LANES), 1)
    one_hot = jnp.where(key_blk == hot_col, 1.0, 0.0).astype(jnp.bfloat16)
    feat = lax.broadcasted_iota(jnp.int32, (LANES, blk), 0)
    head_rows = [(feat >= h * dh) & (feat < (h + 1) * dh) for h in range(n_head)]
    blk_row = lax.broadcasted_iota(jnp.int32, (n_blk, wide), 0)
    key_i = lax.broadcasted_iota(jnp.int32, (blk, wide), 0)
    qry_i = lax.broadcasted_iota(jnp.int32, (blk, wide), 1) % blk
    causal = key_i <= qry_i
    bias_pad = jnp.zeros((LANES - n_blk, wide), jnp.float32)
    ones = jnp.ones((SUM_ROWS, blk), jnp.float32)

    def column(c):
        lanes = slice(c * LANES, (c + 1) * LANES)
        kmean = kmean_ref[0][:, lanes]
        km_hi = kmean.astype(jnp.bfloat16)
        km_lo = (kmean - km_hi.astype(jnp.float32)).astype(jnp.bfloat16)
        kaug_ref[c, :, 0:LANES] = k_ref[:, lanes]
        kaug_ref[c, :, LANES:2 * LANES] = one_hot
        v_t = [None] * n_blk

        def value_block(j):
            if v_t[j] is None:
                v_j = v_ref[j * blk:(j + 1) * blk, lanes].astype(jnp.float32).T
                v_t[j] = jnp.concatenate([v_j, ones], axis=0).astype(jnp.bfloat16)
            return v_t[j]

        def section(i):
            st = {}
            n_keys = (i + 1) * blk

            def prep():
                q_i = q_ref[i * blk:(i + 1) * blk, lanes].astype(jnp.float32).T
                qi_f = jnp.concatenate([jnp.where(rows, q_i, 0.0) for rows in head_rows], axis=1)
                qi_t = qi_f.astype(jnp.bfloat16)
                if i > MOBA_TOPK:
                    gate = (jnp.dot(km_hi, qi_t, preferred_element_type=jnp.float32)
                            + jnp.dot(km_lo, qi_t, preferred_element_type=jnp.float32))
                    rank = jnp.zeros((n_blk, wide), jnp.float32)
                    for j in range(i):
                        gj = gate[j:j + 1, :]
                        beats = (gj > gate) | ((gj == gate) & (blk_row > j))
                        rank = rank + jnp.where(beats, 1.0, 0.0)
                    keep = (rank < float(MOBA_TOPK)) | (blk_row >= i)
                    bias = jnp.where(keep, 0.0, NEG_BIG)
                    st["q"] = jnp.concatenate([qi_f, bias, bias_pad], axis=0).astype(jnp.bfloat16)
                else:
                    st["q"] = qi_t

            def scores():
                if i > MOBA_TOPK:
                    s = jnp.dot(kaug_ref[c, 0:n_keys, :], st.pop("q"),
                                preferred_element_type=jnp.float32)
                else:
                    s = jnp.dot(k_ref[0:n_keys, lanes], st.pop("q"), preferred_element_type=jnp.float32)
                s_own = jnp.where(causal, s[i * blk:n_keys, :], NEG_BIG)
                sm = s_own if i == 0 else jnp.concatenate([s[0:i * blk, :], s_own], axis=0)
                st["s"] = sm
                st["m"] = jnp.max(sm, axis=0, keepdims=True)

            def probs():
                st["p"] = jnp.exp2(st.pop("s") - st.pop("m")).astype(jnp.bfloat16)

            def values():
                v_i = jnp.concatenate([value_block(j) for j in range(i + 1)], axis=1)
                o_t = jnp.dot(v_i, st.pop("p"), preferred_element_type=jnp.float32)
                o_t = o_t[0:LANES, :] * (1.0 / o_t[LANES:LANES + 1, :])
                o_own = jnp.concatenate([o_t[h * dh:(h + 1) * dh, h * blk:(h + 1) * blk]
                                         for h in range(n_head)], axis=0)
                o_ref[i * blk:(i + 1) * blk, lanes] = o_own.T.astype(o_ref.dtype)

            return [prep, scores, probs, values]

        return [section(j) + section(n_blk - 1 - j) for j in range(n_blk // 2)]

    _interleave([stream for c in range(MOBA_COLS) for stream in column(c)], lag=1)


def _moba(mq, mk, mv, kmean, batch, seq):
    n_blk = seq // MOBA_BLOCK
    width = MOBA_COLS * LANES
    tok = pl.BlockSpec((seq, width), lambda b, c: (b, c))
    return pl.pallas_call(
        functools.partial(_moba_kernel, n_blk=n_blk),
        grid=(batch, MOBA_WIDTH // width),
        in_specs=[tok, tok, tok, pl.BlockSpec((1, n_blk, width), lambda b, c: (b, 0, c))],
        out_specs=tok,
        out_shape=jax.ShapeDtypeStruct((batch * seq, MOBA_WIDTH), jnp.bfloat16),
        scratch_shapes=[pltpu.VMEM((MOBA_COLS, seq, 2 * LANES), jnp.bfloat16)],
        compiler_params=pltpu.CompilerParams(
            dimension_semantics=("arbitrary", "arbitrary"), vmem_limit_bytes=VMEM_LIMIT),
        name="moba",
    )(mq, mk, mv, kmean)


def _gla_kernel(q_ref, k_ref, v_ref, la_ref, o_ref, st_ref):
    @pl.when(pl.program_id(1) == 0)
    def _():
        st_ref[...] = jnp.zeros_like(st_ref)

    for r in range(GLA_STEP // GLA_TILE):
        rows = slice(r * GLA_TILE, (r + 1) * GLA_TILE)
        _gla_tile(q_ref[rows, :], k_ref[rows, :], v_ref.at[rows, :], la_ref[rows, :],
                  o_ref.at[rows, :], st_ref)


def _gla_tile(q, k, v_ref, la, o_ref, st_ref):
    c = GLA_TILE
    dk, dv = GLA_KEY_DIM, GLA_VALUE_DIM
    row = lax.broadcasted_iota(jnp.int32, la.shape, 0)
    lane = lax.broadcasted_iota(jnp.int32, la.shape, 1)
    odd_head = (lane & dk) != 0
    st_lane = lax.broadcasted_iota(jnp.int32, (dv, GLA_K_WIDTH), 1)

    b = la
    step = 1
    while step < c:
        b = b + jnp.where(row >= step, pltpu.roll(b, step, 0), 0.0)
        step *= 2

    qf = q.astype(jnp.float32) * (dk ** -0.5)
    kf = k.astype(jnp.float32)

    sub = c // 2
    t_i = lax.broadcasted_iota(jnp.int32, (sub, sub), 0)
    s_i = lax.broadcasted_iota(jnp.int32, (sub, sub), 1)
    level = jnp.where(t_i >= s_i, 31 - lax.clz(t_i ^ s_i), -2)

    def split_heads(x):
        return (jnp.where(odd_head, 0.0, x).astype(jnp.bfloat16),
                jnp.where(odd_head, x, 0.0).astype(jnp.bfloat16))

    def level_products(q_lvl, k_lvl, row_blk, col_blk):
        q_b = q_lvl.astype(jnp.bfloat16)
        k_even, k_odd = split_heads(k_lvl)
        rows = slice(row_blk * sub, (row_blk + 1) * sub)
        keys = slice(col_blk * sub, (col_blk + 1) * sub)
        out = []
        for pair in range(GLA_HEADS // 2):
            cols = slice(pair * LANES, (pair + 1) * LANES)
            k_pair = jnp.concatenate([k_even[keys, cols], k_odd[keys, cols]], axis=0)
            a = lax.dot_general(q_b[rows, cols], k_pair, _NT, preferred_element_type=jnp.float32)
            out += [a[:, 0:sub], a[:, sub:2 * sub]]
        return out

    def add_level(acc, q_lvl, k_lvl, lvl):
        hit = level == lvl
        for r in range(2):
            for h, a in enumerate(level_products(q_lvl, k_lvl, r, r)):
                acc[r][h] = jnp.where(hit, a, acc[r][h])
        return acc

    acc = [[jnp.zeros((sub, sub), jnp.float32) for _ in range(GLA_HEADS)] for _ in range(2)]
    acc = add_level(acc, qf, kf, -1)
    a_cross = None
    b_end = b
    half = 1
    lvl = 0
    while half < c:
        b_before = pltpu.roll(b_end, half, 0)
        q_lvl = qf * jnp.exp2(jnp.minimum(b - b_before, 0.0))
        k_lvl = kf * jnp.exp2(b_end - b)
        if half < sub:
            acc = add_level(acc, q_lvl, k_lvl, lvl)
        else:
            a_cross = level_products(q_lvl, k_lvl, 1, 0)
        upper = (row & half) != 0
        b_end = jnp.where(upper, b_end, pltpu.roll(b_end, c - half, 0))
        half *= 2
        lvl += 1
    q_in = (qf * jnp.exp2(b)).astype(jnp.bfloat16)
    k_out = (kf * jnp.exp2(b_end - b)).astype(jnp.bfloat16)
    tile_decay = jnp.exp2(b_end[0:1, :])

    for h in range(GLA_HEADS):
        cols = slice((h // 2) * LANES, (h // 2 + 1) * LANES)
        v_h = v_ref[:, h * dv:(h + 1) * dv]
        st = st_ref[h]
        o_inter = lax.dot_general(q_in[:, cols], st[:, cols].astype(jnp.bfloat16), _NT,
                                  preferred_element_type=jnp.float32)
        a_lo = acc[0][h].astype(jnp.bfloat16)
        a_hi = jnp.concatenate([a_cross[h], acc[1][h]], axis=1).astype(jnp.bfloat16)
        o_intra = jnp.concatenate(
            [jnp.dot(a_lo, v_h[0:sub, :], preferred_element_type=jnp.float32),
             jnp.dot(a_hi, v_h, preferred_element_type=jnp.float32)], axis=0)
        o_ref[:, h * dv:(h + 1) * dv] = (o_inter + o_intra).astype(o_ref.dtype)
        v_t = v_h.astype(jnp.float32).T.astype(jnp.bfloat16)
        upd = jnp.dot(v_t, k_out, preferred_element_type=jnp.float32)
        own = (st_lane >= h * dk) & (st_lane < (h + 1) * dk)
        st_ref[h] = st * tile_decay + jnp.where(own, upd, 0.0)


def _gla(gq, gk, gv, la, batch, seq):
    n_tile = seq // GLA_STEP

    def tok(width):
        return pl.BlockSpec((GLA_STEP, width), lambda b, t: (b * n_tile + t, 0))

    return pl.pallas_call(
        _gla_kernel,
        grid=(batch, n_tile),
        in_specs=[tok(GLA_K_WIDTH), tok(GLA_K_WIDTH), tok(GLA_V_WIDTH), tok(GLA_K_WIDTH)],
        out_specs=tok(GLA_V_WIDTH),
        out_shape=jax.ShapeDtypeStruct((batch * seq, GLA_V_WIDTH), jnp.bfloat16),
        scratch_shapes=[pltpu.VMEM((GLA_HEADS, GLA_VALUE_DIM, GLA_K_WIDTH), jnp.float32)],
        compiler_params=pltpu.CompilerParams(
            dimension_semantics=("arbitrary", "arbitrary"), vmem_limit_bytes=VMEM_LIMIT),
        name="gla",
    )(gq, gk, gv, la)


def kernel(x, ffn1_pre_g, ffn1_w_gate, ffn1_w_up, ffn1_w_down, ffn1_post_g, mix_pre_g, w_in,
           gla_w_alpha_up, gla_b_alpha, moba_out_g, gla_out_g, w_out, mix_post_g, ffn2_pre_g,
           ffn2_w_gate, ffn2_w_up, ffn2_w_down, ffn2_post_g):
    batch, seq, d = x.shape
    assert d == D_MODEL and seq % MOBA_BLOCK == 0 and seq % GLA_STEP == 0
    assert (batch * seq) % FFN_TM == 0 and (batch * seq) % MIX_TM == 0
    bf = jnp.bfloat16
    xt = x.reshape(batch * seq, d)
    for l in range(w_in.shape[0]):
        wup = jnp.pad(gla_w_alpha_up[l], ((0, GATE_PAD - GLA_GATE_RANK), (0, 0))).astype(bf)
        xt, mq, mk, mv, kmean, gq, gk, gv, gr, la = _ffn_proj(
            xt, ffn1_pre_g[l][None], ffn1_w_gate[l], ffn1_w_up[l], ffn1_w_down[l], ffn1_post_g[l][None],
            mix_pre_g[l][None], w_in[l].T, wup, gla_b_alpha[l][None])

        kmean = kmean.reshape(batch, seq // MOBA_BLOCK, MOBA_WIDTH)
        o_moba = _moba(mq, mk, mv, kmean, batch, seq)
        o_gla = _gla(gq, gk, gv, la, batch, seq)

        xt = _mix_ffn(xt, o_moba, o_gla, gr, moba_out_g[l][None],
                      gla_out_g[l].reshape(1, GLA_V_WIDTH), w_out[l], mix_post_g[l][None],
                      ffn2_pre_g[l][None], ffn2_w_gate[l], ffn2_w_up[l], ffn2_w_down[l],
                      ffn2_post_g[l][None])
    return xt.reshape(batch, seq, d)
```

```python
import functools

import jax
import jax.numpy as jnp
from jax import lax
from jax.experimental import pallas as pl
from jax.experimental.pallas import tpu as pltpu

D_MODEL = 1024
D_FF = 2816
RMS_EPS = 1e-6

MOBA_WIDTH = 512
MOBA_HEAD_DIM = 64
MOBA_BLOCK = 256
MOBA_TOPK = 3
MOBA_COLS = 2
LOG2_E = 1.4426950408889634
MOBA_Q_SCALE = MOBA_HEAD_DIM ** -0.5 * LOG2_E

GLA_HEADS = 4
GLA_KEY_DIM = 64
GLA_VALUE_DIM = 128
GLA_K_WIDTH = GLA_HEADS * GLA_KEY_DIM
GLA_V_WIDTH = GLA_HEADS * GLA_VALUE_DIM
GLA_GATE_RANK = 16
GLA_GATE_TAU = 16.0
GLA_TILE = 256
GLA_STEP = 1024

LANES = 128
GATE_PAD = LANES
NEG_BIG = -1e30
SUM_ROWS = 16

FFN_TM = 512
FFN_TF = 256
CAST_ROWS = 256
CAST_SLOTS = 2
PROJ_MAIN = 3 * MOBA_WIDTH + 2 * GLA_K_WIDTH + GLA_V_WIDTH
MIX_TM = 1024
MIX_SUB = 256
VMEM_LIMIT = 56 * 1024 * 1024

_NT = (((1,), (1,)), ((), ()))


def _rms(x, g):
    return x * lax.rsqrt(jnp.mean(x * x, axis=-1, keepdims=True) + RMS_EPS) * g


def _silu(x):
    return x / (1.0 + jnp.exp(-x))


def _resident(shape):
    return pl.BlockSpec(shape, lambda *_: (0,) * len(shape), pipeline_mode=pl.Buffered(1))


def _interleave(stage_lists, lag):
    n_slots = max(len(st) + j * lag for j, st in enumerate(stage_lists))
    for k in range(n_slots):
        for j, st in enumerate(stage_lists):
            if 0 <= k - j * lag < len(st):
                st[k - j * lag]()


def _cast_weight(src_hbm, stage_ref, sem, store, n_rows, col0, n_cols, row0=0, chunk_rows=CAST_ROWS):
    n_chunk = n_rows // chunk_rows

    def copy(c):
        slot = c % CAST_SLOTS
        return pltpu.make_async_copy(
            src_hbm.at[pl.ds(row0 + c * chunk_rows, chunk_rows), pl.ds(col0, n_cols)],
            stage_ref.at[slot, pl.ds(0, chunk_rows), pl.ds(0, n_cols)], sem.at[slot])

    for c in range(min(CAST_SLOTS, n_chunk)):
        copy(c).start()
    for c in range(n_chunk):
        copy(c).wait()
        store(slice(c * chunk_rows, (c + 1) * chunk_rows),
              stage_ref[c % CAST_SLOTS, 0:chunk_rows, 0:n_cols])
        if c + CAST_SLOTS < n_chunk:
            copy(c + CAST_SLOTS).start()


def _cast_into(src_hbm, dst_ref, stage_ref, sem, col0=0):
    def store(rows, chunk):
        dst_ref[rows, :] = chunk.astype(jnp.bfloat16)

    _cast_weight(src_hbm, stage_ref, sem, store, dst_ref.shape[0], col0, dst_ref.shape[1])


_HBM = pl.BlockSpec(memory_space=pl.ANY)
_CAST_SCRATCH = [pltpu.VMEM((CAST_SLOTS, CAST_ROWS, D_FF), jnp.float32),
                 pltpu.SemaphoreType.DMA((CAST_SLOTS,))]


def _row(width, rows=FFN_TM):
    return pl.BlockSpec((rows, width), lambda i: (i, 0))


_ROW_PARAMS = dict(
    compiler_params=pltpu.CompilerParams(
        dimension_semantics=("arbitrary",), vmem_limit_bytes=VMEM_LIMIT))


def _ffn_proj_stages(r, x_ref, pre_ref, post_ref, g_ref, wup_ref, ba_ref,
                     x1_ref, mq_ref, mk_ref, mv_ref, kmean_ref, gq_ref, gk_ref, gv_ref, gr_ref, la_ref,
                     h_ref, wg_ref, wu_ref, wd_ref, wmain_ref, wgr_ref, wat_ref):
    rows = slice(r * MOBA_BLOCK, (r + 1) * MOBA_BLOCK)
    st = {}

    def prenorm():
        st["xn"] = _rms(x_ref[rows, :], pre_ref[...]).astype(jnp.bfloat16)

    def up(c):
        def run():
            cols = slice(c * FFN_TF, (c + 1) * FFN_TF)
            g = jnp.dot(st["xn"], wg_ref[:, cols], preferred_element_type=jnp.float32)
            u = jnp.dot(st["xn"], wu_ref[:, cols], preferred_element_type=jnp.float32)
            h_ref[rows, cols] = (_silu(g) * u).astype(jnp.bfloat16)
        return run

    def down():
        st["f"] = jnp.dot(h_ref[rows, :], wd_ref[...], preferred_element_type=jnp.float32)

    def mid():
        x1 = x_ref[rows, :] + 0.5 * _rms(st.pop("f"), post_ref[...])
        x1_ref[rows, :] = x1
        st["h"] = _rms(x1, g_ref[...]).astype(jnp.bfloat16)

    def proj(w_ref, lo, hi):
        return jnp.dot(st["h"], w_ref[:, lo:hi], preferred_element_type=jnp.float32)

    w, kw, vw = MOBA_WIDTH, GLA_K_WIDTH, GLA_V_WIDTH

    def moba_q():
        mq_ref[rows, :] = (proj(wmain_ref, 0, w) * MOBA_Q_SCALE).astype(jnp.bfloat16)

    def moba_k():
        mk = proj(wmain_ref, w, 2 * w)
        mk_ref[rows, :] = mk.astype(jnp.bfloat16)
        kmean_ref[r] = jnp.mean(mk, axis=0, keepdims=True)

    def moba_v():
        mv_ref[rows, :] = proj(wmain_ref, 2 * w, 3 * w).astype(jnp.bfloat16)

    def gla_qk():
        gq_ref[rows, :] = proj(wmain_ref, 3 * w, 3 * w + kw).astype(jnp.bfloat16)
        gk_ref[rows, :] = proj(wmain_ref, 3 * w + kw, 3 * w + 2 * kw).astype(jnp.bfloat16)

    def gla_v():
        gv_ref[rows, :] = proj(wmain_ref, 3 * w + 2 * kw, PROJ_MAIN).astype(jnp.bfloat16)

    def gla_gate():
        gr_ref[rows, :] = proj(wgr_ref, 0, vw).astype(jnp.bfloat16)

    def gla_decay():
        ga = lax.dot_general(st["h"], wat_ref[...], _NT, preferred_element_type=jnp.float32)
        z = jnp.dot(ga.astype(jnp.bfloat16), wup_ref[...],
                    preferred_element_type=jnp.float32) + ba_ref[...]
        log_sig = jnp.minimum(z, 0.0) - jnp.log(1.0 + jnp.exp(-jnp.abs(z)))
        la_ref[rows, :] = log_sig * (LOG2_E / GLA_GATE_TAU)

    return ([prenorm] + [up(c) for c in range(D_FF // FFN_TF)] + [down, mid]
            + [moba_q, moba_k, moba_v, gla_qk, gla_v, gla_gate, gla_decay])


def _ffn_proj_kernel(x_ref, pre_ref, wg_hbm, wu_hbm, wd_hbm, post_ref, g_ref, wint_hbm, *rest):
    wg_ref, wu_ref, wd_ref, wmain_ref, wgr_ref, wat_ref, stage_ref, sem = rest[-8:]

    @pl.when(pl.program_id(0) == 0)
    def _():
        _cast_into(wg_hbm, wg_ref, stage_ref, sem)
        _cast_into(wu_hbm, wu_ref, stage_ref, sem)
        _cast_into(wd_hbm, wd_ref, stage_ref, sem)

        def store_main(rows, chunk):
            wmain_ref[:, rows] = chunk.T.astype(jnp.bfloat16)

        _cast_weight(wint_hbm, stage_ref, sem, store_main, PROJ_MAIN, 0, D_MODEL)

        def store_gate(rows, chunk):
            wgr_ref[:, rows] = chunk.T.astype(jnp.bfloat16)

        _cast_weight(wint_hbm, stage_ref, sem, store_gate, GLA_V_WIDTH, 0, D_MODEL,
                     row0=PROJ_MAIN + GLA_GATE_RANK)

        def store_rank(rows, chunk):
            wat_ref[rows, :] = chunk.astype(jnp.bfloat16)

        _cast_weight(wint_hbm, stage_ref, sem, store_rank, GATE_PAD, 0, D_MODEL, row0=PROJ_MAIN,
                     chunk_rows=GATE_PAD)

    refs = (x_ref, pre_ref, post_ref, g_ref) + rest[:-2]
    _interleave([_ffn_proj_stages(r, *refs) for r in range(FFN_TM // MOBA_BLOCK)], lag=4)


def _ffn_proj(x, pre_g, wg, wu, wd, post_g, g, w_in, wup, ba):
    t = x.shape[0]
    nblk = FFN_TM // MOBA_BLOCK
    small = (wup, ba)

    def out(width, dtype=jnp.bfloat16):
        return jax.ShapeDtypeStruct((t, width), dtype)

    return pl.pallas_call(
        _ffn_proj_kernel,
        grid=(t // FFN_TM,),
        in_specs=[_row(D_MODEL), _resident(pre_g.shape), _HBM, _HBM, _HBM, _resident(post_g.shape),
                  _resident(g.shape), _HBM] + [_resident(p.shape) for p in small],
        out_specs=[_row(D_MODEL), _row(MOBA_WIDTH), _row(MOBA_WIDTH), _row(MOBA_WIDTH),
                   pl.BlockSpec((nblk, 1, MOBA_WIDTH), lambda i: (i, 0, 0)),
                   _row(GLA_K_WIDTH), _row(GLA_K_WIDTH), _row(GLA_V_WIDTH), _row(GLA_V_WIDTH),
                   _row(GLA_K_WIDTH)],
        out_shape=[out(D_MODEL, jnp.float32), out(MOBA_WIDTH), out(MOBA_WIDTH), out(MOBA_WIDTH),
                   jax.ShapeDtypeStruct((t // MOBA_BLOCK, 1, MOBA_WIDTH), jnp.float32),
                   out(GLA_K_WIDTH), out(GLA_K_WIDTH), out(GLA_V_WIDTH), out(GLA_V_WIDTH),
                   out(GLA_K_WIDTH, jnp.float32)],
        scratch_shapes=[pltpu.VMEM((FFN_TM, D_FF), jnp.bfloat16),
                        pltpu.VMEM((D_MODEL, D_FF), jnp.bfloat16), pltpu.VMEM((D_MODEL, D_FF), jnp.bfloat16),
                        pltpu.VMEM((D_FF, D_MODEL), jnp.bfloat16),
                        pltpu.VMEM((D_MODEL, PROJ_MAIN), jnp.bfloat16),
                        pltpu.VMEM((D_MODEL, GLA_V_WIDTH), jnp.bfloat16),
                        pltpu.VMEM((GATE_PAD, D_MODEL), jnp.bfloat16)] + _CAST_SCRATCH,
        name="ffn_proj", **_ROW_PARAMS,
    )(x, pre_g, wg, wu, wd, post_g, g, w_in, *small)


def _mix_ffn_stages(rows, x_ref, om_ref, og_ref, gr_ref, mg_ref, gg_ref, mpost_ref, pre_ref, post_ref,
                    o_ref, h_ref, wo_ref, wg_ref, wu_ref, wd_ref):
    st = {}

    def mix_in():
        om = _rms(om_ref[rows, :].astype(jnp.float32), mg_ref[...]).astype(jnp.bfloat16)
        gate = _silu(gr_ref[rows, :].astype(jnp.float32))
        og = og_ref[rows, :].astype(jnp.float32)
        parts = [om]
        for h in range(GLA_HEADS):
            cols = slice(h * GLA_VALUE_DIM, (h + 1) * GLA_VALUE_DIM)
            parts.append((_rms(og[:, cols], gg_ref[:, cols]) * gate[:, cols]).astype(jnp.bfloat16))
        st["mix_in"] = jnp.concatenate(parts, axis=-1)

    def mix_out():
        st["mix"] = jnp.dot(st.pop("mix_in"), wo_ref[...], preferred_element_type=jnp.float32)

    def norms():
        x2 = x_ref[rows, :] + _rms(st.pop("mix"), mpost_ref[...])
        st["x2"] = x2
        st["xn"] = _rms(x2, pre_ref[...]).astype(jnp.bfloat16)

    def up(c):
        def run():
            cols = slice(c * FFN_TF, (c + 1) * FFN_TF)
            g = jnp.dot(st["xn"], wg_ref[:, cols], preferred_element_type=jnp.float32)
            u = jnp.dot(st["xn"], wu_ref[:, cols], preferred_element_type=jnp.float32)
            h_ref[rows, cols] = (_silu(g) * u).astype(jnp.bfloat16)
        return run

    def down():
        st["f"] = jnp.dot(h_ref[rows, :], wd_ref[...], preferred_element_type=jnp.float32)

    def finish():
        o_ref[rows, :] = st.pop("x2") + 0.5 * _rms(st.pop("f"), post_ref[...])

    return [mix_in, mix_out, norms] + [up(c) for c in range(D_FF // FFN_TF)] + [down, finish]


def _mix_ffn_kernel(x_ref, om_ref, og_ref, gr_ref, mg_ref, gg_ref, wo_hbm, mpost_ref,
                    pre_ref, wg_hbm, wu_hbm, wd_hbm, post_ref, o_ref,
                    h_ref, wo_ref, wg_ref, wu_ref, wd_ref, stage_ref, sem):
    @pl.when(pl.program_id(0) == 0)
    def _():
        _cast_into(wo_hbm, wo_ref, stage_ref, sem)
        _cast_into(wg_hbm, wg_ref, stage_ref, sem)
        _cast_into(wu_hbm, wu_ref, stage_ref, sem)
        _cast_into(wd_hbm, wd_ref, stage_ref, sem)

    refs = (x_ref, om_ref, og_ref, gr_ref, mg_ref, gg_ref, mpost_ref, pre_ref, post_ref,
            o_ref, h_ref, wo_ref, wg_ref, wu_ref, wd_ref)
    _interleave([_mix_ffn_stages(slice(r * MIX_SUB, (r + 1) * MIX_SUB), *refs)
                 for r in range(MIX_TM // MIX_SUB)], lag=4)


def _mix_ffn(x, om, og, gr, mg, gg, wo, mpost_g, pre_g, wg, wu, wd, post_g):
    t = x.shape[0]
    return pl.pallas_call(
        _mix_ffn_kernel,
        grid=(t // MIX_TM,),
        in_specs=[_row(D_MODEL, MIX_TM), _row(MOBA_WIDTH, MIX_TM), _row(GLA_V_WIDTH, MIX_TM),
                  _row(GLA_V_WIDTH, MIX_TM), _resident(mg.shape), _resident(gg.shape), _HBM,
                  _resident(mpost_g.shape), _resident(pre_g.shape), _HBM, _HBM, _HBM,
                  _resident(post_g.shape)],
        out_specs=_row(D_MODEL, MIX_TM),
        out_shape=jax.ShapeDtypeStruct((t, D_MODEL), jnp.float32),
        scratch_shapes=[pltpu.VMEM((MIX_TM, D_FF), jnp.bfloat16),
                        pltpu.VMEM((D_MODEL, D_MODEL), jnp.bfloat16),
                        pltpu.VMEM((D_MODEL, D_FF), jnp.bfloat16), pltpu.VMEM((D_MODEL, D_FF), jnp.bfloat16),
                        pltpu.VMEM((D_FF, D_MODEL), jnp.bfloat16)] + _CAST_SCRATCH,
        name="mix_ffn", **_ROW_PARAMS,
    )(x, om, og, gr, mg, gg, wo, mpost_g, pre_g, wg, wu, wd, post_g)


def _moba_kernel(q_ref, k_ref, v_ref, kmean_ref, o_ref, kaug_ref, *, n_blk):
    blk = MOBA_BLOCK
    dh = MOBA_HEAD_DIM
    seq = n_blk * blk
    n_head = LANES // dh
    wide = n_head * blk

    key_blk = lax.broadcasted_iota(jnp.int32, (seq, LANES), 0) // blk
    hot_col = lax.broadcasted_iota(jnp.int32, (seq, LANES), 1)
    one_hot = jnp.where(key_blk == hot_col, 1.0, 0.0).astype(jnp.bfloat16)
    feat = lax.broadcasted_iota(jnp.int32, (LANES, blk), 0)
    head_rows = [(feat >= h * dh) & (feat < (h + 1) * dh) for h in range(n_head)]
    blk_row = lax.broadcasted_iota(jnp.int32, (n_blk, wide), 0)
    key_i = lax.broadcasted_iota(jnp.int32, (blk, wide), 0)
    qry_i = lax.broadcasted_iota(jnp.int32, (blk, wide), 1) % blk
    causal = key_i <= qry_i
    bias_pad = jnp.zeros((LANES - n_blk, wide), jnp.float32)
    ones = jnp.ones((SUM_ROWS, blk), jnp.float32)

    def column(c):
        lanes = slice(c * LANES, (c + 1) * LANES)
        kmean = kmean_ref[0][:, lanes]
        km_hi = kmean.astype(jnp.bfloat16)
        km_lo = (kmean - km_hi.astype(jnp.float32)).astype(jnp.bfloat16)
        kaug_ref[c, :, 0:LANES] = k_ref[:, lanes]
        kaug_ref[c, :, LANES:2 * LANES] = one_hot
        v_t = [None] * n_blk

        def value_block(j):
            if v_t[j] is None:
                v_j = v_ref[j * blk:(j + 1) * blk, lanes].astype(jnp.float32).T
                v_t[j] = jnp.concatenate([v_j, ones], axis=0).astype(jnp.bfloat16)
            return v_t[j]

        def section(i):
            st = {}
            n_keys = (i + 1) * blk

            def prep():
                q_i = q_ref[i * blk:(i + 1) * blk, lanes].astype(jnp.float32).T
                qi_f = jnp.concatenate([jnp.where(rows, q_i, 0.0) for rows in head_rows], axis=1)
                qi_t = qi_f.astype(jnp.bfloat16)
                if i > MOBA_TOPK:
                    gate = (jnp.dot(km_hi, qi_t, preferred_element_type=jnp.float32)
                            + jnp.dot(km_lo, qi_t, preferred_element_type=jnp.float32))
                    rank = jnp.zeros((n_blk, wide), jnp.float32)
                    for j in range(i):
                        gj = gate[j:j + 1, :]
                        beats = (gj > gate) | ((gj == gate) & (blk_row > j))
                        rank = rank + jnp.where(beats, 1.0, 0.0)
                    keep = (rank < float(MOBA_TOPK)) | (blk_row >= i)
                    bias = jnp.where(keep, 0.0, NEG_BIG)
                    st["q"] = jnp.concatenate([qi_f, bias, bias_pad], axis=0).astype(jnp.bfloat16)
                else:
                    st["q"] = qi_t

            def scores():
                if i > MOBA_TOPK:
                    s = jnp.dot(kaug_ref[c, 0:n_keys, :], st.pop("q"),
                                preferred_element_type=jnp.float32)
                else:
                    s = jnp.dot(k_ref[0:n_keys, lanes], st.pop("q"), preferred_element_type=jnp.float32)
                s_own = jnp.where(causal, s[i * blk:n_keys, :], NEG_BIG)
                sm = s_own if i == 0 else jnp.concatenate([s[0:i * blk, :], s_own], axis=0)
                st["s"] = sm
                st["m"] = jnp.max(sm, axis=0, keepdims=True)

            def probs():
                st["p"] = jnp.exp2(st.pop("s") - st.pop("m")).astype(jnp.bfloat16)

            def values():
                v_i = jnp.concatenate([value_block(j) for j in range(i + 1)], axis=1)
                o_t = jnp.dot(v_i, st.pop("p"), preferred_element_type=jnp.float32)
                o_t = o_t[0:LANES, :] * (1.0 / o_t[LANES:LANES + 1, :])
                o_own = jnp.concatenate([o_t[h * dh:(h + 1) * dh, h * blk:(h + 1) * blk]
                                         for h in range(n_head)], axis=0)
                o_ref[i * blk:(i + 1) * blk, lanes] = o_own.T.astype(o_ref.dtype)

            return [prep, scores, probs, values]

        return [section(j) + section(n_blk - 1 - j) for j in range(n_blk // 2)]

    _interleave([stream for c in range(MOBA_COLS) for stream in column(c)], lag=1)


def _moba(mq, mk, mv, kmean, batch, seq):
    n_blk = seq // MOBA_BLOCK
    width = MOBA_COLS * LANES
    tok = pl.BlockSpec((seq, width), lambda b, c: (b, c))
    return pl.pallas_call(
        functools.partial(_moba_kernel, n_blk=n_blk),
        grid=(batch, MOBA_WIDTH // width),
        in_specs=[tok, tok, tok, pl.BlockSpec((1, n_blk, width), lambda b, c: (b, 0, c))],
        out_specs=tok,
        out_shape=jax.ShapeDtypeStruct((batch * seq, MOBA_WIDTH), jnp.bfloat16),
        scratch_shapes=[pltpu.VMEM((MOBA_COLS, seq, 2 * LANES), jnp.bfloat16)],
        compiler_params=pltpu.CompilerParams(
            dimension_semantics=("arbitrary", "arbitrary"), vmem_limit_bytes=VMEM_LIMIT),
        name="moba",
    )(mq, mk, mv, kmean)


def _gla_kernel(q_ref, k_ref, v_ref, la_ref, o_ref, st_ref):
    @pl.when(pl.program_id(1) == 0)
    def _():
        st_ref[...] = jnp.zeros_like(st_ref)

    for r in range(GLA_STEP // GLA_TILE):
        rows = slice(r * GLA_TILE, (r + 1) * GLA_TILE)
        _gla_tile(q_ref[rows, :], k_ref[rows, :], v_ref.at[rows, :], la_ref[rows, :],
                  o_ref.at[rows, :], st_ref)


def _gla_tile(q, k, v_ref, la, o_ref, st_ref):
    c = GLA_TILE
    dk, dv = GLA_KEY_DIM, GLA_VALUE_DIM
    row = lax.broadcasted_iota(jnp.int32, la.shape, 0)
    lane = lax.broadcasted_iota(jnp.int32, la.shape, 1)
    odd_head = (lane & dk) != 0
    st_lane = lax.broadcasted_iota(jnp.int32, (dv, GLA_K_WIDTH), 1)

    b = la
    step = 1
    while step < c:
        b = b + jnp.where(row >= step, pltpu.roll(b, step, 0), 0.0)
        step *= 2

    qf = q.astype(jnp.float32) * (dk ** -0.5)
    kf = k.astype(jnp.float32)

    sub = c // 2
    t_i = lax.broadcasted_iota(jnp.int32, (sub, sub), 0)
    s_i = lax.broadcasted_iota(jnp.int32, (sub, sub), 1)
    level = jnp.where(t_i >= s_i, 31 - lax.clz(t_i ^ s_i), -2)

    def split_heads(x):
        return (jnp.where(odd_head, 0.0, x).astype(jnp.bfloat16),
                jnp.where(odd_head, x, 0.0).astype(jnp.bfloat16))

    def level_products(q_lvl, k_lvl, row_blk, col_blk):
        q_b = q_lvl.astype(jnp.bfloat16)
        k_even, k_odd = split_heads(k_lvl)
        rows = slice(row_blk * sub, (row_blk + 1) * sub)
        keys = slice(col_blk * sub, (col_blk + 1) * sub)
        out = []
        for pair in range(GLA_HEADS // 2):
            cols = slice(pair * LANES, (pair + 1) * LANES)
            k_pair = jnp.concatenate([k_even[keys, cols], k_odd[keys, cols]], axis=0)
            a = lax.dot_general(q_b[rows, cols], k_pair, _NT, preferred_element_type=jnp.float32)
            out += [a[:, 0:sub], a[:, sub:2 * sub]]
        return out

    def add_level(acc, q_lvl, k_lvl, lvl):
        hit = level == lvl
        for r in range(2):
            for h, a in enumerate(level_products(q_lvl, k_lvl, r, r)):
                acc[r][h] = jnp.where(hit, a, acc[r][h])
        return acc

    acc = [[jnp.zeros((sub, sub), jnp.float32) for _ in range(GLA_HEADS)] for _ in range(2)]
    acc = add_level(acc, qf, kf, -1)
    a_cross = None
    b_end = b
    half = 1
    lvl = 0
    while half < c:
        b_before = pltpu.roll(b_end, half, 0)
        q_lvl = qf * jnp.exp2(jnp.minimum(b - b_before, 0.0))
        k_lvl = kf * jnp.exp2(b_end - b)
        if half < sub:
            acc = add_level(acc, q_lvl, k_lvl, lvl)
        else:
            a_cross = level_products(q_lvl, k_lvl, 1, 0)
        upper = (row & half) != 0
        b_end = jnp.where(upper, b_end, pltpu.roll(b_end, c - half, 0))
        half *= 2
        lvl += 1
    q_in = (qf * jnp.exp2(b)).astype(jnp.bfloat16)
    k_out = (kf * jnp.exp2(b_end - b)).astype(jnp.bfloat16)
    tile_decay = jnp.exp2(b_end[0:1, :])

    for h in range(GLA_HEADS):
        cols = slice((h // 2) * LANES, (h // 2 + 1) * LANES)
        v_h = v_ref[:, h * dv:(h + 1) * dv]
        st = st_ref[h]
        o_inter = lax.dot_general(q_in[:, cols], st[:, cols].astype(jnp.bfloat16), _NT,
                                  preferred_element_type=jnp.float32)
        a_lo = acc[0][h].astype(jnp.bfloat16)
        a_hi = jnp.concatenate([a_cross[h], acc[1][h]], axis=1).astype(jnp.bfloat16)
        o_intra = jnp.concatenate(
            [jnp.dot(a_lo, v_h[0:sub, :], preferred_element_type=jnp.float32),
             jnp.dot(a_hi, v_h, preferred_element_type=jnp.float32)], axis=0)
        o_ref[:, h * dv:(h + 1) * dv] = (o_inter + o_intra).astype(o_ref.dtype)
        v_t = v_h.astype(jnp.float32).T.astype(jnp.bfloat16)
        upd = jnp.dot(v_t, k_out, preferred_element_type=jnp.float32)
        own = (st_lane >= h * dk) & (st_lane < (h + 1) * dk)
        st_ref[h] = st * tile_decay + jnp.where(own, upd, 0.0)


def _gla(gq, gk, gv, la, batch, seq):
    n_tile = seq // GLA_STEP

    def tok(width):
        return pl.BlockSpec((GLA_STEP, width), lambda b, t: (b * n_tile + t, 0))

    return pl.pallas_call(
        _gla_kernel,
        grid=(batch, n_tile),
        in_specs=[tok(GLA_K_WIDTH), tok(GLA_K_WIDTH), tok(GLA_V_WIDTH), tok(GLA_K_WIDTH)],
        out_specs=tok(GLA_V_WIDTH),
        out_shape=jax.ShapeDtypeStruct((batch * seq, GLA_V_WIDTH), jnp.bfloat16),
        scratch_shapes=[pltpu.VMEM((GLA_HEADS, GLA_VALUE_DIM, GLA_K_WIDTH), jnp.float32)],
        compiler_params=pltpu.CompilerParams(
            dimension_semantics=("arbitrary", "arbitrary"), vmem_limit_bytes=VMEM_LIMIT),
        name="gla",
    )(gq, gk, gv, la)


def kernel(x, ffn1_pre_g, ffn1_w_gate, ffn1_w_up, ffn1_w_down, ffn1_post_g, mix_pre_g, w_in,
           gla_w_alpha_up, gla_b_alpha, moba_out_g, gla_out_g, w_out, mix_post_g, ffn2_pre_g,
           ffn2_w_gate, ffn2_w_up, ffn2_w_down, ffn2_post_g):
    batch, seq, d = x.shape
    assert d == D_MODEL and seq % MOBA_BLOCK == 0 and seq % GLA_STEP == 0
    assert (batch * seq) % FFN_TM == 0 and (batch * seq) % MIX_TM == 0
    bf = jnp.bfloat16
    xt = x.reshape(batch * seq, d)
    for l in range(w_in.shape[0]):
        wup = jnp.pad(gla_w_alpha_up[l], ((0, GATE_PAD - GLA_GATE_RANK), (0, 0))).astype(bf)
        xt, mq, mk, mv, kmean, gq, gk, gv, gr, la = _ffn_proj(
            xt, ffn1_pre_g[l][None], ffn1_w_gate[l], ffn1_w_up[l], ffn1_w_down[l], ffn1_post_g[l][None],
            mix_pre_g[l][None], w_in[l].T, wup, gla_b_alpha[l][None])

        kmean = kmean.reshape(batch, seq // MOBA_BLOCK, MOBA_WIDTH)
        o_moba = _moba(mq, mk, mv, kmean, batch, seq)
        o_gla = _gla(gq, gk, gv, la, batch, seq)

        xt = _mix_ffn(xt, o_moba, o_gla, gr, moba_out_g[l][None],
                      gla_out_g[l].reshape(1, GLA_V_WIDTH), w_out[l], mix_post_g[l][None],
                      ffn2_pre_g[l][None], ffn2_w_gate[l], ffn2_w_up[l], ffn2_w_down[l],
                      ffn2_post_g[l][None])
    return xt.reshape(batch, seq, d)
```

```python
import functools

import jax
import jax.numpy as jnp
from jax import lax
from jax.experimental import pallas as pl
from jax.experimental.pallas import tpu as pltpu

D_MODEL = 1024
D_FF = 2816
RMS_EPS = 1e-6

MOBA_WIDTH = 512
MOBA_HEAD_DIM = 64
MOBA_BLOCK = 256
MOBA_TOPK = 3
MOBA_COLS = 2
LOG2_E = 1.4426950408889634
MOBA_Q_SCALE = MOBA_HEAD_DIM ** -0.5 * LOG2_E

GLA_HEADS = 4
GLA_KEY_DIM = 64
GLA_VALUE_DIM = 128
GLA_K_WIDTH = GLA_HEADS * GLA_KEY_DIM
GLA_V_WIDTH = GLA_HEADS * GLA_VALUE_DIM
GLA_GATE_RANK = 16
GLA_GATE_TAU = 16.0
GLA_TILE = 256
GLA_STEP = 1024

LANES = 128
GATE_PAD = LANES
NEG_BIG = -1e30
SUM_ROWS = 16

FFN_TM = 512
FFN_TF = 256
CAST_ROWS = 256
CAST_SLOTS = 2
PROJ_MAIN = 3 * MOBA_WIDTH + 2 * GLA_K_WIDTH + GLA_V_WIDTH
PROJ_OUT = PROJ_MAIN + GLA_V_WIDTH
MIX_TM = 1024
MIX_SUB = 256
VMEM_LIMIT = 56 * 1024 * 1024

_NT = (((1,), (1,)), ((), ()))


def _rms(x, g):
    return x * lax.rsqrt(jnp.mean(x * x, axis=-1, keepdims=True) + RMS_EPS) * g


def _silu(x):
    return x / (1.0 + jnp.exp(-x))


def _resident(shape):
    return pl.BlockSpec(shape, lambda *_: (0,) * len(shape), pipeline_mode=pl.Buffered(1))


def _interleave(stage_lists, lag):
    n_slots = max(len(st) + j * lag for j, st in enumerate(stage_lists))
    for k in range(n_slots):
        for j, st in enumerate(stage_lists):
            if 0 <= k - j * lag < len(st):
                st[k - j * lag]()


def _cast_weight(src_hbm, stage_ref, sem, store, n_rows, col0, n_cols, row0=0, chunk_rows=CAST_ROWS):
    n_chunk = n_rows // chunk_rows

    def copy(c):
        slot = c % CAST_SLOTS
        return pltpu.make_async_copy(
            src_hbm.at[pl.ds(row0 + c * chunk_rows, chunk_rows), pl.ds(col0, n_cols)],
            stage_ref.at[slot, pl.ds(0, chunk_rows), pl.ds(0, n_cols)], sem.at[slot])

    for c in range(min(CAST_SLOTS, n_chunk)):
        copy(c).start()
    for c in range(n_chunk):
        copy(c).wait()
        store(slice(c * chunk_rows, (c + 1) * chunk_rows),
              stage_ref[c % CAST_SLOTS, 0:chunk_rows, 0:n_cols])
        if c + CAST_SLOTS < n_chunk:
            copy(c + CAST_SLOTS).start()


def _cast_into(src_hbm, dst_ref, stage_ref, sem, col0=0):
    def store(rows, chunk):
        dst_ref[rows, :] = chunk.astype(jnp.bfloat16)

    _cast_weight(src_hbm, stage_ref, sem, store, dst_ref.shape[0], col0, dst_ref.shape[1])


_HBM = pl.BlockSpec(memory_space=pl.ANY)
_CAST_SCRATCH = [pltpu.VMEM((CAST_SLOTS, CAST_ROWS, D_FF), jnp.float32),
                 pltpu.SemaphoreType.DMA((CAST_SLOTS,))]


def _row(width, rows=FFN_TM):
    return pl.BlockSpec((rows, width), lambda i: (i, 0))


_ROW_PARAMS = dict(
    compiler_params=pltpu.CompilerParams(
        dimension_semantics=("arbitrary",), vmem_limit_bytes=VMEM_LIMIT))


def _ffn_proj_stages(r, x_ref, pre_ref, post_ref, g_ref, wup_ref, ba_ref,
                     x1_ref, pj_ref, kmean_ref, la_ref,
                     h_ref, wg_ref, wu_ref, wd_ref, wmain_ref, wgr_ref, wat_ref):
    rows = slice(r * MOBA_BLOCK, (r + 1) * MOBA_BLOCK)
    st = {}

    def prenorm():
        st["xn"] = _rms(x_ref[rows, :], pre_ref[...]).astype(jnp.bfloat16)

    def up(c):
        def run():
            cols = slice(c * FFN_TF, (c + 1) * FFN_TF)
            g = jnp.dot(st["xn"], wg_ref[:, cols], preferred_element_type=jnp.float32)
            u = jnp.dot(st["xn"], wu_ref[:, cols], preferred_element_type=jnp.float32)
            h_ref[rows, cols] = (_silu(g) * u).astype(jnp.bfloat16)
        return run

    def down():
        st["f"] = jnp.dot(h_ref[rows, :], wd_ref[...], preferred_element_type=jnp.float32)

    def mid():
        x1 = x_ref[rows, :] + 0.5 * _rms(st.pop("f"), post_ref[...])
        x1_ref[rows, :] = x1
        st["h"] = _rms(x1, g_ref[...]).astype(jnp.bfloat16)

    def proj(w_ref, lo, hi):
        return jnp.dot(st["h"], w_ref[:, lo:hi], preferred_element_type=jnp.float32)

    w, kw, vw = MOBA_WIDTH, GLA_K_WIDTH, GLA_V_WIDTH

    def moba_q():
        pj_ref[rows, 0:w] = (proj(wmain_ref, 0, w) * MOBA_Q_SCALE).astype(jnp.bfloat16)

    def moba_k():
        mk = proj(wmain_ref, w, 2 * w)
        pj_ref[rows, w:2 * w] = mk.astype(jnp.bfloat16)
        kmean_ref[r] = jnp.mean(mk, axis=0, keepdims=True)

    def moba_v():
        pj_ref[rows, 2 * w:3 * w] = proj(wmain_ref, 2 * w, 3 * w).astype(jnp.bfloat16)

    def gla_qk():
        pj_ref[rows, 3 * w:3 * w + 2 * kw] = proj(wmain_ref, 3 * w, 3 * w + 2 * kw).astype(jnp.bfloat16)

    def gla_v():
        pj_ref[rows, 3 * w + 2 * kw:PROJ_MAIN] = proj(wmain_ref, 3 * w + 2 * kw, PROJ_MAIN).astype(jnp.bfloat16)

    def gla_gate():
        pj_ref[rows, PROJ_MAIN:PROJ_MAIN + vw] = proj(wgr_ref, 0, vw).astype(jnp.bfloat16)

    def gla_decay():
        ga = lax.dot_general(st["h"], wat_ref[...], _NT, preferred_element_type=jnp.float32)
        z = jnp.dot(ga.astype(jnp.bfloat16), wup_ref[...],
                    preferred_element_type=jnp.float32) + ba_ref[...]
        log_sig = jnp.minimum(z, 0.0) - jnp.log(1.0 + jnp.exp(-jnp.abs(z)))
        la_ref[rows, :] = log_sig * (LOG2_E / GLA_GATE_TAU)

    return ([prenorm] + [up(c) for c in range(D_FF // FFN_TF)] + [down, mid]
            + [moba_q, moba_k, moba_v, gla_qk, gla_v, gla_gate, gla_decay])


def _ffn_proj_kernel(x_ref, pre_ref, wg_hbm, wu_hbm, wd_hbm, post_ref, g_ref, wint_hbm, *rest):
    wg_ref, wu_ref, wd_ref, wmain_ref, wgr_ref, wat_ref, stage_ref, sem = rest[-8:]

    @pl.when(pl.program_id(0) == 0)
    def _():
        _cast_into(wg_hbm, wg_ref, stage_ref, sem)
        _cast_into(wu_hbm, wu_ref, stage_ref, sem)
        _cast_into(wd_hbm, wd_ref, stage_ref, sem)

        def store_main(rows, chunk):
            wmain_ref[:, rows] = chunk.T.astype(jnp.bfloat16)

        _cast_weight(wint_hbm, stage_ref, sem, store_main, PROJ_MAIN, 0, D_MODEL)

        def store_gate(rows, chunk):
            wgr_ref[:, rows] = chunk.T.astype(jnp.bfloat16)

        _cast_weight(wint_hbm, stage_ref, sem, store_gate, GLA_V_WIDTH, 0, D_MODEL,
                     row0=PROJ_MAIN + GLA_GATE_RANK)

        def store_rank(rows, chunk):
            wat_ref[rows, :] = chunk.astype(jnp.bfloat16)

        _cast_weight(wint_hbm, stage_ref, sem, store_rank, GATE_PAD, 0, D_MODEL, row0=PROJ_MAIN,
                     chunk_rows=GATE_PAD)

    refs = (x_ref, pre_ref, post_ref, g_ref) + rest[:-2]
    _interleave([_ffn_proj_stages(r, *refs) for r in range(FFN_TM // MOBA_BLOCK)], lag=4)


def _ffn_proj(x, pre_g, wg, wu, wd, post_g, g, w_in, wup, ba):
    t = x.shape[0]
    nblk = FFN_TM // MOBA_BLOCK
    small = (wup, ba)

    def out(width, dtype=jnp.bfloat16):
        return jax.ShapeDtypeStruct((t, width), dtype)

    return pl.pallas_call(
        _ffn_proj_kernel,
        grid=(t // FFN_TM,),
        in_specs=[_row(D_MODEL), _resident(pre_g.shape), _HBM, _HBM, _HBM, _resident(post_g.shape),
                  _resident(g.shape), _HBM] + [_resident(p.shape) for p in small],
        out_specs=[_row(D_MODEL), _row(PROJ_OUT),
                   pl.BlockSpec((nblk, 1, MOBA_WIDTH), lambda i: (i, 0, 0)), _row(GLA_K_WIDTH)],
        out_shape=[out(D_MODEL, jnp.float32), out(PROJ_OUT),
                   jax.ShapeDtypeStruct((t // MOBA_BLOCK, 1, MOBA_WIDTH), jnp.float32),
                   out(GLA_K_WIDTH, jnp.float32)],
        scratch_shapes=[pltpu.VMEM((FFN_TM, D_FF), jnp.bfloat16),
                        pltpu.VMEM((D_MODEL, D_FF), jnp.bfloat16), pltpu.VMEM((D_MODEL, D_FF), jnp.bfloat16),
                        pltpu.VMEM((D_FF, D_MODEL), jnp.bfloat16),
                        pltpu.VMEM((D_MODEL, PROJ_MAIN), jnp.bfloat16),
                        pltpu.VMEM((D_MODEL, GLA_V_WIDTH), jnp.bfloat16),
                        pltpu.VMEM((GATE_PAD, D_MODEL), jnp.bfloat16)] + _CAST_SCRATCH,
        name="ffn_proj", **_ROW_PARAMS,
    )(x, pre_g, wg, wu, wd, post_g, g, w_in, *small)


def _mix_ffn_stages(rows, x_ref, om_ref, og_ref, gr_ref, mg_ref, gg_ref, mpost_ref, pre_ref, post_ref,
                    o_ref, h_ref, wo_ref, wg_ref, wu_ref, wd_ref):
    st = {}

    def mix_in():
        om = _rms(om_ref[rows, :].astype(jnp.float32), mg_ref[...]).astype(jnp.bfloat16)
        gate = _silu(gr_ref[rows, :].astype(jnp.float32))
        og = og_ref[rows, :].astype(jnp.float32)
        parts = [om]
        for h in range(GLA_HEADS):
            cols = slice(h * GLA_VALUE_DIM, (h + 1) * GLA_VALUE_DIM)
            parts.append((_rms(og[:, cols], gg_ref[:, cols]) * gate[:, cols]).astype(jnp.bfloat16))
        st["mix_in"] = jnp.concatenate(parts, axis=-1)

    def mix_out():
        st["mix"] = jnp.dot(st.pop("mix_in"), wo_ref[...], preferred_element_type=jnp.float32)

    def norms():
        x2 = x_ref[rows, :] + _rms(st.pop("mix"), mpost_ref[...])
        st["x2"] = x2
        st["xn"] = _rms(x2, pre_ref[...]).astype(jnp.bfloat16)

    def up(c):
        def run():
            cols = slice(c * FFN_TF, (c + 1) * FFN_TF)
            g = jnp.dot(st["xn"], wg_ref[:, cols], preferred_element_type=jnp.float32)
            u = jnp.dot(st["xn"], wu_ref[:, cols], preferred_element_type=jnp.float32)
            h_ref[rows, cols] = (_silu(g) * u).astype(jnp.bfloat16)
        return run

    def down():
        st["f"] = jnp.dot(h_ref[rows, :], wd_ref[...], preferred_element_type=jnp.float32)

    def finish():
        o_ref[rows, :] = st.pop("x2") + 0.5 * _rms(st.pop("f"), post_ref[...])

    return [mix_in, mix_out, norms] + [up(c) for c in range(D_FF // FFN_TF)] + [down, finish]


def _mix_ffn_kernel(x_ref, om_ref, og_ref, gr_ref, mg_ref, gg_ref, wo_hbm, mpost_ref,
                    pre_ref, wg_hbm, wu_hbm, wd_hbm, post_ref, o_ref,
                    h_ref, wo_ref, wg_ref, wu_ref, wd_ref, stage_ref, sem):
    @pl.when(pl.program_id(0) == 0)
    def _():
        _cast_into(wo_hbm, wo_ref, stage_ref, sem)
        _cast_into(wg_hbm, wg_ref, stage_ref, sem)
        _cast_into(wu_hbm, wu_ref, stage_ref, sem)
        _cast_into(wd_hbm, wd_ref, stage_ref, sem)

    refs = (x_ref, om_ref, og_ref, gr_ref, mg_ref, gg_ref, mpost_ref, pre_ref, post_ref,
            o_ref, h_ref, wo_ref, wg_ref, wu_ref, wd_ref)
    _interleave([_mix_ffn_stages(slice(r * MIX_SUB, (r + 1) * MIX_SUB), *refs)
                 for r in range(MIX_TM // MIX_SUB)], lag=4)


def _mix_ffn(x, om, og, pj, mg, gg, wo, mpost_g, pre_g, wg, wu, wd, post_g):
    t = x.shape[0]
    return pl.pallas_call(
        _mix_ffn_kernel,
        grid=(t // MIX_TM,),
        in_specs=[_row(D_MODEL, MIX_TM), _row(MOBA_WIDTH, MIX_TM), _row(GLA_V_WIDTH, MIX_TM),
                  pl.BlockSpec((MIX_TM, GLA_V_WIDTH), lambda i: (i, PROJ_MAIN // GLA_V_WIDTH)),
                  _resident(mg.shape), _resident(gg.shape), _HBM,
                  _resident(mpost_g.shape), _resident(pre_g.shape), _HBM, _HBM, _HBM,
                  _resident(post_g.shape)],
        out_specs=_row(D_MODEL, MIX_TM),
        out_shape=jax.ShapeDtypeStruct((t, D_MODEL), jnp.float32),
        scratch_shapes=[pltpu.VMEM((MIX_TM, D_FF), jnp.bfloat16),
                        pltpu.VMEM((D_MODEL, D_MODEL), jnp.bfloat16),
                        pltpu.VMEM((D_MODEL, D_FF), jnp.bfloat16), pltpu.VMEM((D_MODEL, D_FF), jnp.bfloat16),
                        pltpu.VMEM((D_FF, D_MODEL), jnp.bfloat16)] + _CAST_SCRATCH,
        name="mix_ffn", **_ROW_PARAMS,
    )(x, om, og, pj, mg, gg, wo, mpost_g, pre_g, wg, wu, wd, post_g)


def _moba_kernel(q_ref, k_ref, v_ref, kmean_ref, o_ref, kaug_ref, *, n_blk):
    blk = MOBA_BLOCK
    dh = MOBA_HEAD_DIM
    seq = n_blk * blk
    n_head = LANES // dh
    wide = n_head * blk

    key_blk = lax.broadcasted_iota(jnp.int32, (seq, LANES), 0) // blk
    hot_col = lax.broadcasted_iota(jnp.int32, (seq, LANES), 1)
    one_hot = jnp.where(key_blk == hot_col, 1.0, 0.0).astype(jnp.bfloat16)
    feat = lax.broadcasted_iota(jnp.int32, (LANES, blk), 0)
    head_rows = [(feat >= h * dh) & (feat < (h + 1) * dh) for h in range(n_head)]
    blk_row = lax.broadcasted_iota(jnp.int32, (n_blk, wide), 0)
    key_i = lax.broadcasted_iota(jnp.int32, (blk, wide), 0)
    qry_i = lax.broadcasted_iota(jnp.int32, (blk, wide), 1) % blk
    causal = key_i <= qry_i
    bias_pad = jnp.zeros((LANES - n_blk, wide), jnp.float32)
    ones = jnp.ones((SUM_ROWS, blk), jnp.float32)

    def column(c):
        lanes = slice(c * LANES, (c + 1) * LANES)
        kmean = kmean_ref[0][:, lanes]
        km_hi = kmean.astype(jnp.bfloat16)
        km_lo = (kmean - km_hi.astype(jnp.float32)).astype(jnp.bfloat16)
        kaug_ref[c, :, 0:LANES] = k_ref[:, lanes]
        kaug_ref[c, :, LANES:2 * LANES] = one_hot
        v_t = [None] * n_blk

        def value_block(j):
            if v_t[j] is None:
                v_j = v_ref[j * blk:(j + 1) * blk, lanes].astype(jnp.float32).T
                v_t[j] = jnp.concatenate([v_j, ones], axis=0).astype(jnp.bfloat16)
            return v_t[j]

        def section(i):
            st = {}
            n_keys = (i + 1) * blk

            def prep():
                q_i = q_ref[i * blk:(i + 1) * blk, lanes].astype(jnp.float32).T
                qi_f = jnp.concatenate([jnp.where(rows, q_i, 0.0) for rows in head_rows], axis=1)
                qi_t = qi_f.astype(jnp.bfloat16)
                if i > MOBA_TOPK:
                    gate = (jnp.dot(km_hi, qi_t, preferred_element_type=jnp.float32)
                            + jnp.dot(km_lo, qi_t, preferred_element_type=jnp.float32))
                    rank = jnp.zeros((n_blk, wide), jnp.float32)
                    for j in range(i):
                        gj = gate[j:j + 1, :]
                        beats = (gj > gate) | ((gj == gate) & (blk_row > j))
                        rank = rank + jnp.where(beats, 1.0, 0.0)
                    keep = (rank < float(MOBA_TOPK)) | (blk_row >= i)
                    bias = jnp.where(keep, 0.0, NEG_BIG)
                    st["q"] = jnp.concatenate([qi_f, bias, bias_pad], axis=0).astype(jnp.bfloat16)
                else:
                    st["q"] = qi_t

            def scores():
                if i > MOBA_TOPK:
                    s = jnp.dot(kaug_ref[c, 0:n_keys, :], st.pop("q"),
                                preferred_element_type=jnp.float32)
                else:
                    s = jnp.dot(k_ref[0:n_keys, lanes], st.pop("q"), preferred_element_type=jnp.float32)
                s_own = jnp.where(causal, s[i * blk:n_keys, :], NEG_BIG)
                sm = s_own if i == 0 else jnp.concatenate([s[0:i * blk, :], s_own], axis=0)
                st["s"] = sm
                st["m"] = jnp.max(sm, axis=0, keepdims=True)

            def probs():
                st["p"] = jnp.exp2(st.pop("s") - st.pop("m")).astype(jnp.bfloat16)

            def values():
                v_i = jnp.concatenate([value_block(j) for j in range(i + 1)], axis=1)
                o_t = jnp.dot(v_i, st.pop("p"), preferred_element_type=jnp.float32)
                o_t = o_t[0:LANES, :] * (1.0 / o_t[LANES:LANES + 1, :])
                o_own = jnp.concatenate([o_t[h * dh:(h + 1) * dh, h * blk:(h + 1) * blk]
                                         for h in range(n_head)], axis=0)
                o_ref[i * blk:(i + 1) * blk, lanes] = o_own.T.astype(o_ref.dtype)

            return [prep, scores, probs, values]

        return [section(j) + section(n_blk - 1 - j) for j in range(n_blk // 2)]

    _interleave([stream for c in range(MOBA_COLS) for stream in column(c)], lag=1)


def _moba(pj, kmean, batch, seq):
    n_blk = seq // MOBA_BLOCK
    width = MOBA_COLS * LANES
    n_col = MOBA_WIDTH // width

    def tok(group):
        return pl.BlockSpec((seq, width), lambda b, c: (b, group * n_col + c))

    return pl.pallas_call(
        functools.partial(_moba_kernel, n_blk=n_blk),
        grid=(batch, n_col),
        in_specs=[tok(0), tok(1), tok(2), pl.BlockSpec((1, n_blk, width), lambda b, c: (b, 0, c))],
        out_specs=tok(0),
        out_shape=jax.ShapeDtypeStruct((batch * seq, MOBA_WIDTH), jnp.bfloat16),
        scratch_shapes=[pltpu.VMEM((MOBA_COLS, seq, 2 * LANES), jnp.bfloat16)],
        compiler_params=pltpu.CompilerParams(
            dimension_semantics=("arbitrary", "arbitrary"), vmem_limit_bytes=VMEM_LIMIT),
        name="moba",
    )(pj, pj, pj, kmean)


def _gla_kernel(q_ref, k_ref, v_ref, la_ref, o_ref, st_ref):
    @pl.when(pl.program_id(1) == 0)
    def _():
        st_ref[...] = jnp.zeros_like(st_ref)

    for r in range(GLA_STEP // GLA_TILE):
        rows = slice(r * GLA_TILE, (r + 1) * GLA_TILE)
        _gla_tile(q_ref[rows, :], k_ref[rows, :], v_ref.at[rows, :], la_ref[rows, :],
                  o_ref.at[rows, :], st_ref)


def _gla_tile(q, k, v_ref, la, o_ref, st_ref):
    c = GLA_TILE
    dk, dv = GLA_KEY_DIM, GLA_VALUE_DIM
    row = lax.broadcasted_iota(jnp.int32, la.shape, 0)
    lane = lax.broadcasted_iota(jnp.int32, la.shape, 1)
    odd_head = (lane & dk) != 0
    st_lane = lax.broadcasted_iota(jnp.int32, (dv, GLA_K_WIDTH), 1)

    b = la
    step = 1
    while step < c:
        b = b + jnp.where(row >= step, pltpu.roll(b, step, 0), 0.0)
        step *= 2

    qf = q.astype(jnp.float32) * (dk ** -0.5)
    kf = k.astype(jnp.float32)

    sub = c // 2
    t_i = lax.broadcasted_iota(jnp.int32, (sub, sub), 0)
    s_i = lax.broadcasted_iota(jnp.int32, (sub, sub), 1)
    level = jnp.where(t_i >= s_i, 31 - lax.clz(t_i ^ s_i), -2)

    def split_heads(x):
        return (jnp.where(odd_head, 0.0, x).astype(jnp.bfloat16),
                jnp.where(odd_head, x, 0.0).astype(jnp.bfloat16))

    def level_products(q_lvl, k_lvl, row_blk, col_blk):
        q_b = q_lvl.astype(jnp.bfloat16)
        k_even, k_odd = split_heads(k_lvl)
        rows = slice(row_blk * sub, (row_blk + 1) * sub)
        keys = slice(col_blk * sub, (col_blk + 1) * sub)
        out = []
        for pair in range(GLA_HEADS // 2):
            cols = slice(pair * LANES, (pair + 1) * LANES)
            k_pair = jnp.concatenate([k_even[keys, cols], k_odd[keys, cols]], axis=0)
            a = lax.dot_general(q_b[rows, cols], k_pair, _NT, preferred_element_type=jnp.float32)
            out += [a[:, 0:sub], a[:, sub:2 * sub]]
        return out

    def add_level(acc, q_lvl, k_lvl, lvl):
        hit = level == lvl
        for r in range(2):
            for h, a in enumerate(level_products(q_lvl, k_lvl, r, r)):
                acc[r][h] = jnp.where(hit, a, acc[r][h])
        return acc

    acc = [[jnp.zeros((sub, sub), jnp.float32) for _ in range(GLA_HEADS)] for _ in range(2)]
    acc = add_level(acc, qf, kf, -1)
    a_cross = None
    b_end = b
    half = 1
    lvl = 0
    while half < c:
        b_before = pltpu.roll(b_end, half, 0)
        q_lvl = qf * jnp.exp2(jnp.minimum(b - b_before, 0.0))
        k_lvl = kf * jnp.exp2(b_end - b)
        if half < sub:
            acc = add_level(acc, q_lvl, k_lvl, lvl)
        else:
            a_cross = level_products(q_lvl, k_lvl, 1, 0)
        upper = (row & half) != 0
        b_end = jnp.where(upper, b_end, pltpu.roll(b_end, c - half, 0))
        half *= 2
        lvl += 1
    q_in = (qf * jnp.exp2(b)).astype(jnp.bfloat16)
    k_out = (kf * jnp.exp2(b_end - b)).astype(jnp.bfloat16)
    tile_decay = jnp.exp2(b_end[0:1, :])

    for h in range(GLA_HEADS):
        cols = slice((h // 2) * LANES, (h // 2 + 1) * LANES)
        v_h = v_ref[:, h * dv:(h + 1) * dv]
        st = st_ref[h]
        o_inter = lax.dot_general(q_in[:, cols], st[:, cols].astype(jnp.bfloat16), _NT,
                                  preferred_element_type=jnp.float32)
        a_lo = acc[0][h].astype(jnp.bfloat16)
        a_hi = jnp.concatenate([a_cross[h], acc[1][h]], axis=1).astype(jnp.bfloat16)
        o_intra = jnp.concatenate(
            [jnp.dot(a_lo, v_h[0:sub, :], preferred_element_type=jnp.float32),
             jnp.dot(a_hi, v_h, preferred_element_type=jnp.float32)], axis=0)
        o_ref[:, h * dv:(h + 1) * dv] = (o_inter + o_intra).astype(o_ref.dtype)
        v_t = v_h.astype(jnp.float32).T.astype(jnp.bfloat16)
        upd = jnp.dot(v_t, k_out, preferred_element_type=jnp.float32)
        own = (st_lane >= h * dk) & (st_lane < (h + 1) * dk)
        st_ref[h] = st * tile_decay + jnp.where(own, upd, 0.0)


def _gla(pj, la, batch, seq):
    n_tile = seq // GLA_STEP

    def tok(width, col0=0):
        return pl.BlockSpec((GLA_STEP, width), lambda b, t: (b * n_tile + t, col0 // width))

    return pl.pallas_call(
        _gla_kernel,
        grid=(batch, n_tile),
        in_specs=[tok(GLA_K_WIDTH, 3 * MOBA_WIDTH), tok(GLA_K_WIDTH, 3 * MOBA_WIDTH + GLA_K_WIDTH),
                  tok(GLA_V_WIDTH, 3 * MOBA_WIDTH + 2 * GLA_K_WIDTH), tok(GLA_K_WIDTH)],
        out_specs=tok(GLA_V_WIDTH),
        out_shape=jax.ShapeDtypeStruct((batch * seq, GLA_V_WIDTH), jnp.bfloat16),
        scratch_shapes=[pltpu.VMEM((GLA_HEADS, GLA_VALUE_DIM, GLA_K_WIDTH), jnp.float32)],
        compiler_params=pltpu.CompilerParams(
            dimension_semantics=("arbitrary", "arbitrary"), vmem_limit_bytes=VMEM_LIMIT),
        name="gla",
    )(pj, pj, pj, la)


def kernel(x, ffn1_pre_g, ffn1_w_gate, ffn1_w_up, ffn1_w_down, ffn1_post_g, mix_pre_g, w_in,
           gla_w_alpha_up, gla_b_alpha, moba_out_g, gla_out_g, w_out, mix_post_g, ffn2_pre_g,
           ffn2_w_gate, ffn2_w_up, ffn2_w_down, ffn2_post_g):
    batch, seq, d = x.shape
    assert d == D_MODEL and seq % MOBA_BLOCK == 0 and seq % GLA_STEP == 0
    assert (batch * seq) % FFN_TM == 0 and (batch * seq) % MIX_TM == 0
    bf = jnp.bfloat16
    xt = x.reshape(batch * seq, d)
    for l in range(w_in.shape[0]):
        wup = jnp.pad(gla_w_alpha_up[l], ((0, GATE_PAD - GLA_GATE_RANK), (0, 0))).astype(bf)
        xt, pj, kmean, la = _ffn_proj(
            xt, ffn1_pre_g[l][None], ffn1_w_gate[l], ffn1_w_up[l], ffn1_w_down[l], ffn1_post_g[l][None],
            mix_pre_g[l][None], w_in[l].T, wup, gla_b_alpha[l][None])

        kmean = kmean.reshape(batch, seq // MOBA_BLOCK, MOBA_WIDTH)
        o_moba = _moba(pj, kmean, batch, seq)
        o_gla = _gla(pj, la, batch, seq)

        xt = _mix_ffn(xt, o_moba, o_gla, pj, moba_out_g[l][None],
                      gla_out_g[l].reshape(1, GLA_V_WIDTH), w_out[l], mix_post_g[l][None],
                      ffn2_pre_g[l][None], ffn2_w_gate[l], ffn2_w_up[l], ffn2_w_down[l],
                      ffn2_post_g[l][None])
    return xt.reshape(batch, seq, d)
```

```python
import functools

import jax
import jax.numpy as jnp
from jax import lax
from jax.experimental import pallas as pl
from jax.experimental.pallas import tpu as pltpu

D_MODEL = 1024
D_FF = 2816
RMS_EPS = 1e-6

MOBA_WIDTH = 512
MOBA_HEAD_DIM = 64
MOBA_BLOCK = 256
MOBA_TOPK = 3
MOBA_COLS = 2
LOG2_E = 1.4426950408889634
MOBA_Q_SCALE = MOBA_HEAD_DIM ** -0.5 * LOG2_E

GLA_HEADS = 4
GLA_KEY_DIM = 64
GLA_VALUE_DIM = 128
GLA_K_WIDTH = GLA_HEADS * GLA_KEY_DIM
GLA_V_WIDTH = GLA_HEADS * GLA_VALUE_DIM
GLA_GATE_RANK = 16
GLA_GATE_TAU = 16.0
GLA_TILE = 256
GLA_STEP = 1024

LANES = 128
GATE_PAD = LANES
NEG_BIG = -1e30
SUM_ROWS = 16

FFN_TM = 512
FFN_TF = 256
CAST_ROWS = 256
PROJ_MAIN = 3 * MOBA_WIDTH + 2 * GLA_K_WIDTH + GLA_V_WIDTH
PROJ_OUT = PROJ_MAIN + GLA_V_WIDTH
MIX_TM = 1024
MIX_SUB = 256
VMEM_LIMIT = 56 * 1024 * 1024

_NT = (((1,), (1,)), ((), ()))


def _rms(x, g):
    return x * lax.rsqrt(jnp.mean(x * x, axis=-1, keepdims=True) + RMS_EPS) * g


def _silu(x):
    return x / (1.0 + jnp.exp(-x))


def _resident(shape):
    return pl.BlockSpec(shape, lambda *_: (0,) * len(shape), pipeline_mode=pl.Buffered(1))


def _interleave(stage_lists, lag):
    n_slots = max(len(st) + j * lag for j, st in enumerate(stage_lists))
    for k in range(n_slots):
        for j, st in enumerate(stage_lists):
            if 0 <= k - j * lag < len(st):
                st[k - j * lag]()


def _cast_weight(src_hbm, stage_ref, sem, store, n_rows, col0, n_cols, row0=0, chunk_rows=CAST_ROWS):
    n_chunk = n_rows // chunk_rows
    n_slot = stage_ref.shape[0]

    def copy(c):
        slot = c % n_slot
        return pltpu.make_async_copy(
            src_hbm.at[pl.ds(row0 + c * chunk_rows, chunk_rows), pl.ds(col0, n_cols)],
            stage_ref.at[slot, pl.ds(0, chunk_rows), pl.ds(0, n_cols)], sem.at[slot])

    for c in range(min(n_slot, n_chunk)):
        copy(c).start()
    for c in range(n_chunk):
        copy(c).wait()
        store(slice(c * chunk_rows, (c + 1) * chunk_rows),
              stage_ref[c % n_slot, 0:chunk_rows, 0:n_cols])
        if c + n_slot < n_chunk:
            copy(c + n_slot).start()


def _cast_into(src_hbm, dst_ref, stage_ref, sem, col0=0):
    def store(rows, chunk):
        dst_ref[rows, :] = chunk.astype(jnp.bfloat16)

    _cast_weight(src_hbm, stage_ref, sem, store, dst_ref.shape[0], col0, dst_ref.shape[1])


_HBM = pl.BlockSpec(memory_space=pl.ANY)


def _cast_scratch(n_slot):
    return [pltpu.VMEM((n_slot, CAST_ROWS, D_FF), jnp.float32), pltpu.SemaphoreType.DMA((n_slot,))]


def _row(width, rows=FFN_TM):
    return pl.BlockSpec((rows, width), lambda i: (i, 0))


_ROW_PARAMS = dict(
    compiler_params=pltpu.CompilerParams(
        dimension_semantics=("arbitrary",), vmem_limit_bytes=VMEM_LIMIT))


def _ffn_proj_stages(r, x_ref, pre_ref, post_ref, g_ref, wup_ref, ba_ref,
                     x1_ref, pj_ref, kmean_ref, la_ref,
                     h_ref, wg_ref, wu_ref, wd_ref, wmain_ref, wgr_ref, wat_ref):
    rows = slice(r * MOBA_BLOCK, (r + 1) * MOBA_BLOCK)
    st = {}

    def prenorm():
        st["xn"] = _rms(x_ref[rows, :], pre_ref[...]).astype(jnp.bfloat16)

    def up(c):
        def run():
            cols = slice(c * FFN_TF, (c + 1) * FFN_TF)
            g = jnp.dot(st["xn"], wg_ref[:, cols], preferred_element_type=jnp.float32)
            u = jnp.dot(st["xn"], wu_ref[:, cols], preferred_element_type=jnp.float32)
            h_ref[rows, cols] = (_silu(g) * u).astype(jnp.bfloat16)
        return run

    def down():
        st["f"] = jnp.dot(h_ref[rows, :], wd_ref[...], preferred_element_type=jnp.float32)

    def mid():
        x1 = x_ref[rows, :] + 0.5 * _rms(st.pop("f"), post_ref[...])
        x1_ref[rows, :] = x1
        st["h"] = _rms(x1, g_ref[...]).astype(jnp.bfloat16)

    def proj(w_ref, lo, hi):
        return jnp.dot(st["h"], w_ref[:, lo:hi], preferred_element_type=jnp.float32)

    w, kw, vw = MOBA_WIDTH, GLA_K_WIDTH, GLA_V_WIDTH

    def moba_q():
        pj_ref[rows, 0:w] = (proj(wmain_ref, 0, w) * MOBA_Q_SCALE).astype(jnp.bfloat16)

    def moba_k():
        mk = proj(wmain_ref, w, 2 * w)
        pj_ref[rows, w:2 * w] = mk.astype(jnp.bfloat16)
        kmean_ref[r] = jnp.mean(mk, axis=0, keepdims=True)

    def moba_v():
        pj_ref[rows, 2 * w:3 * w] = proj(wmain_ref, 2 * w, 3 * w).astype(jnp.bfloat16)

    def gla_qk():
        pj_ref[rows, 3 * w:3 * w + 2 * kw] = proj(wmain_ref, 3 * w, 3 * w + 2 * kw).astype(jnp.bfloat16)

    def gla_v():
        pj_ref[rows, 3 * w + 2 * kw:PROJ_MAIN] = proj(wmain_ref, 3 * w + 2 * kw, PROJ_MAIN).astype(jnp.bfloat16)

    def gla_gate():
        pj_ref[rows, PROJ_MAIN:PROJ_MAIN + vw] = proj(wgr_ref, 0, vw).astype(jnp.bfloat16)

    def gla_decay():
        ga = lax.dot_general(st["h"], wat_ref[...], _NT, preferred_element_type=jnp.float32)
        z = jnp.dot(ga.astype(jnp.bfloat16), wup_ref[...],
                    preferred_element_type=jnp.float32) + ba_ref[...]
        log_sig = jnp.minimum(z, 0.0) - jnp.log(1.0 + jnp.exp(-jnp.abs(z)))
        la_ref[rows, :] = log_sig * (LOG2_E / GLA_GATE_TAU)

    return ([prenorm] + [up(c) for c in range(D_FF // FFN_TF)] + [down, mid]
            + [moba_q, moba_k, moba_v, gla_qk, gla_v, gla_gate, gla_decay])


def _ffn_proj_kernel(x_ref, pre_ref, wg_hbm, wu_hbm, wd_hbm, post_ref, g_ref, wint_hbm, *rest):
    wg_ref, wu_ref, wd_ref, wmain_ref, wgr_ref, wat_ref, stage_ref, sem = rest[-8:]

    @pl.when(pl.program_id(0) == 0)
    def _():
        _cast_into(wg_hbm, wg_ref, stage_ref, sem)
        _cast_into(wu_hbm, wu_ref, stage_ref, sem)
        _cast_into(wd_hbm, wd_ref, stage_ref, sem)

        def store_main(rows, chunk):
            wmain_ref[:, rows] = chunk.T.astype(jnp.bfloat16)

        _cast_weight(wint_hbm, stage_ref, sem, store_main, PROJ_MAIN, 0, D_MODEL)

        def store_gate(rows, chunk):
            wgr_ref[:, rows] = chunk.T.astype(jnp.bfloat16)

        _cast_weight(wint_hbm, stage_ref, sem, store_gate, GLA_V_WIDTH, 0, D_MODEL,
                     row0=PROJ_MAIN + GLA_GATE_RANK)

        def store_rank(rows, chunk):
            wat_ref[rows, :] = chunk.astype(jnp.bfloat16)

        _cast_weight(wint_hbm, stage_ref, sem, store_rank, GATE_PAD, 0, D_MODEL, row0=PROJ_MAIN,
                     chunk_rows=GATE_PAD)

    refs = (x_ref, pre_ref, post_ref, g_ref) + rest[:-2]
    _interleave([_ffn_proj_stages(r, *refs) for r in range(FFN_TM // MOBA_BLOCK)], lag=4)


def _ffn_proj(x, pre_g, wg, wu, wd, post_g, g, w_in, wup, ba):
    t = x.shape[0]
    nblk = FFN_TM // MOBA_BLOCK
    small = (wup, ba)

    def out(width, dtype=jnp.bfloat16):
        return jax.ShapeDtypeStruct((t, width), dtype)

    return pl.pallas_call(
        _ffn_proj_kernel,
        grid=(t // FFN_TM,),
        in_specs=[_row(D_MODEL), _resident(pre_g.shape), _HBM, _HBM, _HBM, _resident(post_g.shape),
                  _resident(g.shape), _HBM] + [_resident(p.shape) for p in small],
        out_specs=[_row(D_MODEL), _row(PROJ_OUT),
                   pl.BlockSpec((nblk, 1, MOBA_WIDTH), lambda i: (i, 0, 0)), _row(GLA_K_WIDTH)],
        out_shape=[out(D_MODEL, jnp.float32), out(PROJ_OUT),
                   jax.ShapeDtypeStruct((t // MOBA_BLOCK, 1, MOBA_WIDTH), jnp.float32),
                   out(GLA_K_WIDTH, jnp.float32)],
        scratch_shapes=[pltpu.VMEM((FFN_TM, D_FF), jnp.bfloat16),
                        pltpu.VMEM((D_MODEL, D_FF), jnp.bfloat16), pltpu.VMEM((D_MODEL, D_FF), jnp.bfloat16),
                        pltpu.VMEM((D_FF, D_MODEL), jnp.bfloat16),
                        pltpu.VMEM((D_MODEL, PROJ_MAIN), jnp.bfloat16),
                        pltpu.VMEM((D_MODEL, GLA_V_WIDTH), jnp.bfloat16),
                        pltpu.VMEM((GATE_PAD, D_MODEL), jnp.bfloat16)] + _cast_scratch(3),
        name="ffn_proj", **_ROW_PARAMS,
    )(x, pre_g, wg, wu, wd, post_g, g, w_in, *small)


def _mix_ffn_stages(rows, x_ref, om_ref, og_ref, gr_ref, mg_ref, gg_ref, mpost_ref, pre_ref, post_ref,
                    o_ref, h_ref, wo_ref, wg_ref, wu_ref, wd_ref):
    st = {}

    def mix_in():
        om = _rms(om_ref[rows, :].astype(jnp.float32), mg_ref[...]).astype(jnp.bfloat16)
        gate = _silu(gr_ref[rows, :].astype(jnp.float32))
        og = og_ref[rows, :].astype(jnp.float32)
        parts = [om]
        for h in range(GLA_HEADS):
            cols = slice(h * GLA_VALUE_DIM, (h + 1) * GLA_VALUE_DIM)
            parts.append((_rms(og[:, cols], gg_ref[:, cols]) * gate[:, cols]).astype(jnp.bfloat16))
        st["mix_in"] = jnp.concatenate(parts, axis=-1)

    def mix_out():
        st["mix"] = jnp.dot(st.pop("mix_in"), wo_ref[...], preferred_element_type=jnp.float32)

    def norms():
        x2 = x_ref[rows, :] + _rms(st.pop("mix"), mpost_ref[...])
        st["x2"] = x2
        st["xn"] = _rms(x2, pre_ref[...]).astype(jnp.bfloat16)

    def up(c):
        def run():
            cols = slice(c * FFN_TF, (c + 1) * FFN_TF)
            g = jnp.dot(st["xn"], wg_ref[:, cols], preferred_element_type=jnp.float32)
            u = jnp.dot(st["xn"], wu_ref[:, cols], preferred_element_type=jnp.float32)
            h_ref[rows, cols] = (_silu(g) * u).astype(jnp.bfloat16)
        return run

    def down():
        st["f"] = jnp.dot(h_ref[rows, :], wd_ref[...], preferred_element_type=jnp.float32)

    def finish():
        o_ref[rows, :] = st.pop("x2") + 0.5 * _rms(st.pop("f"), post_ref[...])

    return [mix_in, mix_out, norms] + [up(c) for c in range(D_FF // FFN_TF)] + [down, finish]


def _mix_ffn_kernel(x_ref, om_ref, og_ref, gr_ref, mg_ref, gg_ref, wo_hbm, mpost_ref,
                    pre_ref, wg_hbm, wu_hbm, wd_hbm, post_ref, o_ref,
                    h_ref, wo_ref, wg_ref, wu_ref, wd_ref, stage_ref, sem):
    @pl.when(pl.program_id(0) == 0)
    def _():
        _cast_into(wo_hbm, wo_ref, stage_ref, sem)
        _cast_into(wg_hbm, wg_ref, stage_ref, sem)
        _cast_into(wu_hbm, wu_ref, stage_ref, sem)
        _cast_into(wd_hbm, wd_ref, stage_ref, sem)

    refs = (x_ref, om_ref, og_ref, gr_ref, mg_ref, gg_ref, mpost_ref, pre_ref, post_ref,
            o_ref, h_ref, wo_ref, wg_ref, wu_ref, wd_ref)
    _interleave([_mix_ffn_stages(slice(r * MIX_SUB, (r + 1) * MIX_SUB), *refs)
                 for r in range(MIX_TM // MIX_SUB)], lag=4)


def _mix_ffn(x, om, og, pj, mg, gg, wo, mpost_g, pre_g, wg, wu, wd, post_g):
    t = x.shape[0]
    return pl.pallas_call(
        _mix_ffn_kernel,
        grid=(t // MIX_TM,),
        in_specs=[_row(D_MODEL, MIX_TM), _row(MOBA_WIDTH, MIX_TM), _row(GLA_V_WIDTH, MIX_TM),
                  pl.BlockSpec((MIX_TM, GLA_V_WIDTH), lambda i: (i, PROJ_MAIN // GLA_V_WIDTH)),
                  _resident(mg.shape), _resident(gg.shape), _HBM,
                  _resident(mpost_g.shape), _resident(pre_g.shape), _HBM, _HBM, _HBM,
                  _resident(post_g.shape)],
        out_specs=_row(D_MODEL, MIX_TM),
        out_shape=jax.ShapeDtypeStruct((t, D_MODEL), jnp.float32),
        scratch_shapes=[pltpu.VMEM((MIX_TM, D_FF), jnp.bfloat16),
                        pltpu.VMEM((D_MODEL, D_MODEL), jnp.bfloat16),
                        pltpu.VMEM((D_MODEL, D_FF), jnp.bfloat16), pltpu.VMEM((D_MODEL, D_FF), jnp.bfloat16),
                        pltpu.VMEM((D_FF, D_MODEL), jnp.bfloat16)] + _cast_scratch(2),
        name="mix_ffn", **_ROW_PARAMS,
    )(x, om, og, pj, mg, gg, wo, mpost_g, pre_g, wg, wu, wd, post_g)


def _moba_kernel(q_ref, k_ref, v_ref, kmean_ref, o_ref, kaug_ref, *, n_blk):
    blk = MOBA_BLOCK
    dh = MOBA_HEAD_DIM
    seq = n_blk * blk
    n_head = LANES // dh
    wide = n_head * blk

    key_blk = lax.broadcasted_iota(jnp.int32, (seq, LANES), 0) // blk
    hot_col = lax.broadcasted_iota(jnp.int32, (seq, LANES), 1)
    one_hot = jnp.where(key_blk == hot_col, 1.0, 0.0).astype(jnp.bfloat16)
    feat = lax.broadcasted_iota(jnp.int32, (LANES, blk), 0)
    head_rows = [(feat >= h * dh) & (feat < (h + 1) * dh) for h in range(n_head)]
    blk_row = lax.broadcasted_iota(jnp.int32, (n_blk, wide), 0)
    key_i = lax.broadcasted_iota(jnp.int32, (blk, wide), 0)
    qry_i = lax.broadcasted_iota(jnp.int32, (blk, wide), 1) % blk
    causal = key_i <= qry_i
    bias_pad = jnp.zeros((LANES - n_blk, wide), jnp.float32)
    ones = jnp.ones((SUM_ROWS, blk), jnp.float32)

    def column(c):
        lanes = slice(c * LANES, (c + 1) * LANES)
        kmean = kmean_ref[0][:, lanes]
        km_hi = kmean.astype(jnp.bfloat16)
        km_lo = (kmean - km_hi.astype(jnp.float32)).astype(jnp.bfloat16)
        kaug_ref[c, :, 0:LANES] = k_ref[:, lanes]
        kaug_ref[c, :, LANES:2 * LANES] = one_hot
        v_t = [None] * n_blk

        def value_block(j):
            if v_t[j] is None:
                v_j = v_ref[j * blk:(j + 1) * blk, lanes].astype(jnp.float32).T
                v_t[j] = jnp.concatenate([v_j, ones], axis=0).astype(jnp.bfloat16)
            return v_t[j]

        def section(i):
            st = {}
            n_keys = (i + 1) * blk

            def prep():
                q_i = q_ref[i * blk:(i + 1) * blk, lanes].astype(jnp.float32).T
                qi_f = jnp.concatenate([jnp.where(rows, q_i, 0.0) for rows in head_rows], axis=1)
                qi_t = qi_f.astype(jnp.bfloat16)
                if i > MOBA_TOPK:
                    gate = (jnp.dot(km_hi, qi_t, preferred_element_type=jnp.float32)
                            + jnp.dot(km_lo, qi_t, preferred_element_type=jnp.float32))
                    rank = jnp.zeros((n_blk, wide), jnp.float32)
                    for j in range(i):
                        gj = gate[j:j + 1, :]
                        beats = (gj > gate) | ((gj == gate) & (blk_row > j))
                        rank = rank + jnp.where(beats, 1.0, 0.0)
                    keep = (rank < float(MOBA_TOPK)) | (blk_row >= i)
                    bias = jnp.where(keep, 0.0, NEG_BIG)
                    st["q"] = jnp.concatenate([qi_f, bias, bias_pad], axis=0).astype(jnp.bfloat16)
                else:
                    st["q"] = qi_t

            def scores():
                if i > MOBA_TOPK:
                    s = jnp.dot(kaug_ref[c, 0:n_keys, :], st.pop("q"),
                                preferred_element_type=jnp.float32)
                else:
                    s = jnp.dot(k_ref[0:n_keys, lanes], st.pop("q"), preferred_element_type=jnp.float32)
                s_own = jnp.where(causal, s[i * blk:n_keys, :], NEG_BIG)
                sm = s_own if i == 0 else jnp.concatenate([s[0:i * blk, :], s_own], axis=0)
                st["s"] = sm
                st["m"] = jnp.max(sm, axis=0, keepdims=True)

            def probs():
                st["p"] = jnp.exp2(st.pop("s") - st.pop("m")).astype(jnp.bfloat16)

            def values():
                v_i = jnp.concatenate([value_block(j) for j in range(i + 1)], axis=1)
                o_t = jnp.dot(v_i, st.pop("p"), preferred_element_type=jnp.float32)
                o_t = o_t[0:LANES, :] * (1.0 / o_t[LANES:LANES + 1, :])
                o_own = jnp.concatenate([o_t[h * dh:(h + 1) * dh, h * blk:(h + 1) * blk]
                                         for h in range(n_head)], axis=0)
                o_ref[i * blk:(i + 1) * blk, lanes] = o_own.T.astype(o_ref.dtype)

            return [prep, scores, probs, values]

        return [section(j) + section(n_blk - 1 - j) for j in range(n_blk // 2)]

    _interleave([stream for c in range(MOBA_COLS) for stream in column(c)], lag=1)


def _moba(pj, kmean, batch, seq):
    n_blk = seq // MOBA_BLOCK
    width = MOBA_COLS * LANES
    n_col = MOBA_WIDTH // width

    def tok(group):
        return pl.BlockSpec((seq, width), lambda b, c: (b, group * n_col + c))

    return pl.pallas_call(
        functools.partial(_moba_kernel, n_blk=n_blk),
        grid=(batch, n_col),
        in_specs=[tok(0), tok(1), tok(2), pl.BlockSpec((1, n_blk, width), lambda b, c: (b, 0, c))],
        out_specs=tok(0),
        out_shape=jax.ShapeDtypeStruct((batch * seq, MOBA_WIDTH), jnp.bfloat16),
        scratch_shapes=[pltpu.VMEM((MOBA_COLS, seq, 2 * LANES), jnp.bfloat16)],
        compiler_params=pltpu.CompilerParams(
            dimension_semantics=("arbitrary", "arbitrary"), vmem_limit_bytes=VMEM_LIMIT),
        name="moba",
    )(pj, pj, pj, kmean)


def _gla_kernel(q_ref, k_ref, v_ref, la_ref, o_ref, st_ref):
    @pl.when(pl.program_id(1) == 0)
    def _():
        st_ref[...] = jnp.zeros_like(st_ref)

    for r in range(GLA_STEP // GLA_TILE):
        rows = slice(r * GLA_TILE, (r + 1) * GLA_TILE)
        _gla_tile(q_ref[rows, :], k_ref[rows, :], v_ref.at[rows, :], la_ref[rows, :],
                  o_ref.at[rows, :], st_ref)


def _gla_tile(q, k, v_ref, la, o_ref, st_ref):
    c = GLA_TILE
    dk, dv = GLA_KEY_DIM, GLA_VALUE_DIM
    row = lax.broadcasted_iota(jnp.int32, la.shape, 0)
    lane = lax.broadcasted_iota(jnp.int32, la.shape, 1)
    odd_head = (lane & dk) != 0
    st_lane = lax.broadcasted_iota(jnp.int32, (dv, GLA_K_WIDTH), 1)

    b = la
    step = 1
    while step < c:
        b = b + jnp.where(row >= step, pltpu.roll(b, step, 0), 0.0)
        step *= 2

    qf = q.astype(jnp.float32) * (dk ** -0.5)
    kf = k.astype(jnp.float32)

    sub = c // 2
    t_i = lax.broadcasted_iota(jnp.int32, (sub, sub), 0)
    s_i = lax.broadcasted_iota(jnp.int32, (sub, sub), 1)
    level = jnp.where(t_i >= s_i, 31 - lax.clz(t_i ^ s_i), -2)

    def split_heads(x):
        return (jnp.where(odd_head, 0.0, x).astype(jnp.bfloat16),
                jnp.where(odd_head, x, 0.0).astype(jnp.bfloat16))

    def level_products(q_lvl, k_lvl, row_blk, col_blk):
        q_b = q_lvl.astype(jnp.bfloat16)
        k_even, k_odd = split_heads(k_lvl)
        rows = slice(row_blk * sub, (row_blk + 1) * sub)
        keys = slice(col_blk * sub, (col_blk + 1) * sub)
        out = []
        for pair in range(GLA_HEADS // 2):
            cols = slice(pair * LANES, (pair + 1) * LANES)
            k_pair = jnp.concatenate([k_even[keys, cols], k_odd[keys, cols]], axis=0)
            a = lax.dot_general(q_b[rows, cols], k_pair, _NT, preferred_element_type=jnp.float32)
            out += [a[:, 0:sub], a[:, sub:2 * sub]]
        return out

    def add_level(acc, q_lvl, k_lvl, lvl):
        hit = level == lvl
        for r in range(2):
            for h, a in enumerate(level_products(q_lvl, k_lvl, r, r)):
                acc[r][h] = jnp.where(hit, a, acc[r][h])
        return acc

    acc = [[jnp.zeros((sub, sub), jnp.float32) for _ in range(GLA_HEADS)] for _ in range(2)]
    acc = add_level(acc, qf, kf, -1)
    a_cross = None
    b_end = b
    half = 1
    lvl = 0
    while half < c:
        b_before = pltpu.roll(b_end, half, 0)
        q_lvl = qf * jnp.exp2(jnp.minimum(b - b_before, 0.0))
        k_lvl = kf * jnp.exp2(b_end - b)
        if half < sub:
            acc = add_level(acc, q_lvl, k_lvl, lvl)
        else:
            a_cross = level_products(q_lvl, k_lvl, 1, 0)
        upper = (row & half) != 0
        b_end = jnp.where(upper, b_end, pltpu.roll(b_end, c - half, 0))
        half *= 2
        lvl += 1
    q_in = (qf * jnp.exp2(b)).astype(jnp.bfloat16)
    k_out = (kf * jnp.exp2(b_end - b)).astype(jnp.bfloat16)
    tile_decay = jnp.exp2(b_end[0:1, :])

    for h in range(GLA_HEADS):
        cols = slice((h // 2) * LANES, (h // 2 + 1) * LANES)
        v_h = v_ref[:, h * dv:(h + 1) * dv]
        st = st_ref[h]
        o_inter = lax.dot_general(q_in[:, cols], st[:, cols].astype(jnp.bfloat16), _NT,
                                  preferred_element_type=jnp.float32)
        a_lo = acc[0][h].astype(jnp.bfloat16)
        a_hi = jnp.concatenate([a_cross[h], acc[1][h]], axis=1).astype(jnp.bfloat16)
        o_intra = jnp.concatenate(
            [jnp.dot(a_lo, v_h[0:sub, :], preferred_element_type=jnp.float32),
             jnp.dot(a_hi, v_h, preferred_element_type=jnp.float32)], axis=0)
        o_ref[:, h * dv:(h + 1) * dv] = (o_inter + o_intra).astype(o_ref.dtype)
        v_t = v_h.astype(jnp.float32).T.astype(jnp.bfloat16)
        upd = jnp.dot(v_t, k_out, preferred_element_type=jnp.float32)
        own = (st_lane >= h * dk) & (st_lane < (h + 1) * dk)
        st_ref[h] = st * tile_decay + jnp.where(own, upd, 0.0)


def _gla(pj, la, batch, seq):
    n_tile = seq // GLA_STEP

    def tok(width, col0=0):
        return pl.BlockSpec((GLA_STEP, width), lambda b, t: (b * n_tile + t, col0 // width))

    return pl.pallas_call(
        _gla_kernel,
        grid=(batch, n_tile),
        in_specs=[tok(GLA_K_WIDTH, 3 * MOBA_WIDTH), tok(GLA_K_WIDTH, 3 * MOBA_WIDTH + GLA_K_WIDTH),
                  tok(GLA_V_WIDTH, 3 * MOBA_WIDTH + 2 * GLA_K_WIDTH), tok(GLA_K_WIDTH)],
        out_specs=tok(GLA_V_WIDTH),
        out_shape=jax.ShapeDtypeStruct((batch * seq, GLA_V_WIDTH), jnp.bfloat16),
        scratch_shapes=[pltpu.VMEM((GLA_HEADS, GLA_VALUE_DIM, GLA_K_WIDTH), jnp.float32)],
        compiler_params=pltpu.CompilerParams(
            dimension_semantics=("arbitrary", "arbitrary"), vmem_limit_bytes=VMEM_LIMIT),
        name="gla",
    )(pj, pj, pj, la)


def kernel(x, ffn1_pre_g, ffn1_w_gate, ffn1_w_up, ffn1_w_down, ffn1_post_g, mix_pre_g, w_in,
           gla_w_alpha_up, gla_b_alpha, moba_out_g, gla_out_g, w_out, mix_post_g, ffn2_pre_g,
           ffn2_w_gate, ffn2_w_up, ffn2_w_down, ffn2_post_g):
    batch, seq, d = x.shape
    assert d == D_MODEL and seq % MOBA_BLOCK == 0 and seq % GLA_STEP == 0
    assert (batch * seq) % FFN_TM == 0 and (batch * seq) % MIX_TM == 0
    bf = jnp.bfloat16
    xt = x.reshape(batch * seq, d)
    for l in range(w_in.shape[0]):
        wup = jnp.pad(gla_w_alpha_up[l], ((0, GATE_PAD - GLA_GATE_RANK), (0, 0))).astype(bf)
        xt, pj, kmean, la = _ffn_proj(
            xt, ffn1_pre_g[l][None], ffn1_w_gate[l], ffn1_w_up[l], ffn1_w_down[l], ffn1_post_g[l][None],
            mix_pre_g[l][None], w_in[l].T, wup, gla_b_alpha[l][None])

        kmean = kmean.reshape(batch, seq // MOBA_BLOCK, MOBA_WIDTH)
        o_moba = _moba(pj, kmean, batch, seq)
        o_gla = _gla(pj, la, batch, seq)

        xt = _mix_ffn(xt, o_moba, o_gla, pj, moba_out_g[l][None],
                      gla_out_g[l].reshape(1, GLA_V_WIDTH), w_out[l], mix_post_g[l][None],
                      ffn2_pre_g[l][None], ffn2_w_gate[l], ffn2_w_up[l], ffn2_w_down[l],
                      ffn2_post_g[l][None])
    return xt.reshape(batch, seq, d)
```

```python
import functools

import jax
import jax.numpy as jnp
from jax import lax
from jax.experimental import pallas as pl
from jax.experimental.pallas import tpu as pltpu

D_MODEL = 1024
D_FF = 2816
RMS_EPS = 1e-6

MOBA_WIDTH = 512
MOBA_HEAD_DIM = 64
MOBA_BLOCK = 256
MOBA_TOPK = 3
MOBA_COLS = 2
LOG2_E = 1.4426950408889634
MOBA_Q_SCALE = MOBA_HEAD_DIM ** -0.5 * LOG2_E

GLA_HEADS = 4
GLA_KEY_DIM = 64
GLA_VALUE_DIM = 128
GLA_K_WIDTH = GLA_HEADS * GLA_KEY_DIM
GLA_V_WIDTH = GLA_HEADS * GLA_VALUE_DIM
GLA_GATE_RANK = 16
GLA_GATE_TAU = 16.0
GLA_TILE = 256
GLA_STEP = 1024

LANES = 128
GATE_PAD = LANES
NEG_BIG = -1e30
SUM_ROWS = 16

FFN_TM = 512
FFN_TF = 256
CAST_ROWS = 256
PROJ_MAIN = 3 * MOBA_WIDTH + 2 * GLA_K_WIDTH + GLA_V_WIDTH
PROJ_OUT = PROJ_MAIN + GLA_V_WIDTH
MIX_TM = 1024
MIX_SUB = 256
VMEM_LIMIT = 56 * 1024 * 1024

_NT = (((1,), (1,)), ((), ()))


def _rms(x, g):
    return x * lax.rsqrt(jnp.mean(x * x, axis=-1, keepdims=True) + RMS_EPS) * g


def _silu(x):
    return x / (1.0 + jnp.exp(-x))


def _resident(shape):
    return pl.BlockSpec(shape, lambda *_: (0,) * len(shape), pipeline_mode=pl.Buffered(1))


def _interleave(stage_lists, lag):
    n_slots = max(len(st) + j * lag for j, st in enumerate(stage_lists))
    for k in range(n_slots):
        for j, st in enumerate(stage_lists):
            if 0 <= k - j * lag < len(st):
                st[k - j * lag]()


def _cast_jobs(jobs, stage_ref, sem):
    n_slot = stage_ref.shape[0]
    chunks = [(src, store, row0, k, n_cols, rows)
              for src, store, n_rows, row0, n_cols, rows in jobs for k in range(n_rows // rows)]

    def copy(c):
        src, _, row0, k, n_cols, rows = chunks[c]
        slot = c % n_slot
        return pltpu.make_async_copy(
            src.at[pl.ds(row0 + k * rows, rows), pl.ds(0, n_cols)],
            stage_ref.at[slot, pl.ds(0, rows), pl.ds(0, n_cols)], sem.at[slot])

    def prime():
        for c in range(min(n_slot, len(chunks))):
            copy(c).start()

    def stage(c):
        def run():
            _, store, _, k, n_cols, rows = chunks[c]
            copy(c).wait()
            store(slice(k * rows, (k + 1) * rows), stage_ref[c % n_slot, 0:rows, 0:n_cols])
            if c + n_slot < len(chunks):
                copy(c + n_slot).start()
        return run

    return prime, [stage(c) for c in range(len(chunks))]


def _cast_job(src_hbm, dst_ref):
    def store(rows, chunk):
        dst_ref[rows, :] = chunk.astype(jnp.bfloat16)

    return (src_hbm, store, dst_ref.shape[0], 0, dst_ref.shape[1], CAST_ROWS)


def _cast_now(jobs, stage_ref, sem):
    prime, stages = _cast_jobs(jobs, stage_ref, sem)
    prime()
    for stage in stages:
        stage()


_HBM = pl.BlockSpec(memory_space=pl.ANY)


def _cast_scratch(n_slot):
    return [pltpu.VMEM((n_slot, CAST_ROWS, D_FF), jnp.float32), pltpu.SemaphoreType.DMA((n_slot,))]


def _row(width, rows=FFN_TM):
    return pl.BlockSpec((rows, width), lambda i: (i, 0))


_ROW_PARAMS = dict(
    compiler_params=pltpu.CompilerParams(
        dimension_semantics=("arbitrary",), vmem_limit_bytes=VMEM_LIMIT))


def _ffn_proj_stages(r, x_ref, pre_ref, post_ref, g_ref, wup_ref, ba_ref,
                     x1_ref, pj_ref, kmean_ref, la_ref,
                     h_ref, wg_ref, wu_ref, wd_ref, wmain_ref, wgr_ref, wat_ref):
    rows = slice(r * MOBA_BLOCK, (r + 1) * MOBA_BLOCK)
    st = {}

    def prenorm():
        st["xn"] = _rms(x_ref[rows, :], pre_ref[...]).astype(jnp.bfloat16)

    def up(c):
        def run():
            cols = slice(c * FFN_TF, (c + 1) * FFN_TF)
            g = jnp.dot(st["xn"], wg_ref[:, cols], preferred_element_type=jnp.float32)
            u = jnp.dot(st["xn"], wu_ref[:, cols], preferred_element_type=jnp.float32)
            h_ref[rows, cols] = (_silu(g) * u).astype(jnp.bfloat16)
        return run

    def down():
        st["f"] = jnp.dot(h_ref[rows, :], wd_ref[...], preferred_element_type=jnp.float32)

    def mid():
        x1 = x_ref[rows, :] + 0.5 * _rms(st.pop("f"), post_ref[...])
        x1_ref[rows, :] = x1
        st["h"] = _rms(x1, g_ref[...]).astype(jnp.bfloat16)

    def proj(w_ref, lo, hi):
        return jnp.dot(st["h"], w_ref[:, lo:hi], preferred_element_type=jnp.float32)

    w, kw, vw = MOBA_WIDTH, GLA_K_WIDTH, GLA_V_WIDTH

    def moba_q():
        pj_ref[rows, 0:w] = (proj(wmain_ref, 0, w) * MOBA_Q_SCALE).astype(jnp.bfloat16)

    def moba_k():
        mk = proj(wmain_ref, w, 2 * w)
        pj_ref[rows, w:2 * w] = mk.astype(jnp.bfloat16)
        kmean_ref[r] = jnp.mean(mk, axis=0, keepdims=True)

    def moba_v():
        pj_ref[rows, 2 * w:3 * w] = proj(wmain_ref, 2 * w, 3 * w).astype(jnp.bfloat16)

    def gla_qk():
        pj_ref[rows, 3 * w:3 * w + 2 * kw] = proj(wmain_ref, 3 * w, 3 * w + 2 * kw).astype(jnp.bfloat16)

    def gla_v():
        pj_ref[rows, 3 * w + 2 * kw:PROJ_MAIN] = proj(wmain_ref, 3 * w + 2 * kw, PROJ_MAIN).astype(jnp.bfloat16)

    def gla_gate():
        pj_ref[rows, PROJ_MAIN:PROJ_MAIN + vw] = proj(wgr_ref, 0, vw).astype(jnp.bfloat16)

    def gla_decay():
        ga = lax.dot_general(st["h"], wat_ref[...], _NT, preferred_element_type=jnp.float32)
        z = jnp.dot(ga.astype(jnp.bfloat16), wup_ref[...],
                    preferred_element_type=jnp.float32) + ba_ref[...]
        log_sig = jnp.minimum(z, 0.0) - jnp.log(1.0 + jnp.exp(-jnp.abs(z)))
        la_ref[rows, :] = log_sig * (LOG2_E / GLA_GATE_TAU)

    return ([prenorm] + [up(c) for c in range(D_FF // FFN_TF)] + [down, mid]
            + [moba_q, moba_k, moba_v, gla_qk, gla_v, gla_gate, gla_decay])


def _ffn_proj_kernel(x_ref, pre_ref, wg_hbm, wu_hbm, wd_hbm, post_ref, g_ref, wint_hbm, *rest):
    wg_ref, wu_ref, wd_ref, wmain_ref, wgr_ref, wat_ref, stage_ref, sem = rest[-8:]
    refs = (x_ref, pre_ref, post_ref, g_ref) + rest[:-2]
    n_up = D_FF // FFN_TF

    def sub_tiles():
        return [_ffn_proj_stages(r, *refs) for r in range(FFN_TM // MOBA_BLOCK)]

    @pl.when(pl.program_id(0) == 0)
    def _():
        _cast_now([_cast_job(wg_hbm, wg_ref), _cast_job(wu_hbm, wu_ref)], stage_ref, sem)

        def store_main(rows, chunk):
            wmain_ref[:, rows] = chunk.T.astype(jnp.bfloat16)

        def store_gate(rows, chunk):
            wgr_ref[:, rows] = chunk.T.astype(jnp.bfloat16)

        def store_rank(rows, chunk):
            wat_ref[rows, :] = chunk.astype(jnp.bfloat16)

        prime, casts = _cast_jobs(
            [_cast_job(wd_hbm, wd_ref),
             (wint_hbm, store_main, PROJ_MAIN, 0, D_MODEL, CAST_ROWS),
             (wint_hbm, store_gate, GLA_V_WIDTH, PROJ_MAIN + GLA_GATE_RANK, D_MODEL, CAST_ROWS),
             (wint_hbm, store_rank, GATE_PAD, PROJ_MAIN, D_MODEL, GATE_PAD)], stage_ref, sem)
        prime()
        first, second = sub_tiles()
        per_up = -(-len(casts) // n_up)
        merged = [first[0]]
        for c in range(n_up):
            merged += [first[1 + c]] + casts[c * per_up:(c + 1) * per_up]
        _interleave([merged + first[1 + n_up:], second], lag=len(merged) - n_up)

    @pl.when(pl.program_id(0) > 0)
    def _():
        _interleave(sub_tiles(), lag=4)


def _ffn_proj(x, pre_g, wg, wu, wd, post_g, g, w_in, wup, ba):
    t = x.shape[0]
    nblk = FFN_TM // MOBA_BLOCK
    small = (wup, ba)

    def out(width, dtype=jnp.bfloat16):
        return jax.ShapeDtypeStruct((t, width), dtype)

    return pl.pallas_call(
        _ffn_proj_kernel,
        grid=(t // FFN_TM,),
        in_specs=[_row(D_MODEL), _resident(pre_g.shape), _HBM, _HBM, _HBM, _resident(post_g.shape),
                  _resident(g.shape), _HBM] + [_resident(p.shape) for p in small],
        out_specs=[_row(D_MODEL), _row(PROJ_OUT),
                   pl.BlockSpec((nblk, 1, MOBA_WIDTH), lambda i: (i, 0, 0)), _row(GLA_K_WIDTH)],
        out_shape=[out(D_MODEL, jnp.float32), out(PROJ_OUT),
                   jax.ShapeDtypeStruct((t // MOBA_BLOCK, 1, MOBA_WIDTH), jnp.float32),
                   out(GLA_K_WIDTH, jnp.float32)],
        scratch_shapes=[pltpu.VMEM((FFN_TM, D_FF), jnp.bfloat16),
                        pltpu.VMEM((D_MODEL, D_FF), jnp.bfloat16), pltpu.VMEM((D_MODEL, D_FF), jnp.bfloat16),
                        pltpu.VMEM((D_FF, D_MODEL), jnp.bfloat16),
                        pltpu.VMEM((D_MODEL, PROJ_MAIN), jnp.bfloat16),
                        pltpu.VMEM((D_MODEL, GLA_V_WIDTH), jnp.bfloat16),
                        pltpu.VMEM((GATE_PAD, D_MODEL), jnp.bfloat16)] + _cast_scratch(3),
        name="ffn_proj", **_ROW_PARAMS,
    )(x, pre_g, wg, wu, wd, post_g, g, w_in, *small)


def _mix_ffn_stages(rows, x_ref, om_ref, og_ref, gr_ref, mg_ref, gg_ref, mpost_ref, pre_ref, post_ref,
                    o_ref, h_ref, wo_ref, wg_ref, wu_ref, wd_ref):
    st = {}

    def mix_in():
        om = _rms(om_ref[rows, :].astype(jnp.float32), mg_ref[...]).astype(jnp.bfloat16)
        gate = _silu(gr_ref[rows, :].astype(jnp.float32))
        og = og_ref[rows, :].astype(jnp.float32)
        parts = [om]
        for h in range(GLA_HEADS):
            cols = slice(h * GLA_VALUE_DIM, (h + 1) * GLA_VALUE_DIM)
            parts.append((_rms(og[:, cols], gg_ref[:, cols]) * gate[:, cols]).astype(jnp.bfloat16))
        st["mix_in"] = jnp.concatenate(parts, axis=-1)

    def mix_out():
        st["mix"] = jnp.dot(st.pop("mix_in"), wo_ref[...], preferred_element_type=jnp.float32)

    def norms():
        x2 = x_ref[rows, :] + _rms(st.pop("mix"), mpost_ref[...])
        st["x2"] = x2
        st["xn"] = _rms(x2, pre_ref[...]).astype(jnp.bfloat16)

    def up(c):
        def run():
            cols = slice(c * FFN_TF, (c + 1) * FFN_TF)
            g = jnp.dot(st["xn"], wg_ref[:, cols], preferred_element_type=jnp.float32)
            u = jnp.dot(st["xn"], wu_ref[:, cols], preferred_element_type=jnp.float32)
            h_ref[rows, cols] = (_silu(g) * u).astype(jnp.bfloat16)
        return run

    def down():
        st["f"] = jnp.dot(h_ref[rows, :], wd_ref[...], preferred_element_type=jnp.float32)

    def finish():
        o_ref[rows, :] = st.pop("x2") + 0.5 * _rms(st.pop("f"), post_ref[...])

    return [mix_in, mix_out, norms] + [up(c) for c in range(D_FF // FFN_TF)] + [down, finish]


def _mix_ffn_kernel(x_ref, om_ref, og_ref, gr_ref, mg_ref, gg_ref, wo_hbm, mpost_ref,
                    pre_ref, wg_hbm, wu_hbm, wd_hbm, post_ref, o_ref,
                    h_ref, wo_ref, wg_ref, wu_ref, wd_ref, stage_ref, sem):
    @pl.when(pl.program_id(0) == 0)
    def _():
        _cast_now([_cast_job(wo_hbm, wo_ref), _cast_job(wg_hbm, wg_ref), _cast_job(wu_hbm, wu_ref),
                   _cast_job(wd_hbm, wd_ref)], stage_ref, sem)

    refs = (x_ref, om_ref, og_ref, gr_ref, mg_ref, gg_ref, mpost_ref, pre_ref, post_ref,
            o_ref, h_ref, wo_ref, wg_ref, wu_ref, wd_ref)
    _interleave([_mix_ffn_stages(slice(r * MIX_SUB, (r + 1) * MIX_SUB), *refs)
                 for r in range(MIX_TM // MIX_SUB)], lag=4)


def _mix_ffn(x, om, og, pj, mg, gg, wo, mpost_g, pre_g, wg, wu, wd, post_g):
    t = x.shape[0]
    return pl.pallas_call(
        _mix_ffn_kernel,
        grid=(t // MIX_TM,),
        in_specs=[_row(D_MODEL, MIX_TM), _row(MOBA_WIDTH, MIX_TM), _row(GLA_V_WIDTH, MIX_TM),
                  pl.BlockSpec((MIX_TM, GLA_V_WIDTH), lambda i: (i, PROJ_MAIN // GLA_V_WIDTH)),
                  _resident(mg.shape), _resident(gg.shape), _HBM,
                  _resident(mpost_g.shape), _resident(pre_g.shape), _HBM, _HBM, _HBM,
                  _resident(post_g.shape)],
        out_specs=_row(D_MODEL, MIX_TM),
        out_shape=jax.ShapeDtypeStruct((t, D_MODEL), jnp.float32),
        scratch_shapes=[pltpu.VMEM((MIX_TM, D_FF), jnp.bfloat16),
                        pltpu.VMEM((D_MODEL, D_MODEL), jnp.bfloat16),
                        pltpu.VMEM((D_MODEL, D_FF), jnp.bfloat16), pltpu.VMEM((D_MODEL, D_FF), jnp.bfloat16),
                        pltpu.VMEM((D_FF, D_MODEL), jnp.bfloat16)] + _cast_scratch(2),
        name="mix_ffn", **_ROW_PARAMS,
    )(x, om, og, pj, mg, gg, wo, mpost_g, pre_g, wg, wu, wd, post_g)


def _moba_kernel(q_ref, k_ref, v_ref, kmean_ref, o_ref, kaug_ref, *, n_blk):
    blk = MOBA_BLOCK
    dh = MOBA_HEAD_DIM
    seq = n_blk * blk
    n_head = LANES // dh
    wide = n_head * blk

    key_blk = lax.broadcasted_iota(jnp.int32, (seq, LANES), 0) // blk
    hot_col = lax.broadcasted_iota(jnp.int32, (seq, LANES), 1)
    one_hot = jnp.where(key_blk == hot_col, 1.0, 0.0).astype(jnp.bfloat16)
    feat = lax.broadcasted_iota(jnp.int32, (LANES, blk), 0)
    head_rows = [(feat >= h * dh) & (feat < (h + 1) * dh) for h in range(n_head)]
    blk_row = lax.broadcasted_iota(jnp.int32, (n_blk, wide), 0)
    key_i = lax.broadcasted_iota(jnp.int32, (blk, wide), 0)
    qry_i = lax.broadcasted_iota(jnp.int32, (blk, wide), 1) % blk
    causal = key_i <= qry_i
    bias_pad = jnp.zeros((LANES - n_blk, wide), jnp.float32)
    ones = jnp.ones((SUM_ROWS, blk), jnp.float32)

    def column(c):
        lanes = slice(c * LANES, (c + 1) * LANES)
        kmean = kmean_ref[0][:, lanes]
        km_hi = kmean.astype(jnp.bfloat16)
        km_lo = (kmean - km_hi.astype(jnp.float32)).astype(jnp.bfloat16)
        kaug_ref[c, :, 0:LANES] = k_ref[:, lanes]
        kaug_ref[c, :, LANES:2 * LANES] = one_hot
        v_t = [None] * n_blk

        def value_block(j):
            if v_t[j] is None:
                v_j = v_ref[j * blk:(j + 1) * blk, lanes].astype(jnp.float32).T
                v_t[j] = jnp.concatenate([v_j, ones], axis=0).astype(jnp.bfloat16)
            return v_t[j]

        def section(i):
            st = {}
            n_keys = (i + 1) * blk

            def prep():
                q_i = q_ref[i * blk:(i + 1) * blk, lanes].astype(jnp.float32).T
                qi_f = jnp.concatenate([jnp.where(rows, q_i, 0.0) for rows in head_rows], axis=1)
                qi_t = qi_f.astype(jnp.bfloat16)
                if i > MOBA_TOPK:
                    gate = (jnp.dot(km_hi, qi_t, preferred_element_type=jnp.float32)
                            + jnp.dot(km_lo, qi_t, preferred_element_type=jnp.float32))
                    rank = jnp.zeros((n_blk, wide), jnp.float32)
                    for j in range(i):
                        gj = gate[j:j + 1, :]
                        beats = (gj > gate) | ((gj == gate) & (blk_row > j))
                        rank = rank + jnp.where(beats, 1.0, 0.0)
                    keep = (rank < float(MOBA_TOPK)) | (blk_row >= i)
                    bias = jnp.where(keep, 0.0, NEG_BIG)
                    st["q"] = jnp.concatenate([qi_f, bias, bias_pad], axis=0).astype(jnp.bfloat16)
                else:
                    st["q"] = qi_t

            def scores():
                if i > MOBA_TOPK:
                    s = jnp.dot(kaug_ref[c, 0:n_keys, :], st.pop("q"),
                                preferred_element_type=jnp.float32)
                else:
                    s = jnp.dot(k_ref[0:n_keys, lanes], st.pop("q"), preferred_element_type=jnp.float32)
                s_own = jnp.where(causal, s[i * blk:n_keys, :], NEG_BIG)
                sm = s_own if i == 0 else jnp.concatenate([s[0:i * blk, :], s_own], axis=0)
                st["s"] = sm
                st["m"] = jnp.max(sm, axis=0, keepdims=True)

            def probs():
                st["p"] = jnp.exp2(st.pop("s") - st.pop("m")).astype(jnp.bfloat16)

            def values():
                v_i = jnp.concatenate([value_block(j) for j in range(i + 1)], axis=1)
                o_t = jnp.dot(v_i, st.pop("p"), preferred_element_type=jnp.float32)
                o_t = o_t[0:LANES, :] * (1.0 / o_t[LANES:LANES + 1, :])
                o_own = jnp.concatenate([o_t[h * dh:(h + 1) * dh, h * blk:(h + 1) * blk]
                                         for h in range(n_head)], axis=0)
                o_ref[i * blk:(i + 1) * blk, lanes] = o_own.T.astype(o_ref.dtype)

            return [prep, scores, probs, values]

        return [section(j) + section(n_blk - 1 - j) for j in range(n_blk // 2)]

    _interleave([stream for c in range(MOBA_COLS) for stream in column(c)], lag=1)


def _moba(pj, kmean, batch, seq):
    n_blk = seq // MOBA_BLOCK
    width = MOBA_COLS * LANES
    n_col = MOBA_WIDTH // width

    def tok(group):
        return pl.BlockSpec((seq, width), lambda b, c: (b, group * n_col + c))

    return pl.pallas_call(
        functools.partial(_moba_kernel, n_blk=n_blk),
        grid=(batch, n_col),
        in_specs=[tok(0), tok(1), tok(2), pl.BlockSpec((1, n_blk, width), lambda b, c: (b, 0, c))],
        out_specs=tok(0),
        out_shape=jax.ShapeDtypeStruct((batch * seq, MOBA_WIDTH), jnp.bfloat16),
        scratch_shapes=[pltpu.VMEM((MOBA_COLS, seq, 2 * LANES), jnp.bfloat16)],
        compiler_params=pltpu.CompilerParams(
            dimension_semantics=("arbitrary", "arbitrary"), vmem_limit_bytes=VMEM_LIMIT),
        name="moba",
    )(pj, pj, pj, kmean)


def _gla_kernel(q_ref, k_ref, v_ref, la_ref, o_ref, st_ref):
    @pl.when(pl.program_id(1) == 0)
    def _():
        st_ref[...] = jnp.zeros_like(st_ref)

    for r in range(GLA_STEP // GLA_TILE):
        rows = slice(r * GLA_TILE, (r + 1) * GLA_TILE)
        _gla_tile(q_ref[rows, :], k_ref[rows, :], v_ref.at[rows, :], la_ref[rows, :],
                  o_ref.at[rows, :], st_ref)


def _gla_tile(q, k, v_ref, la, o_ref, st_ref):
    c = GLA_TILE
    dk, dv = GLA_KEY_DIM, GLA_VALUE_DIM
    row = lax.broadcasted_iota(jnp.int32, la.shape, 0)
    lane = lax.broadcasted_iota(jnp.int32, la.shape, 1)
    odd_head = (lane & dk) != 0
    st_lane = lax.broadcasted_iota(jnp.int32, (dv, GLA_K_WIDTH), 1)

    b = la
    step = 1
    while step < c:
        b = b + jnp.where(row >= step, pltpu.roll(b, step, 0), 0.0)
        step *= 2

    qf = q.astype(jnp.float32) * (dk ** -0.5)
    kf = k.astype(jnp.float32)

    sub = c // 2
    t_i = lax.broadcasted_iota(jnp.int32, (sub, sub), 0)
    s_i = lax.broadcasted_iota(jnp.int32, (sub, sub), 1)
    level = jnp.where(t_i >= s_i, 31 - lax.clz(t_i ^ s_i), -2)

    def split_heads(x):
        return (jnp.where(odd_head, 0.0, x).astype(jnp.bfloat16),
                jnp.where(odd_head, x, 0.0).astype(jnp.bfloat16))

    def level_products(q_lvl, k_lvl, row_blk, col_blk):
        q_b = q_lvl.astype(jnp.bfloat16)
        k_even, k_odd = split_heads(k_lvl)
        rows = slice(row_blk * sub, (row_blk + 1) * sub)
        keys = slice(col_blk * sub, (col_blk + 1) * sub)
        out = []
        for pair in range(GLA_HEADS // 2):
            cols = slice(pair * LANES, (pair + 1) * LANES)
            k_pair = jnp.concatenate([k_even[keys, cols], k_odd[keys, cols]], axis=0)
            a = lax.dot_general(q_b[rows, cols], k_pair, _NT, preferred_element_type=jnp.float32)
            out += [a[:, 0:sub], a[:, sub:2 * sub]]
        return out

    def add_level(acc, q_lvl, k_lvl, lvl):
        hit = level == lvl
        for r in range(2):
            for h, a in enumerate(level_products(q_lvl, k_lvl, r, r)):
                acc[r][h] = jnp.where(hit, a, acc[r][h])
        return acc

    acc = [[jnp.zeros((sub, sub), jnp.float32) for _ in range(GLA_HEADS)] for _ in range(2)]
    acc = add_level(acc, qf, kf, -1)
    a_cross = None
    b_end = b
    half = 1
    lvl = 0
    while half < c:
        b_before = pltpu.roll(b_end, half, 0)
        q_lvl = qf * jnp.exp2(jnp.minimum(b - b_before, 0.0))
        k_lvl = kf * jnp.exp2(b_end - b)
        if half < sub:
            acc = add_level(acc, q_lvl, k_lvl, lvl)
        else:
            a_cross = level_products(q_lvl, k_lvl, 1, 0)
        upper = (row & half) != 0
        b_end = jnp.where(upper, b_end, pltpu.roll(b_end, c - half, 0))
        half *= 2
        lvl += 1
    q_in = (qf * jnp.exp2(b)).astype(jnp.bfloat16)
    k_out = (kf * jnp.exp2(b_end - b)).astype(jnp.bfloat16)
    tile_decay = jnp.exp2(b_end[0:1, :])

    for h in range(GLA_HEADS):
        cols = slice((h // 2) * LANES, (h // 2 + 1) * LANES)
        v_h = v_ref[:, h * dv:(h + 1) * dv]
        st = st_ref[h]
        o_inter = lax.dot_general(q_in[:, cols], st[:, cols].astype(jnp.bfloat16), _NT,
                                  preferred_element_type=jnp.float32)
        a_lo = acc[0][h].astype(jnp.bfloat16)
        a_hi = jnp.concatenate([a_cross[h], acc[1][h]], axis=1).astype(jnp.bfloat16)
        o_intra = jnp.concatenate(
            [jnp.dot(a_lo, v_h[0:sub, :], preferred_element_type=jnp.float32),
             jnp.dot(a_hi, v_h, preferred_element_type=jnp.float32)], axis=0)
        o_ref[:, h * dv:(h + 1) * dv] = (o_inter + o_intra).astype(o_ref.dtype)
        v_t = v_h.astype(jnp.float32).T.astype(jnp.bfloat16)
        upd = jnp.dot(v_t, k_out, preferred_element_type=jnp.float32)
        own = (st_lane >= h * dk) & (st_lane < (h + 1) * dk)
        st_ref[h] = st * tile_decay + jnp.where(own, upd, 0.0)


def _gla(pj, la, batch, seq):
    n_tile = seq // GLA_STEP

    def tok(width, col0=0):
        return pl.BlockSpec((GLA_STEP, width), lambda b, t: (b * n_tile + t, col0 // width))

    return pl.pallas_call(
        _gla_kernel,
        grid=(batch, n_tile),
        in_specs=[tok(GLA_K_WIDTH, 3 * MOBA_WIDTH), tok(GLA_K_WIDTH, 3 * MOBA_WIDTH + GLA_K_WIDTH),
                  tok(GLA_V_WIDTH, 3 * MOBA_WIDTH + 2 * GLA_K_WIDTH), tok(GLA_K_WIDTH)],
        out_specs=tok(GLA_V_WIDTH),
        out_shape=jax.ShapeDtypeStruct((batch * seq, GLA_V_WIDTH), jnp.bfloat16),
        scratch_shapes=[pltpu.VMEM((GLA_HEADS, GLA_VALUE_DIM, GLA_K_WIDTH), jnp.float32)],
        compiler_params=pltpu.CompilerParams(
            dimension_semantics=("arbitrary", "arbitrary"), vmem_limit_bytes=VMEM_LIMIT),
        name="gla",
    )(pj, pj, pj, la)


def kernel(x, ffn1_pre_g, ffn1_w_gate, ffn1_w_up, ffn1_w_down, ffn1_post_g, mix_pre_g, w_in,
           gla_w_alpha_up, gla_b_alpha, moba_out_g, gla_out_g, w_out, mix_post_g, ffn2_pre_g,
           ffn2_w_gate, ffn2_w_up, ffn2_w_down, ffn2_post_g):
    batch, seq, d = x.shape
    assert d == D_MODEL and seq % MOBA_BLOCK == 0 and seq % GLA_STEP == 0
    assert (batch * seq) % FFN_TM == 0 and (batch * seq) % MIX_TM == 0
    bf = jnp.bfloat16
    xt = x.reshape(batch * seq, d)
    for l in range(w_in.shape[0]):
        wup = jnp.pad(gla_w_alpha_up[l], ((0, GATE_PAD - GLA_GATE_RANK), (0, 0))).astype(bf)
        xt, pj, kmean, la = _ffn_proj(
            xt, ffn1_pre_g[l][None], ffn1_w_gate[l], ffn1_w_up[l], ffn1_w_down[l], ffn1_post_g[l][None],
            mix_pre_g[l][None], w_in[l].T, wup, gla_b_alpha[l][None])

        kmean = kmean.reshape(batch, seq // MOBA_BLOCK, MOBA_WIDTH)
        o_moba = _moba(pj, kmean, batch, seq)
        o_gla = _gla(pj, la, batch, seq)

        xt = _mix_ffn(xt, o_moba, o_gla, pj, moba_out_g[l][None],
                      gla_out_g[l].reshape(1, GLA_V_WIDTH), w_out[l], mix_post_g[l][None],
                      ffn2_pre_g[l][None], ffn2_w_gate[l], ffn2_w_up[l], ffn2_w_down[l],
                      ffn2_post_g[l][None])
    return xt.reshape(batch, seq, d)
```

```python
import functools

import jax
import jax.numpy as jnp
from jax import lax
from jax.experimental import pallas as pl
from jax.experimental.pallas import tpu as pltpu

D_MODEL = 1024
D_FF = 2816
RMS_EPS = 1e-6

MOBA_WIDTH = 512
MOBA_HEAD_DIM = 64
MOBA_BLOCK = 256
MOBA_TOPK = 3
MOBA_COLS = 2
LOG2_E = 1.4426950408889634
MOBA_Q_SCALE = MOBA_HEAD_DIM ** -0.5 * LOG2_E

GLA_HEADS = 4
GLA_KEY_DIM = 64
GLA_VALUE_DIM = 128
GLA_K_WIDTH = GLA_HEADS * GLA_KEY_DIM
GLA_V_WIDTH = GLA_HEADS * GLA_VALUE_DIM
GLA_GATE_RANK = 16
GLA_GATE_TAU = 16.0
GLA_TILE = 256
GLA_STEP = 1024

LANES = 128
BF16_SUBLANES = 16
GATE_PAD = LANES
NEG_BIG = -1e30
SUM_ROWS = 16

FFN_TM = 512
FFN_TF = 256
CAST_ROWS = 256
PROJ_MAIN = 3 * MOBA_WIDTH + 2 * GLA_K_WIDTH + GLA_V_WIDTH
PROJ_OUT = PROJ_MAIN + GLA_V_WIDTH
MIX_TM = 1024
MIX_SUB = 256
VMEM_LIMIT = 56 * 1024 * 1024

_NT = (((1,), (1,)), ((), ()))


def _rms(x, g):
    return x * lax.rsqrt(jnp.mean(x * x, axis=-1, keepdims=True) + RMS_EPS) * g


def _silu(x):
    return x / (1.0 + jnp.exp(-x))


def _resident(shape):
    return pl.BlockSpec(shape, lambda *_: (0,) * len(shape), pipeline_mode=pl.Buffered(1))


def _interleave(stage_lists, lag):
    n_slots = max(len(st) + j * lag for j, st in enumerate(stage_lists))
    for k in range(n_slots):
        for j, st in enumerate(stage_lists):
            if 0 <= k - j * lag < len(st):
                st[k - j * lag]()


def _cast_weight(src_hbm, stage_ref, sem, store, n_rows, col0, n_cols, row0=0, chunk_rows=CAST_ROWS):
    n_chunk = n_rows // chunk_rows
    n_slot = stage_ref.shape[0]

    def copy(c):
        slot = c % n_slot
        return pltpu.make_async_copy(
            src_hbm.at[pl.ds(row0 + c * chunk_rows, chunk_rows), pl.ds(col0, n_cols)],
            stage_ref.at[slot, pl.ds(0, chunk_rows), pl.ds(0, n_cols)], sem.at[slot])

    for c in range(min(n_slot, n_chunk)):
        copy(c).start()
    for c in range(n_chunk):
        copy(c).wait()
        store(slice(c * chunk_rows, (c + 1) * chunk_rows),
              stage_ref[c % n_slot, 0:chunk_rows, 0:n_cols])
        if c + n_slot < n_chunk:
            copy(c + n_slot).start()


def _cast_into(src_hbm, dst_ref, stage_ref, sem, col0=0):
    def store(rows, chunk):
        dst_ref[rows, :] = chunk.astype(jnp.bfloat16)

    _cast_weight(src_hbm, stage_ref, sem, store, dst_ref.shape[0], col0, dst_ref.shape[1])


_HBM = pl.BlockSpec(memory_space=pl.ANY)


def _cast_scratch(n_slot):
    return [pltpu.VMEM((n_slot, CAST_ROWS, D_FF), jnp.float32), pltpu.SemaphoreType.DMA((n_slot,))]


def _row(width, rows=FFN_TM):
    return pl.BlockSpec((rows, width), lambda i: (i, 0))


_ROW_PARAMS = dict(
    compiler_params=pltpu.CompilerParams(
        dimension_semantics=("arbitrary",), vmem_limit_bytes=VMEM_LIMIT))


def _ffn_proj_stages(r, x_ref, pre_ref, post_ref, g_ref, wup_ref, ba_ref,
                     x1_ref, pj_ref, kmean_ref, la_ref,
                     h_ref, wg_ref, wu_ref, wd_ref, wmain_ref, wgr_ref, wat_ref):
    rows = slice(r * MOBA_BLOCK, (r + 1) * MOBA_BLOCK)
    st = {}

    def prenorm():
        st["xn"] = _rms(x_ref[rows, :], pre_ref[...]).astype(jnp.bfloat16)

    def up(c):
        def run():
            cols = slice(c * FFN_TF, (c + 1) * FFN_TF)
            g = jnp.dot(st["xn"], wg_ref[:, cols], preferred_element_type=jnp.float32)
            u = jnp.dot(st["xn"], wu_ref[:, cols], preferred_element_type=jnp.float32)
            h_ref[rows, cols] = (_silu(g) * u).astype(jnp.bfloat16)
        return run

    def down():
        st["f"] = jnp.dot(h_ref[rows, :], wd_ref[...], preferred_element_type=jnp.float32)

    def mid():
        x1 = x_ref[rows, :] + 0.5 * _rms(st.pop("f"), post_ref[...])
        x1_ref[rows, :] = x1
        st["h"] = _rms(x1, g_ref[...]).astype(jnp.bfloat16)

    def proj(w_ref, lo, hi):
        return jnp.dot(st["h"], w_ref[:, lo:hi], preferred_element_type=jnp.float32)

    w, kw, vw = MOBA_WIDTH, GLA_K_WIDTH, GLA_V_WIDTH

    def moba_q():
        pj_ref[rows, 0:w] = (proj(wmain_ref, 0, w) * MOBA_Q_SCALE).astype(jnp.bfloat16)

    def moba_k():
        mk = proj(wmain_ref, w, 2 * w)
        pj_ref[rows, w:2 * w] = mk.astype(jnp.bfloat16)
        kmean_ref[r] = jnp.mean(mk, axis=0, keepdims=True)

    def moba_v():
        pj_ref[rows, 2 * w:3 * w] = proj(wmain_ref, 2 * w, 3 * w).astype(jnp.bfloat16)

    def gla_qk():
        pj_ref[rows, 3 * w:3 * w + 2 * kw] = proj(wmain_ref, 3 * w, 3 * w + 2 * kw).astype(jnp.bfloat16)

    def gla_v():
        pj_ref[rows, 3 * w + 2 * kw:PROJ_MAIN] = proj(wmain_ref, 3 * w + 2 * kw, PROJ_MAIN).astype(jnp.bfloat16)

    def gla_gate():
        pj_ref[rows, PROJ_MAIN:PROJ_MAIN + vw] = proj(wgr_ref, 0, vw).astype(jnp.bfloat16)

    def gla_decay():
        ga = lax.dot_general(st["h"], wat_ref[...], _NT, preferred_element_type=jnp.float32)
        z = jnp.dot(ga.astype(jnp.bfloat16), wup_ref[...],
                    preferred_element_type=jnp.float32) + ba_ref[...]
        log_sig = jnp.minimum(z, 0.0) - jnp.log(1.0 + jnp.exp(-jnp.abs(z)))
        la_ref[rows, :] = log_sig * (LOG2_E / GLA_GATE_TAU)

    return ([prenorm] + [up(c) for c in range(D_FF // FFN_TF)] + [down, mid]
            + [moba_q, moba_k, moba_v, gla_qk, gla_v, gla_gate, gla_decay])


def _ffn_proj_kernel(x_ref, pre_ref, wg_hbm, wu_hbm, wd_hbm, post_ref, g_ref, wint_hbm, wup_ref, ba_ref,
                     *rest, n_next):
    next_in, rest = rest[:n_next], rest[n_next:]
    outs, next_out, scratch = rest[:4], rest[4:4 + n_next], rest[4 + n_next:]
    h_ref, wg_ref, wu_ref, wd_ref, wmain_ref, wgr_ref, wat_ref, stage_ref, sem = scratch

    @pl.when(pl.program_id(0) == 0)
    def _():
        _cast_into(wg_hbm, wg_ref, stage_ref, sem)
        _cast_into(wu_hbm, wu_ref, stage_ref, sem)
        _cast_into(wd_hbm, wd_ref, stage_ref, sem)

        def store_main(rows, chunk):
            wmain_ref[:, rows] = chunk.T.astype(jnp.bfloat16)

        _cast_weight(wint_hbm, stage_ref, sem, store_main, PROJ_MAIN, 0, D_MODEL)

        def store_gate(rows, chunk):
            wgr_ref[:, rows] = chunk.T.astype(jnp.bfloat16)

        _cast_weight(wint_hbm, stage_ref, sem, store_gate, GLA_V_WIDTH, 0, D_MODEL,
                     row0=PROJ_MAIN + GLA_GATE_RANK)

        def store_rank(rows, chunk):
            wat_ref[rows, :] = chunk.astype(jnp.bfloat16)

        _cast_weight(wint_hbm, stage_ref, sem, store_rank, GATE_PAD, 0, D_MODEL, row0=PROJ_MAIN,
                     chunk_rows=GATE_PAD)

    refs = (x_ref, pre_ref, post_ref, g_ref, wup_ref, ba_ref) + outs + (
        h_ref, wg_ref, wu_ref, wd_ref, wmain_ref, wgr_ref, wat_ref)
    _interleave([_ffn_proj_stages(r, *refs) for r in range(FFN_TM // MOBA_BLOCK)], lag=4)
    for src_ref, dst_ref in zip(next_in, next_out):
        dst_ref[...] = src_ref[...].astype(jnp.bfloat16)


def _ffn_proj(x, pre_g, wg, wu, wd, post_g, g, w_in, wup, ba, next_weights):
    t = x.shape[0]
    n_step = t // FFN_TM
    nblk = FFN_TM // MOBA_BLOCK

    def out(width, dtype=jnp.bfloat16):
        return jax.ShapeDtypeStruct((t, width), dtype)

    def slice_spec(w):
        rows = w.shape[0] // n_step
        rep = 1 if rows % BF16_SUBLANES == 0 else 2
        assert (rows * rep) % BF16_SUBLANES == 0 and w.shape[0] % n_step == 0
        return pl.BlockSpec((rows * rep, w.shape[1]), lambda i: (i // rep, 0))

    next_specs = [slice_spec(w) for w in next_weights]
    return pl.pallas_call(
        functools.partial(_ffn_proj_kernel, n_next=len(next_weights)),
        grid=(n_step,),
        in_specs=[_row(D_MODEL), _resident(pre_g.shape), _HBM, _HBM, _HBM, _resident(post_g.shape),
                  _resident(g.shape), _HBM, _resident(wup.shape), _resident(ba.shape)] + next_specs,
        out_specs=[_row(D_MODEL), _row(PROJ_OUT),
                   pl.BlockSpec((nblk, 1, MOBA_WIDTH), lambda i: (i, 0, 0)), _row(GLA_K_WIDTH)] + next_specs,
        out_shape=[out(D_MODEL, jnp.float32), out(PROJ_OUT),
                   jax.ShapeDtypeStruct((t // MOBA_BLOCK, 1, MOBA_WIDTH), jnp.float32),
                   out(GLA_K_WIDTH, jnp.float32)]
        + [jax.ShapeDtypeStruct(w.shape, jnp.bfloat16) for w in next_weights],
        scratch_shapes=[pltpu.VMEM((FFN_TM, D_FF), jnp.bfloat16),
                        pltpu.VMEM((D_MODEL, D_FF), jnp.bfloat16), pltpu.VMEM((D_MODEL, D_FF), jnp.bfloat16),
                        pltpu.VMEM((D_FF, D_MODEL), jnp.bfloat16),
                        pltpu.VMEM((D_MODEL, PROJ_MAIN), jnp.bfloat16),
                        pltpu.VMEM((D_MODEL, GLA_V_WIDTH), jnp.bfloat16),
                        pltpu.VMEM((GATE_PAD, D_MODEL), jnp.bfloat16)] + _cast_scratch(3),
        name="ffn_proj", **_ROW_PARAMS,
    )(x, pre_g, wg, wu, wd, post_g, g, w_in, wup, ba, *next_weights)


def _mix_ffn_stages(rows, x_ref, om_ref, og_ref, gr_ref, mg_ref, gg_ref, mpost_ref, pre_ref, post_ref,
                    o_ref, h_ref, wo_ref, wg_ref, wu_ref, wd_ref):
    st = {}

    def mix_in():
        om = _rms(om_ref[rows, :].astype(jnp.float32), mg_ref[...]).astype(jnp.bfloat16)
        gate = _silu(gr_ref[rows, :].astype(jnp.float32))
        og = og_ref[rows, :].astype(jnp.float32)
        parts = [om]
        for h in range(GLA_HEADS):
            cols = slice(h * GLA_VALUE_DIM, (h + 1) * GLA_VALUE_DIM)
            parts.append((_rms(og[:, cols], gg_ref[:, cols]) * gate[:, cols]).astype(jnp.bfloat16))
        st["mix_in"] = jnp.concatenate(parts, axis=-1)

    def mix_out():
        st["mix"] = jnp.dot(st.pop("mix_in"), wo_ref[...], preferred_element_type=jnp.float32)

    def norms():
        x2 = x_ref[rows, :] + _rms(st.pop("mix"), mpost_ref[...])
        st["x2"] = x2
        st["xn"] = _rms(x2, pre_ref[...]).astype(jnp.bfloat16)

    def up(c):
        def run():
            cols = slice(c * FFN_TF, (c + 1) * FFN_TF)
            g = jnp.dot(st["xn"], wg_ref[:, cols], preferred_element_type=jnp.float32)
            u = jnp.dot(st["xn"], wu_ref[:, cols], preferred_element_type=jnp.float32)
            h_ref[rows, cols] = (_silu(g) * u).astype(jnp.bfloat16)
        return run

    def down():
        st["f"] = jnp.dot(h_ref[rows, :], wd_ref[...], preferred_element_type=jnp.float32)

    def finish():
        o_ref[rows, :] = st.pop("x2") + 0.5 * _rms(st.pop("f"), post_ref[...])

    return [mix_in, mix_out, norms] + [up(c) for c in range(D_FF // FFN_TF)] + [down, finish]


def _mix_ffn_kernel(x_ref, om_ref, og_ref, gr_ref, mg_ref, gg_ref, wo_ref, mpost_ref,
                    pre_ref, wg_ref, wu_ref, wd_ref, post_ref, o_ref, h_ref):
    refs = (x_ref, om_ref, og_ref, gr_ref, mg_ref, gg_ref, mpost_ref, pre_ref, post_ref,
            o_ref, h_ref, wo_ref, wg_ref, wu_ref, wd_ref)
    _interleave([_mix_ffn_stages(slice(r * MIX_SUB, (r + 1) * MIX_SUB), *refs)
                 for r in range(MIX_TM // MIX_SUB)], lag=4)


def _mix_ffn(x, om, og, pj, mg, gg, wo, mpost_g, pre_g, wg, wu, wd, post_g):
    t = x.shape[0]
    params = (mg, gg, wo, mpost_g, pre_g, wg, wu, wd, post_g)
    return pl.pallas_call(
        _mix_ffn_kernel,
        grid=(t // MIX_TM,),
        in_specs=[_row(D_MODEL, MIX_TM), _row(MOBA_WIDTH, MIX_TM), _row(GLA_V_WIDTH, MIX_TM),
                  pl.BlockSpec((MIX_TM, GLA_V_WIDTH), lambda i: (i, PROJ_MAIN // GLA_V_WIDTH))]
        + [_resident(p.shape) for p in params],
        out_specs=_row(D_MODEL, MIX_TM),
        out_shape=jax.ShapeDtypeStruct((t, D_MODEL), jnp.float32),
        scratch_shapes=[pltpu.VMEM((MIX_TM, D_FF), jnp.bfloat16)],
        name="mix_ffn", **_ROW_PARAMS,
    )(x, om, og, pj, *params)


def _moba_kernel(q_ref, k_ref, v_ref, kmean_ref, o_ref, kaug_ref, *, n_blk):
    blk = MOBA_BLOCK
    dh = MOBA_HEAD_DIM
    seq = n_blk * blk
    n_head = LANES // dh
    wide = n_head * blk

    key_blk = lax.broadcasted_iota(jnp.int32, (seq, LANES), 0) // blk
    hot_col = lax.broadcasted_iota(jnp.int32, (seq, LANES), 1)
    one_hot = jnp.where(key_blk == hot_col, 1.0, 0.0).astype(jnp.bfloat16)
    feat = lax.broadcasted_iota(jnp.int32, (LANES, blk), 0)
    head_rows = [(feat >= h * dh) & (feat < (h + 1) * dh) for h in range(n_head)]
    blk_row = lax.broadcasted_iota(jnp.int32, (n_blk, wide), 0)
    key_i = lax.broadcasted_iota(jnp.int32, (blk, wide), 0)
    qry_i = lax.broadcasted_iota(jnp.int32, (blk, wide), 1) % blk
    causal = key_i <= qry_i
    bias_pad = jnp.zeros((LANES - n_blk, wide), jnp.float32)
    ones = jnp.ones((SUM_ROWS, blk), jnp.float32)

    def column(c):
        lanes = slice(c * LANES, (c + 1) * LANES)
        kmean = kmean_ref[0][:, lanes]
        km_hi = kmean.astype(jnp.bfloat16)
        km_lo = (kmean - km_hi.astype(jnp.float32)).astype(jnp.bfloat16)
        kaug_ref[c, :, 0:LANES] = k_ref[:, lanes]
        kaug_ref[c, :, LANES:2 * LANES] = one_hot
        v_t = [None] * n_blk

        def value_block(j):
            if v_t[j] is None:
                v_j = v_ref[j * blk:(j + 1) * blk, lanes].astype(jnp.float32).T
                v_t[j] = jnp.concatenate([v_j, ones], axis=0).astype(jnp.bfloat16)
            return v_t[j]

        def section(i):
            st = {}
            n_keys = (i + 1) * blk

            def prep():
                q_i = q_ref[i * blk:(i + 1) * blk, lanes].astype(jnp.float32).T
                qi_f = jnp.concatenate([jnp.where(rows, q_i, 0.0) for rows in head_rows], axis=1)
                qi_t = qi_f.astype(jnp.bfloat16)
                if i > MOBA_TOPK:
                    gate = (jnp.dot(km_hi, qi_t, preferred_element_type=jnp.float32)
                            + jnp.dot(km_lo, qi_t, preferred_element_type=jnp.float32))
                    rank = jnp.zeros((n_blk, wide), jnp.float32)
                    for j in range(i):
                        gj = gate[j:j + 1, :]
                        beats = (gj > gate) | ((gj == gate) & (blk_row > j))
                        rank = rank + jnp.where(beats, 1.0, 0.0)
                    keep = (rank < float(MOBA_TOPK)) | (blk_row >= i)
                    bias = jnp.where(keep, 0.0, NEG_BIG)
                    st["q"] = jnp.concatenate([qi_f, bias, bias_pad], axis=0).astype(jnp.bfloat16)
                else:
                    st["q"] = qi_t

            def scores():
                if i > MOBA_TOPK:
                    s = jnp.dot(kaug_ref[c, 0:n_keys, :], st.pop("q"),
                                preferred_element_type=jnp.float32)
                else:
                    s = jnp.dot(k_ref[0:n_keys, lanes], st.pop("q"), preferred_element_type=jnp.float32)
                s_own = jnp.where(causal, s[i * blk:n_keys, :], NEG_BIG)
                sm = s_own if i == 0 else jnp.concatenate([s[0:i * blk, :], s_own], axis=0)
                st["s"] = sm
                st["m"] = jnp.max(sm, axis=0, keepdims=True)

            def probs():
                st["p"] = jnp.exp2(st.pop("s") - st.pop("m")).astype(jnp.bfloat16)

            def values():
                v_i = jnp.concatenate([value_block(j) for j in range(i + 1)], axis=1)
                o_t = jnp.dot(v_i, st.pop("p"), preferred_element_type=jnp.float32)
                o_t = o_t[0:LANES, :] * (1.0 / o_t[LANES:LANES + 1, :])
                o_own = jnp.concatenate([o_t[h * dh:(h + 1) * dh, h * blk:(h + 1) * blk]
                                         for h in range(n_head)], axis=0)
                o_ref[i * blk:(i + 1) * blk, lanes] = o_own.T.astype(o_ref.dtype)

            return [prep, scores, probs, values]

        return [section(j) + section(n_blk - 1 - j) for j in range(n_blk // 2)]

    _interleave([stream for c in range(MOBA_COLS) for stream in column(c)], lag=1)


def _moba(pj, kmean, batch, seq):
    n_blk = seq // MOBA_BLOCK
    width = MOBA_COLS * LANES
    n_col = MOBA_WIDTH // width

    def tok(group):
        return pl.BlockSpec((seq, width), lambda b, c: (b, group * n_col + c))

    return pl.pallas_call(
        functools.partial(_moba_kernel, n_blk=n_blk),
        grid=(batch, n_col),
        in_specs=[tok(0), tok(1), tok(2), pl.BlockSpec((1, n_blk, width), lambda b, c: (b, 0, c))],
        out_specs=tok(0),
        out_shape=jax.ShapeDtypeStruct((batch * seq, MOBA_WIDTH), jnp.bfloat16),
        scratch_shapes=[pltpu.VMEM((MOBA_COLS, seq, 2 * LANES), jnp.bfloat16)],
        compiler_params=pltpu.CompilerParams(
            dimension_semantics=("arbitrary", "arbitrary"), vmem_limit_bytes=VMEM_LIMIT),
        name="moba",
    )(pj, pj, pj, kmean)


def _gla_kernel(q_ref, k_ref, v_ref, la_ref, o_ref, st_ref):
    @pl.when(pl.program_id(1) == 0)
    def _():
        st_ref[...] = jnp.zeros_like(st_ref)

    for r in range(GLA_STEP // GLA_TILE):
        rows = slice(r * GLA_TILE, (r + 1) * GLA_TILE)
        _gla_tile(q_ref[rows, :], k_ref[rows, :], v_ref.at[rows, :], la_ref[rows, :],
                  o_ref.at[rows, :], st_ref)


def _gla_tile(q, k, v_ref, la, o_ref, st_ref):
    c = GLA_TILE
    dk, dv = GLA_KEY_DIM, GLA_VALUE_DIM
    row = lax.broadcasted_iota(jnp.int32, la.shape, 0)
    lane = lax.broadcasted_iota(jnp.int32, la.shape, 1)
    odd_head = (lane & dk) != 0
    st_lane = lax.broadcasted_iota(jnp.int32, (dv, GLA_K_WIDTH), 1)

    b = la
    step = 1
    while step < c:
        b = b + jnp.where(row >= step, pltpu.roll(b, step, 0), 0.0)
        step *= 2

    qf = q.astype(jnp.float32) * (dk ** -0.5)
    kf = k.astype(jnp.float32)

    sub = c // 2
    t_i = lax.broadcasted_iota(jnp.int32, (sub, sub), 0)
    s_i = lax.broadcasted_iota(jnp.int32, (sub, sub), 1)
    level = jnp.where(t_i >= s_i, 31 - lax.clz(t_i ^ s_i), -2)

    def split_heads(x):
        return (jnp.where(odd_head, 0.0, x).astype(jnp.bfloat16),
                jnp.where(odd_head, x, 0.0).astype(jnp.bfloat16))

    def level_products(q_lvl, k_lvl, row_blk, col_blk):
        q_b = q_lvl.astype(jnp.bfloat16)
        k_even, k_odd = split_heads(k_lvl)
        rows = slice(row_blk * sub, (row_blk + 1) * sub)
        keys = slice(col_blk * sub, (col_blk + 1) * sub)
        out = []
        for pair in range(GLA_HEADS // 2):
            cols = slice(pair * LANES, (pair + 1) * LANES)
            k_pair = jnp.concatenate([k_even[keys, cols], k_odd[keys, cols]], axis=0)
            a = lax.dot_general(q_b[rows, cols], k_pair, _NT, preferred_element_type=jnp.float32)
            out += [a[:, 0:sub], a[:, sub:2 * sub]]
        return out

    def add_level(acc, q_lvl, k_lvl, lvl):
        hit = level == lvl
        for r in range(2):
            for h, a in enumerate(level_products(q_lvl, k_lvl, r, r)):
                acc[r][h] = jnp.where(hit, a, acc[r][h])
        return acc

    acc = [[jnp.zeros((sub, sub), jnp.float32) for _ in range(GLA_HEADS)] for _ in range(2)]
    acc = add_level(acc, qf, kf, -1)
    a_cross = None
    b_end = b
    half = 1
    lvl = 0
    while half < c:
        b_before = pltpu.roll(b_end, half, 0)
        q_lvl = qf * jnp.exp2(jnp.minimum(b - b_before, 0.0))
        k_lvl = kf * jnp.exp2(b_end - b)
        if half < sub:
            acc = add_level(acc, q_lvl, k_lvl, lvl)
        else:
            a_cross = level_products(q_lvl, k_lvl, 1, 0)
        upper = (row & half) != 0
        b_end = jnp.where(upper, b_end, pltpu.roll(b_end, c - half, 0))
        half *= 2
        lvl += 1
    q_in = (qf * jnp.exp2(b)).astype(jnp.bfloat16)
    k_out = (kf * jnp.exp2(b_end - b)).astype(jnp.bfloat16)
    tile_decay = jnp.exp2(b_end[0:1, :])

    for h in range(GLA_HEADS):
        cols = slice((h // 2) * LANES, (h // 2 + 1) * LANES)
        v_h = v_ref[:, h * dv:(h + 1) * dv]
        st = st_ref[h]
        o_inter = lax.dot_general(q_in[:, cols], st[:, cols].astype(jnp.bfloat16), _NT,
                                  preferred_element_type=jnp.float32)
        a_lo = acc[0][h].astype(jnp.bfloat16)
        a_hi = jnp.concatenate([a_cross[h], acc[1][h]], axis=1).astype(jnp.bfloat16)
        o_intra = jnp.concatenate(
            [jnp.dot(a_lo, v_h[0:sub, :], preferred_element_type=jnp.float32),
             jnp.dot(a_hi, v_h, preferred_element_type=jnp.float32)], axis=0)
        o_ref[:, h * dv:(h + 1) * dv] = (o_inter + o_intra).astype(o_ref.dtype)
        v_t = v_h.astype(jnp.float32).T.astype(jnp.bfloat16)
        upd = jnp.dot(v_t, k_out, preferred_element_type=jnp.float32)
        own = (st_lane >= h * dk) & (st_lane < (h + 1) * dk)
        st_ref[h] = st * tile_decay + jnp.where(own, upd, 0.0)


def _gla(pj, la, batch, seq):
    n_tile = seq // GLA_STEP

    def tok(width, col0=0):
        return pl.BlockSpec((GLA_STEP, width), lambda b, t: (b * n_tile + t, col0 // width))

    return pl.pallas_call(
        _gla_kernel,
        grid=(batch, n_tile),
        in_specs=[tok(GLA_K_WIDTH, 3 * MOBA_WIDTH), tok(GLA_K_WIDTH, 3 * MOBA_WIDTH + GLA_K_WIDTH),
                  tok(GLA_V_WIDTH, 3 * MOBA_WIDTH + 2 * GLA_K_WIDTH), tok(GLA_K_WIDTH)],
        out_specs=tok(GLA_V_WIDTH),
        out_shape=jax.ShapeDtypeStruct((batch * seq, GLA_V_WIDTH), jnp.bfloat16),
        scratch_shapes=[pltpu.VMEM((GLA_HEADS, GLA_VALUE_DIM, GLA_K_WIDTH), jnp.float32)],
        compiler_params=pltpu.CompilerParams(
            dimension_semantics=("arbitrary", "arbitrary"), vmem_limit_bytes=VMEM_LIMIT),
        name="gla",
    )(pj, pj, pj, la)


def kernel(x, ffn1_pre_g, ffn1_w_gate, ffn1_w_up, ffn1_w_down, ffn1_post_g, mix_pre_g, w_in,
           gla_w_alpha_up, gla_b_alpha, moba_out_g, gla_out_g, w_out, mix_post_g, ffn2_pre_g,
           ffn2_w_gate, ffn2_w_up, ffn2_w_down, ffn2_post_g):
    batch, seq, d = x.shape
    assert d == D_MODEL and seq % MOBA_BLOCK == 0 and seq % GLA_STEP == 0
    assert (batch * seq) % FFN_TM == 0 and (batch * seq) % MIX_TM == 0
    bf = jnp.bfloat16
    xt = x.reshape(batch * seq, d)
    for l in range(w_in.shape[0]):
        wup = jnp.pad(gla_w_alpha_up[l], ((0, GATE_PAD - GLA_GATE_RANK), (0, 0))).astype(bf)
        xt, pj, kmean, la, wo, wg2, wu2, wd2 = _ffn_proj(
            xt, ffn1_pre_g[l][None], ffn1_w_gate[l], ffn1_w_up[l], ffn1_w_down[l], ffn1_post_g[l][None],
            mix_pre_g[l][None], w_in[l].T, wup, gla_b_alpha[l][None],
            (w_out[l], ffn2_w_gate[l], ffn2_w_up[l], ffn2_w_down[l]))

        kmean = kmean.reshape(batch, seq // MOBA_BLOCK, MOBA_WIDTH)
        o_moba = _moba(pj, kmean, batch, seq)
        o_gla = _gla(pj, la, batch, seq)

        xt = _mix_ffn(xt, o_moba, o_gla, pj, moba_out_g[l][None],
                      gla_out_g[l].reshape(1, GLA_V_WIDTH), wo, mix_post_g[l][None],
                      ffn2_pre_g[l][None], wg2, wu2, wd2, ffn2_post_g[l][None])
    return xt.reshape(batch, seq, d)
```

```python
import functools

import jax
import jax.numpy as jnp
from jax import lax
from jax.experimental import pallas as pl
from jax.experimental.pallas import tpu as pltpu

D_MODEL = 1024
D_FF = 2816
RMS_EPS = 1e-6

MOBA_WIDTH = 512
MOBA_HEAD_DIM = 64
MOBA_BLOCK = 256
MOBA_TOPK = 3
MOBA_COLS = 2
LOG2_E = 1.4426950408889634
MOBA_Q_SCALE = MOBA_HEAD_DIM ** -0.5 * LOG2_E

GLA_HEADS = 4
GLA_KEY_DIM = 64
GLA_VALUE_DIM = 128
GLA_K_WIDTH = GLA_HEADS * GLA_KEY_DIM
GLA_V_WIDTH = GLA_HEADS * GLA_VALUE_DIM
GLA_GATE_RANK = 16
GLA_GATE_TAU = 16.0
GLA_TILE = 256
GLA_STEP = 1024

LANES = 128
BF16_SUBLANES = 16
GATE_PAD = LANES
NEG_BIG = -1e30
SUM_ROWS = 16

FFN_TM = 512
FFN_TF = 256
CAST_ROWS = 256
PROJ_MAIN = 3 * MOBA_WIDTH + 2 * GLA_K_WIDTH + GLA_V_WIDTH
PROJ_OUT = PROJ_MAIN + GLA_V_WIDTH
MIX_TM = 1024
MIX_SUB = 256
VMEM_LIMIT = 56 * 1024 * 1024

_NT = (((1,), (1,)), ((), ()))


def _rms(x, g):
    return x * lax.rsqrt(jnp.mean(x * x, axis=-1, keepdims=True) + RMS_EPS) * g


def _silu(x):
    return x / (1.0 + jnp.exp(-x))


def _resident(shape):
    return pl.BlockSpec(shape, lambda *_: (0,) * len(shape), pipeline_mode=pl.Buffered(1))


def _interleave(stage_lists, lag):
    n_slots = max(len(st) + j * lag for j, st in enumerate(stage_lists))
    for k in range(n_slots):
        for j, st in enumerate(stage_lists):
            if 0 <= k - j * lag < len(st):
                st[k - j * lag]()


def _cast_weights(jobs, stage_ref, sem):
    n_slot = stage_ref.shape[0]
    per_job = [[(src, store, row0, k, n_cols, rows) for k in range(n_rows // rows)]
               for src, store, n_rows, row0, n_cols, rows in jobs]
    chunks = [job[k] for k in range(max(map(len, per_job))) for job in per_job if k < len(job)]

    def copy(c):
        src, _, row0, k, n_cols, rows = chunks[c]
        slot = c % n_slot
        return pltpu.make_async_copy(
            src.at[pl.ds(row0 + k * rows, rows), pl.ds(0, n_cols)],
            stage_ref.at[slot, pl.ds(0, rows), pl.ds(0, n_cols)], sem.at[slot])

    for c in range(min(n_slot, len(chunks))):
        copy(c).start()
    for c, (_, store, _, k, n_cols, rows) in enumerate(chunks):
        copy(c).wait()
        store(slice(k * rows, (k + 1) * rows), stage_ref[c % n_slot, 0:rows, 0:n_cols])
        if c + n_slot < len(chunks):
            copy(c + n_slot).start()


def _cast_job(src_hbm, dst_ref):
    def store(rows, chunk):
        dst_ref[rows, :] = chunk.astype(jnp.bfloat16)

    return (src_hbm, store, dst_ref.shape[0], 0, dst_ref.shape[1], CAST_ROWS)


_HBM = pl.BlockSpec(memory_space=pl.ANY)


def _cast_scratch(n_slot):
    return [pltpu.VMEM((n_slot, CAST_ROWS, D_FF), jnp.float32), pltpu.SemaphoreType.DMA((n_slot,))]


def _row(width, rows=FFN_TM):
    return pl.BlockSpec((rows, width), lambda i: (i, 0))


_ROW_PARAMS = dict(
    compiler_params=pltpu.CompilerParams(
        dimension_semantics=("arbitrary",), vmem_limit_bytes=VMEM_LIMIT))


def _ffn_proj_stages(r, x_ref, pre_ref, post_ref, g_ref, wup_ref, ba_ref,
                     x1_ref, pj_ref, kmean_ref, la_ref,
                     h_ref, wg_ref, wu_ref, wd_ref, wmain_ref, wgr_ref, wat_ref):
    rows = slice(r * MOBA_BLOCK, (r + 1) * MOBA_BLOCK)
    st = {}

    def prenorm():
        st["xn"] = _rms(x_ref[rows, :], pre_ref[...]).astype(jnp.bfloat16)

    def up(c):
        def run():
            cols = slice(c * FFN_TF, (c + 1) * FFN_TF)
            g = jnp.dot(st["xn"], wg_ref[:, cols], preferred_element_type=jnp.float32)
            u = jnp.dot(st["xn"], wu_ref[:, cols], preferred_element_type=jnp.float32)
            h_ref[rows, cols] = (_silu(g) * u).astype(jnp.bfloat16)
        return run

    def down():
        st["f"] = jnp.dot(h_ref[rows, :], wd_ref[...], preferred_element_type=jnp.float32)

    def mid():
        x1 = x_ref[rows, :] + 0.5 * _rms(st.pop("f"), post_ref[...])
        x1_ref[rows, :] = x1
        st["h"] = _rms(x1, g_ref[...]).astype(jnp.bfloat16)

    def proj(w_ref, lo, hi):
        return jnp.dot(st["h"], w_ref[:, lo:hi], preferred_element_type=jnp.float32)

    w, kw, vw = MOBA_WIDTH, GLA_K_WIDTH, GLA_V_WIDTH

    def moba_q():
        pj_ref[rows, 0:w] = (proj(wmain_ref, 0, w) * MOBA_Q_SCALE).astype(jnp.bfloat16)

    def moba_k():
        mk = proj(wmain_ref, w, 2 * w)
        pj_ref[rows, w:2 * w] = mk.astype(jnp.bfloat16)
        kmean_ref[r] = jnp.mean(mk, axis=0, keepdims=True)

    def moba_v():
        pj_ref[rows, 2 * w:3 * w] = proj(wmain_ref, 2 * w, 3 * w).astype(jnp.bfloat16)

    def gla_qk():
        pj_ref[rows, 3 * w:3 * w + 2 * kw] = proj(wmain_ref, 3 * w, 3 * w + 2 * kw).astype(jnp.bfloat16)

    def gla_v():
        pj_ref[rows, 3 * w + 2 * kw:PROJ_MAIN] = proj(wmain_ref, 3 * w + 2 * kw, PROJ_MAIN).astype(jnp.bfloat16)

    def gla_gate():
        pj_ref[rows, PROJ_MAIN:PROJ_MAIN + vw] = proj(wgr_ref, 0, vw).astype(jnp.bfloat16)

    def gla_decay():
        ga = lax.dot_general(st["h"], wat_ref[...], _NT, preferred_element_type=jnp.float32)
        z = jnp.dot(ga.astype(jnp.bfloat16), wup_ref[...],
                    preferred_element_type=jnp.float32) + ba_ref[...]
        log_sig = jnp.minimum(z, 0.0) - jnp.log(1.0 + jnp.exp(-jnp.abs(z)))
        la_ref[rows, :] = log_sig * (LOG2_E / GLA_GATE_TAU)

    return ([prenorm] + [up(c) for c in range(D_FF // FFN_TF)] + [down, mid]
            + [moba_q, moba_k, moba_v, gla_qk, gla_v, gla_gate, gla_decay])


def _ffn_proj_kernel(x_ref, pre_ref, wg_hbm, wu_hbm, wd_hbm, post_ref, g_ref, wint_hbm, wup_ref, ba_ref,
                     *rest, n_next):
    next_in, rest = rest[:n_next], rest[n_next:]
    outs, next_out, scratch = rest[:4], rest[4:4 + n_next], rest[4 + n_next:]
    h_ref, wg_ref, wu_ref, wd_ref, wmain_ref, wgr_ref, wat_ref, stage_ref, sem = scratch

    @pl.when(pl.program_id(0) == 0)
    def _():
        def store_main(rows, chunk):
            wmain_ref[:, rows] = chunk.T.astype(jnp.bfloat16)

        def store_gate(rows, chunk):
            wgr_ref[:, rows] = chunk.T.astype(jnp.bfloat16)

        def store_rank(rows, chunk):
            wat_ref[rows, :] = chunk.astype(jnp.bfloat16)

        _cast_weights(
            [_cast_job(wg_hbm, wg_ref),
             (wint_hbm, store_main, PROJ_MAIN, 0, D_MODEL, CAST_ROWS),
             _cast_job(wu_hbm, wu_ref),
             (wint_hbm, store_gate, GLA_V_WIDTH, PROJ_MAIN + GLA_GATE_RANK, D_MODEL, CAST_ROWS),
             _cast_job(wd_hbm, wd_ref),
             (wint_hbm, store_rank, GATE_PAD, PROJ_MAIN, D_MODEL, GATE_PAD)], stage_ref, sem)

    refs = (x_ref, pre_ref, post_ref, g_ref, wup_ref, ba_ref) + outs + (
        h_ref, wg_ref, wu_ref, wd_ref, wmain_ref, wgr_ref, wat_ref)
    _interleave([_ffn_proj_stages(r, *refs) for r in range(FFN_TM // MOBA_BLOCK)], lag=4)
    for src_ref, dst_ref in zip(next_in, next_out):
        dst_ref[...] = src_ref[...].astype(jnp.bfloat16)


def _ffn_proj(x, pre_g, wg, wu, wd, post_g, g, w_in, wup, ba, next_weights):
    t = x.shape[0]
    n_step = t // FFN_TM
    nblk = FFN_TM // MOBA_BLOCK

    def out(width, dtype=jnp.bfloat16):
        return jax.ShapeDtypeStruct((t, width), dtype)

    def slice_spec(w):
        rows = w.shape[0] // n_step
        rep = 1 if rows % BF16_SUBLANES == 0 else 2
        assert (rows * rep) % BF16_SUBLANES == 0 and w.shape[0] % n_step == 0
        return pl.BlockSpec((rows * rep, w.shape[1]), lambda i: (i // rep, 0))

    next_specs = [slice_spec(w) for w in next_weights]
    return pl.pallas_call(
        functools.partial(_ffn_proj_kernel, n_next=len(next_weights)),
        grid=(n_step,),
        in_specs=[_row(D_MODEL), _resident(pre_g.shape), _HBM, _HBM, _HBM, _resident(post_g.shape),
                  _resident(g.shape), _HBM, _resident(wup.shape), _resident(ba.shape)] + next_specs,
        out_specs=[_row(D_MODEL), _row(PROJ_OUT),
                   pl.BlockSpec((nblk, 1, MOBA_WIDTH), lambda i: (i, 0, 0)), _row(GLA_K_WIDTH)] + next_specs,
        out_shape=[out(D_MODEL, jnp.float32), out(PROJ_OUT),
                   jax.ShapeDtypeStruct((t // MOBA_BLOCK, 1, MOBA_WIDTH), jnp.float32),
                   out(GLA_K_WIDTH, jnp.float32)]
        + [jax.ShapeDtypeStruct(w.shape, jnp.bfloat16) for w in next_weights],
        scratch_shapes=[pltpu.VMEM((FFN_TM, D_FF), jnp.bfloat16),
                        pltpu.VMEM((D_MODEL, D_FF), jnp.bfloat16), pltpu.VMEM((D_MODEL, D_FF), jnp.bfloat16),
                        pltpu.VMEM((D_FF, D_MODEL), jnp.bfloat16),
                        pltpu.VMEM((D_MODEL, PROJ_MAIN), jnp.bfloat16),
                        pltpu.VMEM((D_MODEL, GLA_V_WIDTH), jnp.bfloat16),
                        pltpu.VMEM((GATE_PAD, D_MODEL), jnp.bfloat16)] + _cast_scratch(3),
        name="ffn_proj", **_ROW_PARAMS,
    )(x, pre_g, wg, wu, wd, post_g, g, w_in, wup, ba, *next_weights)


def _mix_ffn_stages(rows, x_ref, om_ref, og_ref, gr_ref, mg_ref, gg_ref, mpost_ref, pre_ref, post_ref,
                    o_ref, h_ref, wo_ref, wg_ref, wu_ref, wd_ref):
    st = {}

    def mix_in():
        om = _rms(om_ref[rows, :].astype(jnp.float32), mg_ref[...]).astype(jnp.bfloat16)
        gate = _silu(gr_ref[rows, :].astype(jnp.float32))
        og = og_ref[rows, :].astype(jnp.float32)
        parts = [om]
        for h in range(GLA_HEADS):
            cols = slice(h * GLA_VALUE_DIM, (h + 1) * GLA_VALUE_DIM)
            parts.append((_rms(og[:, cols], gg_ref[:, cols]) * gate[:, cols]).astype(jnp.bfloat16))
        st["mix_in"] = jnp.concatenate(parts, axis=-1)

    def mix_out():
        st["mix"] = jnp.dot(st.pop("mix_in"), wo_ref[...], preferred_element_type=jnp.float32)

    def norms():
        x2 = x_ref[rows, :] + _rms(st.pop("mix"), mpost_ref[...])
        st["x2"] = x2
        st["xn"] = _rms(x2, pre_ref[...]).astype(jnp.bfloat16)

    def up(c):
        def run():
            cols = slice(c * FFN_TF, (c + 1) * FFN_TF)
            g = jnp.dot(st["xn"], wg_ref[:, cols], preferred_element_type=jnp.float32)
            u = jnp.dot(st["xn"], wu_ref[:, cols], preferred_element_type=jnp.float32)
            h_ref[rows, cols] = (_silu(g) * u).astype(jnp.bfloat16)
        return run

    def down():
        st["f"] = jnp.dot(h_ref[rows, :], wd_ref[...], preferred_element_type=jnp.float32)

    def finish():
        o_ref[rows, :] = st.pop("x2") + 0.5 * _rms(st.pop("f"), post_ref[...])

    return [mix_in, mix_out, norms] + [up(c) for c in range(D_FF // FFN_TF)] + [down, finish]


def _mix_ffn_kernel(x_ref, om_ref, og_ref, gr_ref, mg_ref, gg_ref, wo_ref, mpost_ref,
                    pre_ref, wg_ref, wu_ref, wd_ref, post_ref, o_ref, h_ref):
    refs = (x_ref, om_ref, og_ref, gr_ref, mg_ref, gg_ref, mpost_ref, pre_ref, post_ref,
            o_ref, h_ref, wo_ref, wg_ref, wu_ref, wd_ref)
    _interleave([_mix_ffn_stages(slice(r * MIX_SUB, (r + 1) * MIX_SUB), *refs)
                 for r in range(MIX_TM // MIX_SUB)], lag=4)


def _mix_ffn(x, om, og, pj, mg, gg, wo, mpost_g, pre_g, wg, wu, wd, post_g):
    t = x.shape[0]
    params = (mg, gg, wo, mpost_g, pre_g, wg, wu, wd, post_g)
    return pl.pallas_call(
        _mix_ffn_kernel,
        grid=(t // MIX_TM,),
        in_specs=[_row(D_MODEL, MIX_TM), _row(MOBA_WIDTH, MIX_TM), _row(GLA_V_WIDTH, MIX_TM),
                  pl.BlockSpec((MIX_TM, GLA_V_WIDTH), lambda i: (i, PROJ_MAIN // GLA_V_WIDTH))]
        + [_resident(p.shape) for p in params],
        out_specs=_row(D_MODEL, MIX_TM),
        out_shape=jax.ShapeDtypeStruct((t, D_MODEL), jnp.float32),
        scratch_shapes=[pltpu.VMEM((MIX_TM, D_FF), jnp.bfloat16)],
        name="mix_ffn", **_ROW_PARAMS,
    )(x, om, og, pj, *params)


def _moba_kernel(q_ref, k_ref, v_ref, kmean_ref, o_ref, kaug_ref, *, n_blk):
    blk = MOBA_BLOCK
    dh = MOBA_HEAD_DIM
    seq = n_blk * blk
    n_head = LANES // dh
    wide = n_head * blk

    key_blk = lax.broadcasted_iota(jnp.int32, (seq, LANES), 0) // blk
    hot_col = lax.broadcasted_iota(jnp.int32, (seq, LANES), 1)
    one_hot = jnp.where(key_blk == hot_col, 1.0, 0.0).astype(jnp.bfloat16)
    feat = lax.broadcasted_iota(jnp.int32, (LANES, blk), 0)
    head_rows = [(feat >= h * dh) & (feat < (h + 1) * dh) for h in range(n_head)]
    blk_row = lax.broadcasted_iota(jnp.int32, (n_blk, wide), 0)
    key_i = lax.broadcasted_iota(jnp.int32, (blk, wide), 0)
    qry_i = lax.broadcasted_iota(jnp.int32, (blk, wide), 1) % blk
    causal = key_i <= qry_i
    bias_pad = jnp.zeros((LANES - n_blk, wide), jnp.float32)
    ones = jnp.ones((SUM_ROWS, blk), jnp.float32)

    def column(c):
        lanes = slice(c * LANES, (c + 1) * LANES)
        kmean = kmean_ref[0][:, lanes]
        km_hi = kmean.astype(jnp.bfloat16)
        km_lo = (kmean - km_hi.astype(jnp.float32)).astype(jnp.bfloat16)
        kaug_ref[c, :, 0:LANES] = k_ref[:, lanes]
        kaug_ref[c, :, LANES:2 * LANES] = one_hot
        v_t = [None] * n_blk

        def value_block(j):
            if v_t[j] is None:
                v_j = v_ref[j * blk:(j + 1) * blk, lanes].astype(jnp.float32).T
                v_t[j] = jnp.concatenate([v_j, ones], axis=0).astype(jnp.bfloat16)
            return v_t[j]

        def section(i):
            st = {}
            n_keys = (i + 1) * blk

            def prep():
                q_i = q_ref[i * blk:(i + 1) * blk, lanes].astype(jnp.float32).T
                qi_f = jnp.concatenate([jnp.where(rows, q_i, 0.0) for rows in head_rows], axis=1)
                qi_t = qi_f.astype(jnp.bfloat16)
                if i > MOBA_TOPK:
                    gate = (jnp.dot(km_hi, qi_t, preferred_element_type=jnp.float32)
                            + jnp.dot(km_lo, qi_t, preferred_element_type=jnp.float32))
                    rank = jnp.zeros((n_blk, wide), jnp.float32)
                    for j in range(i):
                        gj = gate[j:j + 1, :]
                        beats = (gj > gate) | ((gj == gate) & (blk_row > j))
                        rank = rank + jnp.where(beats, 1.0, 0.0)
                    keep = (rank < float(MOBA_TOPK)) | (blk_row >= i)
                    bias = jnp.where(keep, 0.0, NEG_BIG)
                    st["q"] = jnp.concatenate([qi_f, bias, bias_pad], axis=0).astype(jnp.bfloat16)
                else:
                    st["q"] = qi_t

            def scores():
                if i > MOBA_TOPK:
                    s = jnp.dot(kaug_ref[c, 0:n_keys, :], st.pop("q"),
                                preferred_element_type=jnp.float32)
                else:
                    s = jnp.dot(k_ref[0:n_keys, lanes], st.pop("q"), preferred_element_type=jnp.float32)
                s_own = jnp.where(causal, s[i * blk:n_keys, :], NEG_BIG)
                sm = s_own if i == 0 else jnp.concatenate([s[0:i * blk, :], s_own], axis=0)
                st["s"] = sm
                st["m"] = jnp.max(sm, axis=0, keepdims=True)

            def probs():
                st["p"] = jnp.exp2(st.pop("s") - st.pop("m")).astype(jnp.bfloat16)

            def values():
                v_i = jnp.concatenate([value_block(j) for j in range(i + 1)], axis=1)
                o_t = jnp.dot(v_i, st.pop("p"), preferred_element_type=jnp.float32)
                o_t = o_t[0:LANES, :] * (1.0 / o_t[LANES:LANES + 1, :])
                o_own = jnp.concatenate([o_t[h * dh:(h + 1) * dh, h * blk:(h + 1) * blk]
                                         for h in range(n_head)], axis=0)
                o_ref[i * blk:(i + 1) * blk, lanes] = o_own.T.astype(o_ref.dtype)

            return [prep, scores, probs, values]

        return [section(j) + section(n_blk - 1 - j) for j in range(n_blk // 2)]

    _interleave([stream for c in range(MOBA_COLS) for stream in column(c)], lag=1)


def _moba(pj, kmean, batch, seq):
    n_blk = seq // MOBA_BLOCK
    width = MOBA_COLS * LANES
    n_col = MOBA_WIDTH // width

    def tok(group):
        return pl.BlockSpec((seq, width), lambda b, c: (b, group * n_col + c))

    return pl.pallas_call(
        functools.partial(_moba_kernel, n_blk=n_blk),
        grid=(batch, n_col),
        in_specs=[tok(0), tok(1), tok(2), pl.BlockSpec((1, n_blk, width), lambda b, c: (b, 0, c))],
        out_specs=tok(0),
        out_shape=jax.ShapeDtypeStruct((batch * seq, MOBA_WIDTH), jnp.bfloat16),
        scratch_shapes=[pltpu.VMEM((MOBA_COLS, seq, 2 * LANES), jnp.bfloat16)],
        compiler_params=pltpu.CompilerParams(
            dimension_semantics=("arbitrary", "arbitrary"), vmem_limit_bytes=VMEM_LIMIT),
        name="moba",
    )(pj, pj, pj, kmean)


def _gla_kernel(q_ref, k_ref, v_ref, la_ref, o_ref, st_ref):
    @pl.when(pl.program_id(1) == 0)
    def _():
        st_ref[...] = jnp.zeros_like(st_ref)

    for r in range(GLA_STEP // GLA_TILE):
        rows = slice(r * GLA_TILE, (r + 1) * GLA_TILE)
        _gla_tile(q_ref[rows, :], k_ref[rows, :], v_ref.at[rows, :], la_ref[rows, :],
                  o_ref.at[rows, :], st_ref)


def _gla_tile(q, k, v_ref, la, o_ref, st_ref):
    c = GLA_TILE
    dk, dv = GLA_KEY_DIM, GLA_VALUE_DIM
    row = lax.broadcasted_iota(jnp.int32, la.shape, 0)
    lane = lax.broadcasted_iota(jnp.int32, la.shape, 1)
    odd_head = (lane & dk) != 0
    st_lane = lax.broadcasted_iota(jnp.int32, (dv, GLA_K_WIDTH), 1)

    b = la
    step = 1
    while step < c:
        b = b + jnp.where(row >= step, pltpu.roll(b, step, 0), 0.0)
        step *= 2

    qf = q.astype(jnp.float32) * (dk ** -0.5)
    kf = k.astype(jnp.float32)

    sub = c // 2
    t_i = lax.broadcasted_iota(jnp.int32, (sub, sub), 0)
    s_i = lax.broadcasted_iota(jnp.int32, (sub, sub), 1)
    level = jnp.where(t_i >= s_i, 31 - lax.clz(t_i ^ s_i), -2)

    def split_heads(x):
        return (jnp.where(odd_head, 0.0, x).astype(jnp.bfloat16),
                jnp.where(odd_head, x, 0.0).astype(jnp.bfloat16))

    def level_products(q_lvl, k_lvl, row_blk, col_blk):
        q_b = q_lvl.astype(jnp.bfloat16)
        k_even, k_odd = split_heads(k_lvl)
        rows = slice(row_blk * sub, (row_blk + 1) * sub)
        keys = slice(col_blk * sub, (col_blk + 1) * sub)
        out = []
        for pair in range(GLA_HEADS // 2):
            cols = slice(pair * LANES, (pair + 1) * LANES)
            k_pair = jnp.concatenate([k_even[keys, cols], k_odd[keys, cols]], axis=0)
            a = lax.dot_general(q_b[rows, cols], k_pair, _NT, preferred_element_type=jnp.float32)
            out += [a[:, 0:sub], a[:, sub:2 * sub]]
        return out

    def add_level(acc, q_lvl, k_lvl, lvl):
        hit = level == lvl
        for r in range(2):
            for h, a in enumerate(level_products(q_lvl, k_lvl, r, r)):
                acc[r][h] = jnp.where(hit, a, acc[r][h])
        return acc

    acc = [[jnp.zeros((sub, sub), jnp.float32) for _ in range(GLA_HEADS)] for _ in range(2)]
    acc = add_level(acc, qf, kf, -1)
    a_cross = None
    b_end = b
    half = 1
    lvl = 0
    while half < c:
        b_before = pltpu.roll(b_end, half, 0)
        q_lvl = qf * jnp.exp2(jnp.minimum(b - b_before, 0.0))
        k_lvl = kf * jnp.exp2(b_end - b)
        if half < sub:
            acc = add_level(acc, q_lvl, k_lvl, lvl)
        else:
            a_cross = level_products(q_lvl, k_lvl, 1, 0)
        upper = (row & half) != 0
        b_end = jnp.where(upper, b_end, pltpu.roll(b_end, c - half, 0))
        half *= 2
        lvl += 1
    q_in = (qf * jnp.exp2(b)).astype(jnp.bfloat16)
    k_out = (kf * jnp.exp2(b_end - b)).astype(jnp.bfloat16)
    tile_decay = jnp.exp2(b_end[0:1, :])

    for h in range(GLA_HEADS):
        cols = slice((h // 2) * LANES, (h // 2 + 1) * LANES)
        v_h = v_ref[:, h * dv:(h + 1) * dv]
        st = st_ref[h]
        o_inter = lax.dot_general(q_in[:, cols], st[:, cols].astype(jnp.bfloat16), _NT,
                                  preferred_element_type=jnp.float32)
        a_lo = acc[0][h].astype(jnp.bfloat16)
        a_hi = jnp.concatenate([a_cross[h], acc[1][h]], axis=1).astype(jnp.bfloat16)
        o_intra = jnp.concatenate(
            [jnp.dot(a_lo, v_h[0:sub, :], preferred_element_type=jnp.float32),
             jnp.dot(a_hi, v_h, preferred_element_type=jnp.float32)], axis=0)
        o_ref[:, h * dv:(h + 1) * dv] = (o_inter + o_intra).astype(o_ref.dtype)
        v_t = v_h.astype(jnp.float32).T.astype(jnp.bfloat16)
        upd = jnp.dot(v_t, k_out, preferred_element_type=jnp.float32)
        own = (st_lane >= h * dk) & (st_lane < (h + 1) * dk)
        st_ref[h] = st * tile_decay + jnp.where(own, upd, 0.0)


def _gla(pj, la, batch, seq):
    n_tile = seq // GLA_STEP

    def tok(width, col0=0):
        return pl.BlockSpec((GLA_STEP, width), lambda b, t: (b * n_tile + t, col0 // width))

    return pl.pallas_call(
        _gla_kernel,
        grid=(batch, n_tile),
        in_specs=[tok(GLA_K_WIDTH, 3 * MOBA_WIDTH), tok(GLA_K_WIDTH, 3 * MOBA_WIDTH + GLA_K_WIDTH),
                  tok(GLA_V_WIDTH, 3 * MOBA_WIDTH + 2 * GLA_K_WIDTH), tok(GLA_K_WIDTH)],
        out_specs=tok(GLA_V_WIDTH),
        out_shape=jax.ShapeDtypeStruct((batch * seq, GLA_V_WIDTH), jnp.bfloat16),
        scratch_shapes=[pltpu.VMEM((GLA_HEADS, GLA_VALUE_DIM, GLA_K_WIDTH), jnp.float32)],
        compiler_params=pltpu.CompilerParams(
            dimension_semantics=("arbitrary", "arbitrary"), vmem_limit_bytes=VMEM_LIMIT),
        name="gla",
    )(pj, pj, pj, la)


def kernel(x, ffn1_pre_g, ffn1_w_gate, ffn1_w_up, ffn1_w_down, ffn1_post_g, mix_pre_g, w_in,
           gla_w_alpha_up, gla_b_alpha, moba_out_g, gla_out_g, w_out, mix_post_g, ffn2_pre_g,
           ffn2_w_gate, ffn2_w_up, ffn2_w_down, ffn2_post_g):
    batch, seq, d = x.shape
    assert d == D_MODEL and seq % MOBA_BLOCK == 0 and seq % GLA_STEP == 0
    assert (batch * seq) % FFN_TM == 0 and (batch * seq) % MIX_TM == 0
    bf = jnp.bfloat16
    xt = x.reshape(batch * seq, d)
    for l in range(w_in.shape[0]):
        wup = jnp.pad(gla_w_alpha_up[l], ((0, GATE_PAD - GLA_GATE_RANK), (0, 0))).astype(bf)
        xt, pj, kmean, la, wo, wg2, wu2, wd2 = _ffn_proj(
            xt, ffn1_pre_g[l][None], ffn1_w_gate[l], ffn1_w_up[l], ffn1_w_down[l], ffn1_post_g[l][None],
            mix_pre_g[l][None], w_in[l].T, wup, gla_b_alpha[l][None],
            (w_out[l], ffn2_w_gate[l], ffn2_w_up[l], ffn2_w_down[l]))

        kmean = kmean.reshape(batch, seq // MOBA_BLOCK, MOBA_WIDTH)
        o_moba = _moba(pj, kmean, batch, seq)
        o_gla = _gla(pj, la, batch, seq)

        xt = _mix_ffn(xt, o_moba, o_gla, pj, moba_out_g[l][None],
                      gla_out_g[l].reshape(1, GLA_V_WIDTH), wo, mix_post_g[l][None],
                      ffn2_pre_g[l][None], wg2, wu2, wd2, ffn2_post_g[l][None])
    return xt.reshape(batch, seq, d)
```

```python
import functools

import jax
import jax.numpy as jnp
from jax import lax
from jax.experimental import pallas as pl
from jax.experimental.pallas import tpu as pltpu

D_MODEL = 1024
D_FF = 2816
RMS_EPS = 1e-6

MOBA_WIDTH = 512
MOBA_HEAD_DIM = 64
MOBA_BLOCK = 256
MOBA_TOPK = 3
MOBA_MIN_DENOM = 2.0 ** -100
MOBA_COLS = 2
LOG2_E = 1.4426950408889634
MOBA_Q_SCALE = MOBA_HEAD_DIM ** -0.5 * LOG2_E

GLA_HEADS = 4
GLA_KEY_DIM = 64
GLA_VALUE_DIM = 128
GLA_K_WIDTH = GLA_HEADS * GLA_KEY_DIM
GLA_V_WIDTH = GLA_HEADS * GLA_VALUE_DIM
GLA_GATE_RANK = 16
GLA_GATE_TAU = 16.0
GLA_TILE = 256
GLA_STEP = 1024

LANES = 128
BF16_SUBLANES = 16
GATE_PAD = LANES
NEG_BIG = -1e30
SUM_ROWS = 16

FFN_TM = 512
FFN_TF = 256
CAST_ROWS = 256
PROJ_MAIN = 3 * MOBA_WIDTH + 2 * GLA_K_WIDTH + GLA_V_WIDTH
PROJ_OUT = PROJ_MAIN + GLA_V_WIDTH
MIX_TM = 1024
MIX_SUB = 256
VMEM_LIMIT = 56 * 1024 * 1024

_NT = (((1,), (1,)), ((), ()))


def _rms(x, g):
    return x * lax.rsqrt(jnp.mean(x * x, axis=-1, keepdims=True) + RMS_EPS) * g


def _silu(x):
    return x / (1.0 + jnp.exp(-x))


def _resident(shape):
    return pl.BlockSpec(shape, lambda *_: (0,) * len(shape), pipeline_mode=pl.Buffered(1))


def _interleave(stage_lists, lag):
    n_slots = max(len(st) + j * lag for j, st in enumerate(stage_lists))
    for k in range(n_slots):
        for j, st in enumerate(stage_lists):
            if 0 <= k - j * lag < len(st):
                st[k - j * lag]()


def _cast_weights(jobs, stage_ref, sem):
    n_slot = stage_ref.shape[0]
    per_job = [[(src, store, row0, k, n_cols, rows) for k in range(n_rows // rows)]
               for src, store, n_rows, row0, n_cols, rows in jobs]
    chunks = [job[k] for k in range(max(map(len, per_job))) for job in per_job if k < len(job)]

    def copy(c):
        src, _, row0, k, n_cols, rows = chunks[c]
        slot = c % n_slot
        return pltpu.make_async_copy(
            src.at[pl.ds(row0 + k * rows, rows), pl.ds(0, n_cols)],
            stage_ref.at[slot, pl.ds(0, rows), pl.ds(0, n_cols)], sem.at[slot])

    for c in range(min(n_slot, len(chunks))):
        copy(c).start()
    for c, (_, store, _, k, n_cols, rows) in enumerate(chunks):
        copy(c).wait()
        store(slice(k * rows, (k + 1) * rows), stage_ref[c % n_slot, 0:rows, 0:n_cols])
        if c + n_slot < len(chunks):
            copy(c + n_slot).start()


def _cast_job(src_hbm, dst_ref):
    def store(rows, chunk):
        dst_ref[rows, :] = chunk.astype(jnp.bfloat16)

    return (src_hbm, store, dst_ref.shape[0], 0, dst_ref.shape[1], CAST_ROWS)


_HBM = pl.BlockSpec(memory_space=pl.ANY)


def _cast_scratch(n_slot):
    return [pltpu.VMEM((n_slot, CAST_ROWS, D_FF), jnp.float32), pltpu.SemaphoreType.DMA((n_slot,))]


def _row(width, rows=FFN_TM):
    return pl.BlockSpec((rows, width), lambda i: (i, 0))


_ROW_PARAMS = dict(
    compiler_params=pltpu.CompilerParams(
        dimension_semantics=("arbitrary",), vmem_limit_bytes=VMEM_LIMIT))


def _ffn_proj_stages(r, x_ref, pre_ref, post_ref, g_ref, wup_ref, ba_ref,
                     x1_ref, pj_ref, kmean_ref, la_ref,
                     h_ref, wg_ref, wu_ref, wd_ref, wmain_ref, wgr_ref, wat_ref):
    rows = slice(r * MOBA_BLOCK, (r + 1) * MOBA_BLOCK)
    st = {}

    def prenorm():
        st["xn"] = _rms(x_ref[rows, :], pre_ref[...]).astype(jnp.bfloat16)

    def up(c):
        def run():
            cols = slice(c * FFN_TF, (c + 1) * FFN_TF)
            g = jnp.dot(st["xn"], wg_ref[:, cols], preferred_element_type=jnp.float32)
            u = jnp.dot(st["xn"], wu_ref[:, cols], preferred_element_type=jnp.float32)
            h_ref[rows, cols] = (_silu(g) * u).astype(jnp.bfloat16)
        return run

    def down():
        st["f"] = jnp.dot(h_ref[rows, :], wd_ref[...], preferred_element_type=jnp.float32)

    def mid():
        x1 = x_ref[rows, :] + 0.5 * _rms(st.pop("f"), post_ref[...])
        x1_ref[rows, :] = x1
        st["h"] = _rms(x1, g_ref[...]).astype(jnp.bfloat16)

    def proj(w_ref, lo, hi):
        return jnp.dot(st["h"], w_ref[:, lo:hi], preferred_element_type=jnp.float32)

    w, kw, vw = MOBA_WIDTH, GLA_K_WIDTH, GLA_V_WIDTH

    def moba_q():
        pj_ref[rows, 0:w] = (proj(wmain_ref, 0, w) * MOBA_Q_SCALE).astype(jnp.bfloat16)

    def moba_k():
        mk = proj(wmain_ref, w, 2 * w)
        pj_ref[rows, w:2 * w] = mk.astype(jnp.bfloat16)
        kmean_ref[r] = jnp.mean(mk, axis=0, keepdims=True)

    def moba_v():
        pj_ref[rows, 2 * w:3 * w] = proj(wmain_ref, 2 * w, 3 * w).astype(jnp.bfloat16)

    def gla_qk():
        pj_ref[rows, 3 * w:3 * w + 2 * kw] = proj(wmain_ref, 3 * w, 3 * w + 2 * kw).astype(jnp.bfloat16)

    def gla_v():
        pj_ref[rows, 3 * w + 2 * kw:PROJ_MAIN] = proj(wmain_ref, 3 * w + 2 * kw, PROJ_MAIN).astype(jnp.bfloat16)

    def gla_gate():
        pj_ref[rows, PROJ_MAIN:PROJ_MAIN + vw] = proj(wgr_ref, 0, vw).astype(jnp.bfloat16)

    def gla_decay():
        ga = lax.dot_general(st["h"], wat_ref[...], _NT, preferred_element_type=jnp.float32)
        z = jnp.dot(ga.astype(jnp.bfloat16), wup_ref[...],
                    preferred_element_type=jnp.float32) + ba_ref[...]
        log_sig = jnp.minimum(z, 0.0) - jnp.log(1.0 + jnp.exp(-jnp.abs(z)))
        la_ref[rows, :] = log_sig * (LOG2_E / GLA_GATE_TAU)

    return ([prenorm] + [up(c) for c in range(D_FF // FFN_TF)] + [down, mid]
            + [moba_q, moba_k, moba_v, gla_qk, gla_v, gla_gate, gla_decay])


def _ffn_proj_kernel(x_ref, pre_ref, wg_hbm, wu_hbm, wd_hbm, post_ref, g_ref, wint_hbm, wup_ref, ba_ref,
                     *rest, n_next):
    next_in, rest = rest[:n_next], rest[n_next:]
    outs, next_out, scratch = rest[:4], rest[4:4 + n_next], rest[4 + n_next:]
    h_ref, wg_ref, wu_ref, wd_ref, wmain_ref, wgr_ref, wat_ref, stage_ref, sem = scratch

    @pl.when(pl.program_id(0) == 0)
    def _():
        def store_main(rows, chunk):
            wmain_ref[:, rows] = chunk.T.astype(jnp.bfloat16)

        def store_gate(rows, chunk):
            wgr_ref[:, rows] = chunk.T.astype(jnp.bfloat16)

        def store_rank(rows, chunk):
            wat_ref[rows, :] = chunk.astype(jnp.bfloat16)

        _cast_weights(
            [_cast_job(wg_hbm, wg_ref),
             (wint_hbm, store_main, PROJ_MAIN, 0, D_MODEL, CAST_ROWS),
             _cast_job(wu_hbm, wu_ref),
             (wint_hbm, store_gate, GLA_V_WIDTH, PROJ_MAIN + GLA_GATE_RANK, D_MODEL, CAST_ROWS),
             _cast_job(wd_hbm, wd_ref),
             (wint_hbm, store_rank, GATE_PAD, PROJ_MAIN, D_MODEL, GATE_PAD)], stage_ref, sem)

    refs = (x_ref, pre_ref, post_ref, g_ref, wup_ref, ba_ref) + outs + (
        h_ref, wg_ref, wu_ref, wd_ref, wmain_ref, wgr_ref, wat_ref)
    _interleave([_ffn_proj_stages(r, *refs) for r in range(FFN_TM // MOBA_BLOCK)], lag=4)
    for src_ref, dst_ref in zip(next_in, next_out):
        dst_ref[...] = src_ref[...].astype(jnp.bfloat16)


def _ffn_proj(x, pre_g, wg, wu, wd, post_g, g, w_in, wup, ba, next_weights):
    t = x.shape[0]
    n_step = t // FFN_TM
    nblk = FFN_TM // MOBA_BLOCK

    def out(width, dtype=jnp.bfloat16):
        return jax.ShapeDtypeStruct((t, width), dtype)

    def slice_spec(w):
        rows = w.shape[0] // n_step
        rep = 1 if rows % BF16_SUBLANES == 0 else 2
        assert (rows * rep) % BF16_SUBLANES == 0 and w.shape[0] % n_step == 0
        return pl.BlockSpec((rows * rep, w.shape[1]), lambda i: (i // rep, 0))

    next_specs = [slice_spec(w) for w in next_weights]
    return pl.pallas_call(
        functools.partial(_ffn_proj_kernel, n_next=len(next_weights)),
        grid=(n_step,),
        in_specs=[_row(D_MODEL), _resident(pre_g.shape), _HBM, _HBM, _HBM, _resident(post_g.shape),
                  _resident(g.shape), _HBM, _resident(wup.shape), _resident(ba.shape)] + next_specs,
        out_specs=[_row(D_MODEL), _row(PROJ_OUT),
                   pl.BlockSpec((nblk, 1, MOBA_WIDTH), lambda i: (i, 0, 0)), _row(GLA_K_WIDTH)] + next_specs,
        out_shape=[out(D_MODEL, jnp.float32), out(PROJ_OUT),
                   jax.ShapeDtypeStruct((t // MOBA_BLOCK, 1, MOBA_WIDTH), jnp.float32),
                   out(GLA_K_WIDTH, jnp.float32)]
        + [jax.ShapeDtypeStruct(w.shape, jnp.bfloat16) for w in next_weights],
        scratch_shapes=[pltpu.VMEM((FFN_TM, D_FF), jnp.bfloat16),
                        pltpu.VMEM((D_MODEL, D_FF), jnp.bfloat16), pltpu.VMEM((D_MODEL, D_FF), jnp.bfloat16),
                        pltpu.VMEM((D_FF, D_MODEL), jnp.bfloat16),
                        pltpu.VMEM((D_MODEL, PROJ_MAIN), jnp.bfloat16),
                        pltpu.VMEM((D_MODEL, GLA_V_WIDTH), jnp.bfloat16),
                        pltpu.VMEM((GATE_PAD, D_MODEL), jnp.bfloat16)] + _cast_scratch(3),
        name="ffn_proj", **_ROW_PARAMS,
    )(x, pre_g, wg, wu, wd, post_g, g, w_in, wup, ba, *next_weights)


def _mix_ffn_stages(rows, x_ref, om_ref, og_ref, gr_ref, mg_ref, gg_ref, mpost_ref, pre_ref, post_ref,
                    o_ref, h_ref, wo_ref, wg_ref, wu_ref, wd_ref):
    st = {}

    def mix_in():
        om = _rms(om_ref[rows, :].astype(jnp.float32), mg_ref[...]).astype(jnp.bfloat16)
        gate = _silu(gr_ref[rows, :].astype(jnp.float32))
        og = og_ref[rows, :].astype(jnp.float32)
        parts = [om]
        for h in range(GLA_HEADS):
            cols = slice(h * GLA_VALUE_DIM, (h + 1) * GLA_VALUE_DIM)
            parts.append((_rms(og[:, cols], gg_ref[:, cols]) * gate[:, cols]).astype(jnp.bfloat16))
        st["mix_in"] = jnp.concatenate(parts, axis=-1)

    def mix_out():
        st["mix"] = jnp.dot(st.pop("mix_in"), wo_ref[...], preferred_element_type=jnp.float32)

    def norms():
        x2 = x_ref[rows, :] + _rms(st.pop("mix"), mpost_ref[...])
        st["x2"] = x2
        st["xn"] = _rms(x2, pre_ref[...]).astype(jnp.bfloat16)

    def up(c):
        def run():
            cols = slice(c * FFN_TF, (c + 1) * FFN_TF)
            g = jnp.dot(st["xn"], wg_ref[:, cols], preferred_element_type=jnp.float32)
            u = jnp.dot(st["xn"], wu_ref[:, cols], preferred_element_type=jnp.float32)
            h_ref[rows, cols] = (_silu(g) * u).astype(jnp.bfloat16)
        return run

    def down():
        st["f"] = jnp.dot(h_ref[rows, :], wd_ref[...], preferred_element_type=jnp.float32)

    def finish():
        o_ref[rows, :] = st.pop("x2") + 0.5 * _rms(st.pop("f"), post_ref[...])

    return [mix_in, mix_out, norms] + [up(c) for c in range(D_FF // FFN_TF)] + [down, finish]


def _mix_ffn_kernel(x_ref, om_ref, og_ref, gr_ref, mg_ref, gg_ref, wo_ref, mpost_ref,
                    pre_ref, wg_ref, wu_ref, wd_ref, post_ref, o_ref, h_ref):
    refs = (x_ref, om_ref, og_ref, gr_ref, mg_ref, gg_ref, mpost_ref, pre_ref, post_ref,
            o_ref, h_ref, wo_ref, wg_ref, wu_ref, wd_ref)
    _interleave([_mix_ffn_stages(slice(r * MIX_SUB, (r + 1) * MIX_SUB), *refs)
                 for r in range(MIX_TM // MIX_SUB)], lag=4)


def _mix_ffn(x, om, og, pj, mg, gg, wo, mpost_g, pre_g, wg, wu, wd, post_g):
    t = x.shape[0]
    params = (mg, gg, wo, mpost_g, pre_g, wg, wu, wd, post_g)
    return pl.pallas_call(
        _mix_ffn_kernel,
        grid=(t // MIX_TM,),
        in_specs=[_row(D_MODEL, MIX_TM), _row(MOBA_WIDTH, MIX_TM), _row(GLA_V_WIDTH, MIX_TM),
                  pl.BlockSpec((MIX_TM, GLA_V_WIDTH), lambda i: (i, PROJ_MAIN // GLA_V_WIDTH))]
        + [_resident(p.shape) for p in params],
        out_specs=_row(D_MODEL, MIX_TM),
        out_shape=jax.ShapeDtypeStruct((t, D_MODEL), jnp.float32),
        scratch_shapes=[pltpu.VMEM((MIX_TM, D_FF), jnp.bfloat16)],
        name="mix_ffn", **_ROW_PARAMS,
    )(x, om, og, pj, *params)


def _moba_kernel(q_ref, k_ref, v_ref, kmean_ref, o_ref, kaug_ref, *, n_blk):
    blk = MOBA_BLOCK
    dh = MOBA_HEAD_DIM
    seq = n_blk * blk
    n_head = LANES // dh
    wide = n_head * blk

    key_blk = lax.broadcasted_iota(jnp.int32, (seq, LANES), 0) // blk
    hot_col = lax.broadcasted_iota(jnp.int32, (seq, LANES), 1)
    one_hot = jnp.where(key_blk == hot_col, 1.0, 0.0).astype(jnp.bfloat16)
    feat = lax.broadcasted_iota(jnp.int32, (LANES, blk), 0)
    head_rows = [(feat >= h * dh) & (feat < (h + 1) * dh) for h in range(n_head)]
    blk_row = lax.broadcasted_iota(jnp.int32, (n_blk, wide), 0)
    key_i = lax.broadcasted_iota(jnp.int32, (blk, wide), 0)
    qry_i = lax.broadcasted_iota(jnp.int32, (blk, wide), 1) % blk
    causal = key_i <= qry_i
    bias_pad = jnp.zeros((LANES - n_blk, wide), jnp.float32)
    ones = jnp.ones((SUM_ROWS, blk), jnp.float32)

    f_i = lax.broadcasted_iota(jnp.int32, (LANES, LANES), 0) // dh
    g_i = lax.broadcasted_iota(jnp.int32, (LANES, LANES), 1) // dh
    same_head = jnp.where(f_i == g_i, 1.0, 0.0).astype(jnp.bfloat16)
    denominators = []

    def column(c, exact):
        lanes = slice(c * LANES, (c + 1) * LANES)
        kmean = kmean_ref[0][:, lanes]
        km_hi = kmean.astype(jnp.bfloat16)
        km_lo = (kmean - km_hi.astype(jnp.float32)).astype(jnp.bfloat16)
        v_t = [None] * n_blk
        if not exact:
            kaug_ref[c, :, 0:LANES] = k_ref[:, lanes]
            kaug_ref[c, :, LANES:2 * LANES] = one_hot
            kf = k_ref[:, lanes].astype(jnp.float32)
            k2 = jnp.dot((kf * kf).astype(jnp.bfloat16), same_head, preferred_element_type=jnp.float32)
            kmax2 = [jnp.max(k2[j * blk:(j + 1) * blk, :], axis=0, keepdims=True) for j in range(n_blk)]
            for j in range(1, n_blk):
                kmax2[j] = jnp.maximum(kmax2[j], kmax2[j - 1])

        def value_block(j):
            if v_t[j] is None:
                v_j = v_ref[j * blk:(j + 1) * blk, lanes].astype(jnp.float32).T
                v_t[j] = jnp.concatenate([v_j, ones], axis=0).astype(jnp.bfloat16)
            return v_t[j]

        def section(i):
            st = {}
            n_keys = (i + 1) * blk

            def prep():
                q_i = q_ref[i * blk:(i + 1) * blk, lanes].astype(jnp.float32).T
                qi_f = jnp.concatenate([jnp.where(rows, q_i, 0.0) for rows in head_rows], axis=1)
                qi_t = qi_f.astype(jnp.bfloat16)
                if not exact:
                    qn2 = jnp.sum(qi_f * qi_f, axis=0, keepdims=True)
                    kb = jnp.concatenate([jnp.broadcast_to(kmax2[i][:, h * dh:h * dh + 1], (1, blk))
                                          for h in range(n_head)], axis=1)
                    st["shift"] = jnp.sqrt(qn2 * kb)
                if i > MOBA_TOPK:
                    gate = (jnp.dot(km_hi, qi_t, preferred_element_type=jnp.float32)
                            + jnp.dot(km_lo, qi_t, preferred_element_type=jnp.float32))
                    rank = jnp.zeros((n_blk, wide), jnp.float32)
                    for j in range(i):
                        gj = gate[j:j + 1, :]
                        beats = (gj > gate) | ((gj == gate) & (blk_row > j))
                        rank = rank + jnp.where(beats, 1.0, 0.0)
                    keep = (rank < float(MOBA_TOPK)) | (blk_row >= i)
                    bias = jnp.where(keep, 0.0, NEG_BIG)
                    st["q"] = jnp.concatenate([qi_f, bias, bias_pad], axis=0).astype(jnp.bfloat16)
                else:
                    st["q"] = qi_t

            def scores():
                if i > MOBA_TOPK:
                    s = jnp.dot(kaug_ref[c, 0:n_keys, :], st.pop("q"),
                                preferred_element_type=jnp.float32)
                else:
                    s = jnp.dot(k_ref[0:n_keys, lanes], st.pop("q"), preferred_element_type=jnp.float32)
                s_own = jnp.where(causal, s[i * blk:n_keys, :], NEG_BIG)
                sm = s_own if i == 0 else jnp.concatenate([s[0:i * blk, :], s_own], axis=0)
                if exact:
                    st["s"] = sm
                    st["shift"] = jnp.max(sm, axis=0, keepdims=True)
                else:
                    st["p"] = jnp.exp2(sm - st.pop("shift")).astype(jnp.bfloat16)

            def probs():
                if exact:
                    st["p"] = jnp.exp2(st.pop("s") - st.pop("shift")).astype(jnp.bfloat16)

            def values():
                v_i = jnp.concatenate([value_block(j) for j in range(i + 1)], axis=1)
                o_t = jnp.dot(v_i, st.pop("p"), preferred_element_type=jnp.float32)
                denom = o_t[LANES:LANES + 1, :]
                if not exact:
                    denominators.append(denom)
                o_t = o_t[0:LANES, :] * (1.0 / denom)
                o_own = jnp.concatenate([o_t[h * dh:(h + 1) * dh, h * blk:(h + 1) * blk]
                                         for h in range(n_head)], axis=0)
                o_ref[i * blk:(i + 1) * blk, lanes] = o_own.T.astype(o_ref.dtype)

            return [prep, scores, probs, values]

        return [section(j) + section(n_blk - 1 - j) for j in range(n_blk // 2)]

    _interleave([stream for c in range(MOBA_COLS) for stream in column(c, exact=False)], lag=1)

    smallest = denominators[0]
    for denom in denominators[1:]:
        smallest = jnp.minimum(smallest, denom)

    @pl.when(jnp.logical_not(jnp.min(smallest) >= MOBA_MIN_DENOM))
    def _():
        for c in range(MOBA_COLS):
            for stream in column(c, exact=True):
                for stage in stream:
                    stage()


def _moba(pj, kmean, batch, seq):
    n_blk = seq // MOBA_BLOCK
    width = MOBA_COLS * LANES
    n_col = MOBA_WIDTH // width

    def tok(group):
        return pl.BlockSpec((seq, width), lambda b, c: (b, group * n_col + c))

    return pl.pallas_call(
        functools.partial(_moba_kernel, n_blk=n_blk),
        grid=(batch, n_col),
        in_specs=[tok(0), tok(1), tok(2), pl.BlockSpec((1, n_blk, width), lambda b, c: (b, 0, c))],
        out_specs=tok(0),
        out_shape=jax.ShapeDtypeStruct((batch * seq, MOBA_WIDTH), jnp.bfloat16),
        scratch_shapes=[pltpu.VMEM((MOBA_COLS, seq, 2 * LANES), jnp.bfloat16)],
        compiler_params=pltpu.CompilerParams(
            dimension_semantics=("arbitrary", "arbitrary"), vmem_limit_bytes=VMEM_LIMIT),
        name="moba",
    )(pj, pj, pj, kmean)


def _gla_kernel(q_ref, k_ref, v_ref, la_ref, o_ref, st_ref):
    @pl.when(pl.program_id(1) == 0)
    def _():
        st_ref[...] = jnp.zeros_like(st_ref)

    for r in range(GLA_STEP // GLA_TILE):
        rows = slice(r * GLA_TILE, (r + 1) * GLA_TILE)
        _gla_tile(q_ref[rows, :], k_ref[rows, :], v_ref.at[rows, :], la_ref[rows, :],
                  o_ref.at[rows, :], st_ref)


def _gla_tile(q, k, v_ref, la, o_ref, st_ref):
    c = GLA_TILE
    dk, dv = GLA_KEY_DIM, GLA_VALUE_DIM
    row = lax.broadcasted_iota(jnp.int32, la.shape, 0)
    lane = lax.broadcasted_iota(jnp.int32, la.shape, 1)
    odd_head = (lane & dk) != 0
    st_lane = lax.broadcasted_iota(jnp.int32, (dv, GLA_K_WIDTH), 1)

    b = la
    step = 1
    while step < c:
        b = b + jnp.where(row >= step, pltpu.roll(b, step, 0), 0.0)
        step *= 2

    qf = q.astype(jnp.float32) * (dk ** -0.5)
    kf = k.astype(jnp.float32)

    sub = c // 2
    t_i = lax.broadcasted_iota(jnp.int32, (sub, sub), 0)
    s_i = lax.broadcasted_iota(jnp.int32, (sub, sub), 1)
    level = jnp.where(t_i >= s_i, 31 - lax.clz(t_i ^ s_i), -2)

    def split_heads(x):
        return (jnp.where(odd_head, 0.0, x).astype(jnp.bfloat16),
                jnp.where(odd_head, x, 0.0).astype(jnp.bfloat16))

    def level_products(q_lvl, k_lvl, row_blk, col_blk):
        q_b = q_lvl.astype(jnp.bfloat16)
        k_even, k_odd = split_heads(k_lvl)
        rows = slice(row_blk * sub, (row_blk + 1) * sub)
        keys = slice(col_blk * sub, (col_blk + 1) * sub)
        out = []
        for pair in range(GLA_HEADS // 2):
            cols = slice(pair * LANES, (pair + 1) * LANES)
            k_pair = jnp.concatenate([k_even[keys, cols], k_odd[keys, cols]], axis=0)
            a = lax.dot_general(q_b[rows, cols], k_pair, _NT, preferred_element_type=jnp.float32)
            out += [a[:, 0:sub], a[:, sub:2 * sub]]
        return out

    def add_level(acc, q_lvl, k_lvl, lvl):
        hit = level == lvl
        for r in range(2):
            for h, a in enumerate(level_products(q_lvl, k_lvl, r, r)):
                acc[r][h] = jnp.where(hit, a, acc[r][h])
        return acc

    acc = [[jnp.zeros((sub, sub), jnp.float32) for _ in range(GLA_HEADS)] for _ in range(2)]
    acc = add_level(acc, qf, kf, -1)
    a_cross = None
    b_end = b
    half = 1
    lvl = 0
    while half < c:
        b_before = pltpu.roll(b_end, half, 0)
        q_lvl = qf * jnp.exp2(jnp.minimum(b - b_before, 0.0))
        k_lvl = kf * jnp.exp2(b_end - b)
        if half < sub:
            acc = add_level(acc, q_lvl, k_lvl, lvl)
        else:
            a_cross = level_products(q_lvl, k_lvl, 1, 0)
        upper = (row & half) != 0
        b_end = jnp.where(upper, b_end, pltpu.roll(b_end, c - half, 0))
        half *= 2
        lvl += 1
    q_in = (qf * jnp.exp2(b)).astype(jnp.bfloat16)
    k_out = (kf * jnp.exp2(b_end - b)).astype(jnp.bfloat16)
    tile_decay = jnp.exp2(b_end[0:1, :])

    for h in range(GLA_HEADS):
        cols = slice((h // 2) * LANES, (h // 2 + 1) * LANES)
        v_h = v_ref[:, h * dv:(h + 1) * dv]
        st = st_ref[h]
        o_inter = lax.dot_general(q_in[:, cols], st[:, cols].astype(jnp.bfloat16), _NT,
                                  preferred_element_type=jnp.float32)
        a_lo = acc[0][h].astype(jnp.bfloat16)
        a_hi = jnp.concatenate([a_cross[h], acc[1][h]], axis=1).astype(jnp.bfloat16)
        o_intra = jnp.concatenate(
            [jnp.dot(a_lo, v_h[0:sub, :], preferred_element_type=jnp.float32),
             jnp.dot(a_hi, v_h, preferred_element_type=jnp.float32)], axis=0)
        o_ref[:, h * dv:(h + 1) * dv] = (o_inter + o_intra).astype(o_ref.dtype)
        v_t = v_h.astype(jnp.float32).T.astype(jnp.bfloat16)
        upd = jnp.dot(v_t, k_out, preferred_element_type=jnp.float32)
        own = (st_lane >= h * dk) & (st_lane < (h + 1) * dk)
        st_ref[h] = st * tile_decay + jnp.where(own, upd, 0.0)


def _gla(pj, la, batch, seq):
    n_tile = seq // GLA_STEP

    def tok(width, col0=0):
        return pl.BlockSpec((GLA_STEP, width), lambda b, t: (b * n_tile + t, col0 // width))

    return pl.pallas_call(
        _gla_kernel,
        grid=(batch, n_tile),
        in_specs=[tok(GLA_K_WIDTH, 3 * MOBA_WIDTH), tok(GLA_K_WIDTH, 3 * MOBA_WIDTH + GLA_K_WIDTH),
                  tok(GLA_V_WIDTH, 3 * MOBA_WIDTH + 2 * GLA_K_WIDTH), tok(GLA_K_WIDTH)],
        out_specs=tok(GLA_V_WIDTH),
        out_shape=jax.ShapeDtypeStruct((batch * seq, GLA_V_WIDTH), jnp.bfloat16),
        scratch_shapes=[pltpu.VMEM((GLA_HEADS, GLA_VALUE_DIM, GLA_K_WIDTH), jnp.float32)],
        compiler_params=pltpu.CompilerParams(
            dimension_semantics=("arbitrary", "arbitrary"), vmem_limit_bytes=VMEM_LIMIT),
        name="gla",
    )(pj, pj, pj, la)


def kernel(x, ffn1_pre_g, ffn1_w_gate, ffn1_w_up, ffn1_w_down, ffn1_post_g, mix_pre_g, w_in,
           gla_w_alpha_up, gla_b_alpha, moba_out_g, gla_out_g, w_out, mix_post_g, ffn2_pre_g,
           ffn2_w_gate, ffn2_w_up, ffn2_w_down, ffn2_post_g):
    batch, seq, d = x.shape
    assert d == D_MODEL and seq % MOBA_BLOCK == 0 and seq % GLA_STEP == 0
    assert (batch * seq) % FFN_TM == 0 and (batch * seq) % MIX_TM == 0
    bf = jnp.bfloat16
    xt = x.reshape(batch * seq, d)
    for l in range(w_in.shape[0]):
        wup = jnp.pad(gla_w_alpha_up[l], ((0, GATE_PAD - GLA_GATE_RANK), (0, 0))).astype(bf)
        xt, pj, kmean, la, wo, wg2, wu2, wd2 = _ffn_proj(
            xt, ffn1_pre_g[l][None], ffn1_w_gate[l], ffn1_w_up[l], ffn1_w_down[l], ffn1_post_g[l][None],
            mix_pre_g[l][None], w_in[l].T, wup, gla_b_alpha[l][None],
            (w_out[l], ffn2_w_gate[l], ffn2_w_up[l], ffn2_w_down[l]))

        kmean = kmean.reshape(batch, seq // MOBA_BLOCK, MOBA_WIDTH)
        o_moba = _moba(pj, kmean, batch, seq)
        o_gla = _gla(pj, la, batch, seq)

        xt = _mix_ffn(xt, o_moba, o_gla, pj, moba_out_g[l][None],
                      gla_out_g[l].reshape(1, GLA_V_WIDTH), wo, mix_post_g[l][None],
                      ffn2_pre_g[l][None], wg2, wu2, wd2, ffn2_post_g[l][None])
    return xt.reshape(batch, seq, d)
```

```python
import functools

import jax
import jax.numpy as jnp
from jax import lax
from jax.experimental import pallas as pl
from jax.experimental.pallas import tpu as pltpu

D_MODEL = 1024
D_FF = 2816
RMS_EPS = 1e-6

MOBA_WIDTH = 512
MOBA_HEAD_DIM = 64
MOBA_BLOCK = 256
MOBA_TOPK = 3
MOBA_MIN_DENOM = 2.0 ** -100
MOBA_COLS = 2
LOG2_E = 1.4426950408889634
MOBA_Q_SCALE = MOBA_HEAD_DIM ** -0.5 * LOG2_E

GLA_HEADS = 4
GLA_KEY_DIM = 64
GLA_VALUE_DIM = 128
GLA_K_WIDTH = GLA_HEADS * GLA_KEY_DIM
GLA_V_WIDTH = GLA_HEADS * GLA_VALUE_DIM
GLA_GATE_RANK = 16
GLA_GATE_TAU = 16.0
GLA_TILE = 256
GLA_STEP = 1024

LANES = 128
BF16_SUBLANES = 16
GATE_PAD = LANES
NEG_BIG = -1e30
SUM_ROWS = 16

FFN_TM = 512
FFN_TF = 256
CAST_ROWS = 256
PROJ_MAIN = 3 * MOBA_WIDTH + 2 * GLA_K_WIDTH + GLA_V_WIDTH
PROJ_OUT = PROJ_MAIN + GLA_V_WIDTH
MIX_TM = 1024
MIX_SUB = 256
VMEM_LIMIT = 56 * 1024 * 1024

_NT = (((1,), (1,)), ((), ()))


def _rms(x, g):
    return x * lax.rsqrt(jnp.mean(x * x, axis=-1, keepdims=True) + RMS_EPS) * g


def _silu(x):
    return x / (1.0 + jnp.exp(-x))


def _resident(shape):
    return pl.BlockSpec(shape, lambda *_: (0,) * len(shape), pipeline_mode=pl.Buffered(1))


def _interleave(stage_lists, lag):
    n_slots = max(len(st) + j * lag for j, st in enumerate(stage_lists))
    for k in range(n_slots):
        for j, st in enumerate(stage_lists):
            if 0 <= k - j * lag < len(st):
                st[k - j * lag]()


def _cast_weights(jobs, stage_ref, sem):
    n_slot = stage_ref.shape[0]
    per_job = [[(src, store, row0, k, n_cols, rows) for k in range(n_rows // rows)]
               for src, store, n_rows, row0, n_cols, rows in jobs]
    chunks = [job[k] for k in range(max(map(len, per_job))) for job in per_job if k < len(job)]

    def copy(c):
        src, _, row0, k, n_cols, rows = chunks[c]
        slot = c % n_slot
        return pltpu.make_async_copy(
            src.at[pl.ds(row0 + k * rows, rows), pl.ds(0, n_cols)],
            stage_ref.at[slot, pl.ds(0, rows), pl.ds(0, n_cols)], sem.at[slot])

    for c in range(min(n_slot, len(chunks))):
        copy(c).start()
    for c, (_, store, _, k, n_cols, rows) in enumerate(chunks):
        copy(c).wait()
        store(slice(k * rows, (k + 1) * rows), stage_ref[c % n_slot, 0:rows, 0:n_cols])
        if c + n_slot < len(chunks):
            copy(c + n_slot).start()


def _cast_job(src_hbm, dst_ref):
    def store(rows, chunk):
        dst_ref[rows, :] = chunk.astype(jnp.bfloat16)

    return (src_hbm, store, dst_ref.shape[0], 0, dst_ref.shape[1], CAST_ROWS)


_HBM = pl.BlockSpec(memory_space=pl.ANY)


def _cast_scratch(n_slot):
    return [pltpu.VMEM((n_slot, CAST_ROWS, D_FF), jnp.float32), pltpu.SemaphoreType.DMA((n_slot,))]


def _row(width, rows=FFN_TM):
    return pl.BlockSpec((rows, width), lambda i: (i, 0))


_ROW_PARAMS = dict(
    compiler_params=pltpu.CompilerParams(
        dimension_semantics=("arbitrary",), vmem_limit_bytes=VMEM_LIMIT))


def _ffn_proj_stages(r, x_ref, pre_ref, post_ref, g_ref, wup_ref, ba_ref,
                     x1_ref, pj_ref, kmean_ref, la_ref,
                     h_ref, wg_ref, wu_ref, wd_ref, wmain_ref, wgr_ref, wat_ref):
    rows = slice(r * MOBA_BLOCK, (r + 1) * MOBA_BLOCK)
    st = {}

    def prenorm():
        st["xn"] = _rms(x_ref[rows, :], pre_ref[...]).astype(jnp.bfloat16)

    def up(c):
        def run():
            cols = slice(c * FFN_TF, (c + 1) * FFN_TF)
            g = jnp.dot(st["xn"], wg_ref[:, cols], preferred_element_type=jnp.float32)
            u = jnp.dot(st["xn"], wu_ref[:, cols], preferred_element_type=jnp.float32)
            h_ref[rows, cols] = (_silu(g) * u).astype(jnp.bfloat16)
        return run

    def down():
        st["f"] = jnp.dot(h_ref[rows, :], wd_ref[...], preferred_element_type=jnp.float32)

    def mid():
        x1 = x_ref[rows, :] + 0.5 * _rms(st.pop("f"), post_ref[...])
        x1_ref[rows, :] = x1
        st["h"] = _rms(x1, g_ref[...]).astype(jnp.bfloat16)

    def proj(w_ref, lo, hi):
        return jnp.dot(st["h"], w_ref[:, lo:hi], preferred_element_type=jnp.float32)

    w, kw, vw = MOBA_WIDTH, GLA_K_WIDTH, GLA_V_WIDTH

    def moba_q():
        pj_ref[rows, 0:w] = (proj(wmain_ref, 0, w) * MOBA_Q_SCALE).astype(jnp.bfloat16)

    def moba_k():
        mk = proj(wmain_ref, w, 2 * w)
        pj_ref[rows, w:2 * w] = mk.astype(jnp.bfloat16)
        kmean_ref[r] = jnp.mean(mk, axis=0, keepdims=True)

    def moba_v():
        pj_ref[rows, 2 * w:3 * w] = proj(wmain_ref, 2 * w, 3 * w).astype(jnp.bfloat16)

    def gla_qk():
        pj_ref[rows, 3 * w:3 * w + 2 * kw] = proj(wmain_ref, 3 * w, 3 * w + 2 * kw).astype(jnp.bfloat16)

    def gla_v():
        pj_ref[rows, 3 * w + 2 * kw:PROJ_MAIN] = proj(wmain_ref, 3 * w + 2 * kw, PROJ_MAIN).astype(jnp.bfloat16)

    def gla_gate():
        pj_ref[rows, PROJ_MAIN:PROJ_MAIN + vw] = proj(wgr_ref, 0, vw).astype(jnp.bfloat16)

    def gla_decay():
        ga = lax.dot_general(st["h"], wat_ref[...], _NT, preferred_element_type=jnp.float32)
        z = jnp.dot(ga.astype(jnp.bfloat16), wup_ref[...],
                    preferred_element_type=jnp.float32) + ba_ref[...]
        log_sig = jnp.minimum(z, 0.0) - jnp.log(1.0 + jnp.exp(-jnp.abs(z)))
        la_ref[rows, :] = log_sig * (LOG2_E / GLA_GATE_TAU)

    return ([prenorm] + [up(c) for c in range(D_FF // FFN_TF)] + [down, mid]
            + [moba_q, moba_k, moba_v, gla_qk, gla_v, gla_gate, gla_decay])


def _ffn_proj_kernel(x_ref, pre_ref, wg_hbm, wu_hbm, wd_hbm, post_ref, g_ref, wint_hbm, wup_ref, ba_ref,
                     *rest, n_next):
    next_in, rest = rest[:n_next], rest[n_next:]
    outs, next_out, scratch = rest[:4], rest[4:4 + n_next], rest[4 + n_next:]
    h_ref, wg_ref, wu_ref, wd_ref, wmain_ref, wgr_ref, wat_ref, stage_ref, sem = scratch

    @pl.when(pl.program_id(0) == 0)
    def _():
        def store_main(rows, chunk):
            wmain_ref[:, rows] = chunk.T.astype(jnp.bfloat16)

        def store_gate(rows, chunk):
            wgr_ref[:, rows] = chunk.T.astype(jnp.bfloat16)

        def store_rank(rows, chunk):
            wat_ref[rows, :] = chunk.astype(jnp.bfloat16)

        _cast_weights(
            [_cast_job(wg_hbm, wg_ref),
             (wint_hbm, store_main, PROJ_MAIN, 0, D_MODEL, CAST_ROWS),
             _cast_job(wu_hbm, wu_ref),
             (wint_hbm, store_gate, GLA_V_WIDTH, PROJ_MAIN + GLA_GATE_RANK, D_MODEL, CAST_ROWS),
             _cast_job(wd_hbm, wd_ref),
             (wint_hbm, store_rank, GATE_PAD, PROJ_MAIN, D_MODEL, GATE_PAD)], stage_ref, sem)

    refs = (x_ref, pre_ref, post_ref, g_ref, wup_ref, ba_ref) + outs + (
        h_ref, wg_ref, wu_ref, wd_ref, wmain_ref, wgr_ref, wat_ref)
    _interleave([_ffn_proj_stages(r, *refs) for r in range(FFN_TM // MOBA_BLOCK)], lag=4)
    for src_ref, dst_ref in zip(next_in, next_out):
        dst_ref[...] = src_ref[...].astype(jnp.bfloat16)


def _ffn_proj(x, pre_g, wg, wu, wd, post_g, g, w_in, wup, ba, next_weights):
    t = x.shape[0]
    n_step = t // FFN_TM
    nblk = FFN_TM // MOBA_BLOCK

    def out(width, dtype=jnp.bfloat16):
        return jax.ShapeDtypeStruct((t, width), dtype)

    def slice_spec(w):
        rows = w.shape[0] // n_step
        rep = 1 if rows % BF16_SUBLANES == 0 else 2
        assert (rows * rep) % BF16_SUBLANES == 0 and w.shape[0] % n_step == 0
        return pl.BlockSpec((rows * rep, w.shape[1]), lambda i: (i // rep, 0))

    next_specs = [slice_spec(w) for w in next_weights]
    return pl.pallas_call(
        functools.partial(_ffn_proj_kernel, n_next=len(next_weights)),
        grid=(n_step,),
        in_specs=[_row(D_MODEL), _resident(pre_g.shape), _HBM, _HBM, _HBM, _resident(post_g.shape),
                  _resident(g.shape), _HBM, _resident(wup.shape), _resident(ba.shape)] + next_specs,
        out_specs=[_row(D_MODEL), _row(PROJ_OUT),
                   pl.BlockSpec((nblk, 1, MOBA_WIDTH), lambda i: (i, 0, 0)), _row(GLA_K_WIDTH)] + next_specs,
        out_shape=[out(D_MODEL, jnp.float32), out(PROJ_OUT),
                   jax.ShapeDtypeStruct((t // MOBA_BLOCK, 1, MOBA_WIDTH), jnp.float32),
                   out(GLA_K_WIDTH, jnp.float32)]
        + [jax.ShapeDtypeStruct(w.shape, jnp.bfloat16) for w in next_weights],
        scratch_shapes=[pltpu.VMEM((FFN_TM, D_FF), jnp.bfloat16),
                        pltpu.VMEM((D_MODEL, D_FF), jnp.bfloat16), pltpu.VMEM((D_MODEL, D_FF), jnp.bfloat16),
                        pltpu.VMEM((D_FF, D_MODEL), jnp.bfloat16),
                        pltpu.VMEM((D_MODEL, PROJ_MAIN), jnp.bfloat16),
                        pltpu.VMEM((D_MODEL, GLA_V_WIDTH), jnp.bfloat16),
                        pltpu.VMEM((GATE_PAD, D_MODEL), jnp.bfloat16)] + _cast_scratch(3),
        name="ffn_proj", **_ROW_PARAMS,
    )(x, pre_g, wg, wu, wd, post_g, g, w_in, wup, ba, *next_weights)


def _mix_ffn_stages(rows, x_ref, om_ref, og_ref, gr_ref, mg_ref, gg_ref, mpost_ref, pre_ref, post_ref,
                    o_ref, h_ref, wo_ref, wg_ref, wu_ref, wd_ref):
    st = {}

    def mix_in():
        om = _rms(om_ref[rows, :].astype(jnp.float32), mg_ref[...]).astype(jnp.bfloat16)
        gate = _silu(gr_ref[rows, :].astype(jnp.float32))
        og = og_ref[rows, :].astype(jnp.float32)
        parts = [om]
        for h in range(GLA_HEADS):
            cols = slice(h * GLA_VALUE_DIM, (h + 1) * GLA_VALUE_DIM)
            parts.append((_rms(og[:, cols], gg_ref[:, cols]) * gate[:, cols]).astype(jnp.bfloat16))
        st["mix_in"] = jnp.concatenate(parts, axis=-1)

    def mix_out():
        st["mix"] = jnp.dot(st.pop("mix_in"), wo_ref[...], preferred_element_type=jnp.float32)

    def norms():
        x2 = x_ref[rows, :] + _rms(st.pop("mix"), mpost_ref[...])
        st["x2"] = x2
        st["xn"] = _rms(x2, pre_ref[...]).astype(jnp.bfloat16)

    def up(c):
        def run():
            cols = slice(c * FFN_TF, (c + 1) * FFN_TF)
            g = jnp.dot(st["xn"], wg_ref[:, cols], preferred_element_type=jnp.float32)
            u = jnp.dot(st["xn"], wu_ref[:, cols], preferred_element_type=jnp.float32)
            h_ref[rows, cols] = (_silu(g) * u).astype(jnp.bfloat16)
        return run

    def down():
        st["f"] = jnp.dot(h_ref[rows, :], wd_ref[...], preferred_element_type=jnp.float32)

    def finish():
        o_ref[rows, :] = st.pop("x2") + 0.5 * _rms(st.pop("f"), post_ref[...])

    return [mix_in, mix_out, norms] + [up(c) for c in range(D_FF // FFN_TF)] + [down, finish]


def _mix_ffn_kernel(x_ref, om_ref, og_ref, gr_ref, mg_ref, gg_ref, wo_ref, mpost_ref,
                    pre_ref, wg_ref, wu_ref, wd_ref, post_ref, o_ref, h_ref):
    refs = (x_ref, om_ref, og_ref, gr_ref, mg_ref, gg_ref, mpost_ref, pre_ref, post_ref,
            o_ref, h_ref, wo_ref, wg_ref, wu_ref, wd_ref)
    _interleave([_mix_ffn_stages(slice(r * MIX_SUB, (r + 1) * MIX_SUB), *refs)
                 for r in range(MIX_TM // MIX_SUB)], lag=4)


def _mix_ffn(x, om, og, pj, mg, gg, wo, mpost_g, pre_g, wg, wu, wd, post_g):
    t = x.shape[0]
    params = (mg, gg, wo, mpost_g, pre_g, wg, wu, wd, post_g)
    return pl.pallas_call(
        _mix_ffn_kernel,
        grid=(t // MIX_TM,),
        in_specs=[_row(D_MODEL, MIX_TM), _row(MOBA_WIDTH, MIX_TM), _row(GLA_V_WIDTH, MIX_TM),
                  pl.BlockSpec((MIX_TM, GLA_V_WIDTH), lambda i: (i, PROJ_MAIN // GLA_V_WIDTH))]
        + [_resident(p.shape) for p in params],
        out_specs=_row(D_MODEL, MIX_TM),
        out_shape=jax.ShapeDtypeStruct((t, D_MODEL), jnp.float32),
        scratch_shapes=[pltpu.VMEM((MIX_TM, D_FF), jnp.bfloat16)],
        name="mix_ffn", **_ROW_PARAMS,
    )(x, om, og, pj, *params)


def _moba_kernel(q_ref, k_ref, v_ref, kmean_ref, o_ref, kaug_ref, *, n_blk):
    blk = MOBA_BLOCK
    dh = MOBA_HEAD_DIM
    seq = n_blk * blk
    n_head = LANES // dh
    wide = n_head * blk

    key_blk = lax.broadcasted_iota(jnp.int32, (seq, LANES), 0) // blk
    hot_col = lax.broadcasted_iota(jnp.int32, (seq, LANES), 1)
    one_hot = jnp.where(key_blk == hot_col, 1.0, 0.0).astype(jnp.bfloat16)
    feat = lax.broadcasted_iota(jnp.int32, (LANES, blk), 0)
    head_rows = [(feat >= h * dh) & (feat < (h + 1) * dh) for h in range(n_head)]
    blk_row = lax.broadcasted_iota(jnp.int32, (n_blk, wide), 0)
    key_i = lax.broadcasted_iota(jnp.int32, (blk, wide), 0)
    qry_i = lax.broadcasted_iota(jnp.int32, (blk, wide), 1) % blk
    causal = key_i <= qry_i
    bias_pad = jnp.zeros((LANES - n_blk, wide), jnp.float32)
    ones = jnp.ones((SUM_ROWS, blk), jnp.float32)

    f_i = lax.broadcasted_iota(jnp.int32, (LANES, LANES), 0) // dh
    g_i = lax.broadcasted_iota(jnp.int32, (LANES, LANES), 1) // dh
    same_head = jnp.where(f_i == g_i, 1.0, 0.0).astype(jnp.bfloat16)
    denominators = []

    def column(c, exact):
        lanes = slice(c * LANES, (c + 1) * LANES)
        kmean = kmean_ref[0][:, lanes]
        km_hi = kmean.astype(jnp.bfloat16)
        km_lo = (kmean - km_hi.astype(jnp.float32)).astype(jnp.bfloat16)
        v_t = [None] * n_blk
        if not exact:
            kaug_ref[c, :, 0:LANES] = k_ref[:, lanes]
            kaug_ref[c, :, LANES:2 * LANES] = one_hot
            kf = k_ref[:, lanes].astype(jnp.float32)
            k2 = jnp.dot((kf * kf).astype(jnp.bfloat16), same_head, preferred_element_type=jnp.float32)
            kmax2 = [jnp.max(k2[j * blk:(j + 1) * blk, :], axis=0, keepdims=True) for j in range(n_blk)]
            for j in range(1, n_blk):
                kmax2[j] = jnp.maximum(kmax2[j], kmax2[j - 1])

        def value_block(j):
            if v_t[j] is None:
                v_j = v_ref[j * blk:(j + 1) * blk, lanes].astype(jnp.float32).T
                v_t[j] = jnp.concatenate([v_j, ones], axis=0).astype(jnp.bfloat16)
            return v_t[j]

        def section(i):
            st = {}
            n_keys = (i + 1) * blk

            def prep():
                q_i = q_ref[i * blk:(i + 1) * blk, lanes].astype(jnp.float32).T
                qi_f = jnp.concatenate([jnp.where(rows, q_i, 0.0) for rows in head_rows], axis=1)
                qi_t = qi_f.astype(jnp.bfloat16)
                if not exact:
                    qn2 = jnp.sum(qi_f * qi_f, axis=0, keepdims=True)
                    kb = jnp.concatenate([jnp.broadcast_to(kmax2[i][:, h * dh:h * dh + 1], (1, blk))
                                          for h in range(n_head)], axis=1)
                    st["shift"] = jnp.sqrt(qn2 * kb)
                if i > MOBA_TOPK:
                    gate = (jnp.dot(km_hi, qi_t, preferred_element_type=jnp.float32)
                            + jnp.dot(km_lo, qi_t, preferred_element_type=jnp.float32))
                    rank = jnp.zeros((n_blk, wide), jnp.float32)
                    for j in range(i):
                        gj = gate[j:j + 1, :]
                        beats = (gj > gate) | ((gj == gate) & (blk_row > j))
                        rank = rank + jnp.where(beats, 1.0, 0.0)
                    keep = (rank < float(MOBA_TOPK)) | (blk_row >= i)
                    bias = jnp.where(keep, 0.0, NEG_BIG)
                    st["q"] = jnp.concatenate([qi_f, bias, bias_pad], axis=0).astype(jnp.bfloat16)
                else:
                    st["q"] = qi_t

            def scores():
                if i > MOBA_TOPK:
                    s = jnp.dot(kaug_ref[c, 0:n_keys, :], st.pop("q"),
                                preferred_element_type=jnp.float32)
                else:
                    s = jnp.dot(k_ref[0:n_keys, lanes], st.pop("q"), preferred_element_type=jnp.float32)
                s_own = jnp.where(causal, s[i * blk:n_keys, :], NEG_BIG)
                sm = s_own if i == 0 else jnp.concatenate([s[0:i * blk, :], s_own], axis=0)
                if exact:
                    st["s"] = sm
                    st["shift"] = jnp.max(sm, axis=0, keepdims=True)
                else:
                    st["p"] = jnp.exp2(sm - st.pop("shift")).astype(jnp.bfloat16)

            def probs():
                if exact:
                    st["p"] = jnp.exp2(st.pop("s") - st.pop("shift")).astype(jnp.bfloat16)

            def values():
                v_i = jnp.concatenate([value_block(j) for j in range(i + 1)], axis=1)
                o_t = jnp.dot(v_i, st.pop("p"), preferred_element_type=jnp.float32)
                denom = o_t[LANES:LANES + 1, :]
                if not exact:
                    denominators.append(denom)
                o_t = o_t[0:LANES, :] * (1.0 / denom)
                o_own = jnp.concatenate([o_t[h * dh:(h + 1) * dh, h * blk:(h + 1) * blk]
                                         for h in range(n_head)], axis=0)
                o_ref[i * blk:(i + 1) * blk, lanes] = o_own.T.astype(o_ref.dtype)

            return [prep, scores, probs, values]

        return [section(j) + section(n_blk - 1 - j) for j in range(n_blk // 2)]

    _interleave([stream for c in range(MOBA_COLS) for stream in column(c, exact=False)], lag=1)

    smallest = denominators[0]
    for denom in denominators[1:]:
        smallest = jnp.minimum(smallest, denom)

    @pl.when(jnp.logical_not(jnp.min(smallest) >= MOBA_MIN_DENOM))
    def _():
        for c in range(MOBA_COLS):
            for stream in column(c, exact=True):
                for stage in stream:
                    stage()


def _moba(pj, kmean, batch, seq):
    n_blk = seq // MOBA_BLOCK
    width = MOBA_COLS * LANES
    n_col = MOBA_WIDTH // width

    def tok(group):
        return pl.BlockSpec((seq, width), lambda b, c: (b, group * n_col + c))

    return pl.pallas_call(
        functools.partial(_moba_kernel, n_blk=n_blk),
        grid=(batch, n_col),
        in_specs=[tok(0), tok(1), tok(2), pl.BlockSpec((1, n_blk, width), lambda b, c: (b, 0, c))],
        out_specs=tok(0),
        out_shape=jax.ShapeDtypeStruct((batch * seq, MOBA_WIDTH), jnp.bfloat16),
        scratch_shapes=[pltpu.VMEM((MOBA_COLS, seq, 2 * LANES), jnp.bfloat16)],
        compiler_params=pltpu.CompilerParams(
            dimension_semantics=("arbitrary", "arbitrary"), vmem_limit_bytes=VMEM_LIMIT),
        name="moba",
    )(pj, pj, pj, kmean)


def _gla_kernel(q_ref, k_ref, v_ref, la_ref, o_ref, st_ref):
    @pl.when(pl.program_id(1) == 0)
    def _():
        st_ref[...] = jnp.zeros_like(st_ref)

    for r in range(GLA_STEP // GLA_TILE):
        rows = slice(r * GLA_TILE, (r + 1) * GLA_TILE)
        _gla_tile(q_ref[rows, :], k_ref[rows, :], v_ref.at[rows, :], la_ref[rows, :],
                  o_ref.at[rows, :], st_ref)


def _gla_tile(q, k, v_ref, la, o_ref, st_ref):
    c = GLA_TILE
    dk, dv = GLA_KEY_DIM, GLA_VALUE_DIM
    row = lax.broadcasted_iota(jnp.int32, la.shape, 0)
    lane = lax.broadcasted_iota(jnp.int32, la.shape, 1)
    odd_head = (lane & dk) != 0
    st_lane = lax.broadcasted_iota(jnp.int32, (dv, GLA_K_WIDTH), 1)

    b = la
    step = 1
    while step < c:
        b = b + jnp.where(row >= step, pltpu.roll(b, step, 0), 0.0)
        step *= 2

    qf = q.astype(jnp.float32) * (dk ** -0.5)
    kf = k.astype(jnp.float32)
    qb, kb = qf.astype(jnp.bfloat16), k

    sub = c // 2
    t_i = lax.broadcasted_iota(jnp.int32, (sub, sub), 0)
    s_i = lax.broadcasted_iota(jnp.int32, (sub, sub), 1)
    level = jnp.where(t_i >= s_i, 31 - lax.clz(t_i ^ s_i), -2)

    def split_heads(x):
        x = x.astype(jnp.bfloat16)
        zero = jnp.zeros_like(x)
        return jnp.where(odd_head, zero, x), jnp.where(odd_head, x, zero)

    def level_products(q_lvl, k_lvl, row_blk, col_blk):
        q_b = q_lvl.astype(jnp.bfloat16)
        k_even, k_odd = split_heads(k_lvl)
        rows = slice(row_blk * sub, (row_blk + 1) * sub)
        keys = slice(col_blk * sub, (col_blk + 1) * sub)
        out = []
        for pair in range(GLA_HEADS // 2):
            cols = slice(pair * LANES, (pair + 1) * LANES)
            k_pair = jnp.concatenate([k_even[keys, cols], k_odd[keys, cols]], axis=0)
            a = lax.dot_general(q_b[rows, cols], k_pair, _NT, preferred_element_type=jnp.float32)
            out += [a[:, 0:sub], a[:, sub:2 * sub]]
        return out

    def add_level(acc, q_lvl, k_lvl, lvl):
        hit = level == lvl
        for r in range(2):
            for h, a in enumerate(level_products(q_lvl, k_lvl, r, r)):
                acc[r][h] = jnp.where(hit, a, acc[r][h])
        return acc

    acc = [[jnp.zeros((sub, sub), jnp.float32) for _ in range(GLA_HEADS)] for _ in range(2)]
    acc = add_level(acc, qf, kf, -1)
    a_cross = None
    b_end = b
    half = 1
    lvl = 0
    while half < c:
        b_before = pltpu.roll(b_end, half, 0)
        q_lvl = qb * jnp.exp2(jnp.minimum(b - b_before, 0.0)).astype(jnp.bfloat16)
        k_lvl = kb * jnp.exp2(b_end - b).astype(jnp.bfloat16)
        if half < sub:
            acc = add_level(acc, q_lvl, k_lvl, lvl)
        else:
            a_cross = level_products(q_lvl, k_lvl, 1, 0)
        upper = (row & half) != 0
        b_end = jnp.where(upper, b_end, pltpu.roll(b_end, c - half, 0))
        half *= 2
        lvl += 1
    q_in = (qf * jnp.exp2(b)).astype(jnp.bfloat16)
    k_out = (kf * jnp.exp2(b_end - b)).astype(jnp.bfloat16)
    tile_decay = jnp.exp2(b_end[0:1, :])

    for h in range(GLA_HEADS):
        cols = slice((h // 2) * LANES, (h // 2 + 1) * LANES)
        v_h = v_ref[:, h * dv:(h + 1) * dv]
        st = st_ref[h]
        o_inter = lax.dot_general(q_in[:, cols], st[:, cols].astype(jnp.bfloat16), _NT,
                                  preferred_element_type=jnp.float32)
        a_lo = acc[0][h].astype(jnp.bfloat16)
        a_hi = jnp.concatenate([a_cross[h], acc[1][h]], axis=1).astype(jnp.bfloat16)
        o_intra = jnp.concatenate(
            [jnp.dot(a_lo, v_h[0:sub, :], preferred_element_type=jnp.float32),
             jnp.dot(a_hi, v_h, preferred_element_type=jnp.float32)], axis=0)
        o_ref[:, h * dv:(h + 1) * dv] = (o_inter + o_intra).astype(o_ref.dtype)
        v_t = v_h.astype(jnp.float32).T.astype(jnp.bfloat16)
        upd = jnp.dot(v_t, k_out, preferred_element_type=jnp.float32)
        own = (st_lane >= h * dk) & (st_lane < (h + 1) * dk)
        st_ref[h] = st * tile_decay + jnp.where(own, upd, 0.0)


def _gla(pj, la, batch, seq):
    n_tile = seq // GLA_STEP

    def tok(width, col0=0):
        return pl.BlockSpec((GLA_STEP, width), lambda b, t: (b * n_tile + t, col0 // width))

    return pl.pallas_call(
        _gla_kernel,
        grid=(batch, n_tile),
        in_specs=[tok(GLA_K_WIDTH, 3 * MOBA_WIDTH), tok(GLA_K_WIDTH, 3 * MOBA_WIDTH + GLA_K_WIDTH),
                  tok(GLA_V_WIDTH, 3 * MOBA_WIDTH + 2 * GLA_K_WIDTH), tok(GLA_K_WIDTH)],
        out_specs=tok(GLA_V_WIDTH),
        out_shape=jax.ShapeDtypeStruct((batch * seq, GLA_V_WIDTH), jnp.bfloat16),
        scratch_shapes=[pltpu.VMEM((GLA_HEADS, GLA_VALUE_DIM, GLA_K_WIDTH), jnp.float32)],
        compiler_params=pltpu.CompilerParams(
            dimension_semantics=("arbitrary", "arbitrary"), vmem_limit_bytes=VMEM_LIMIT),
        name="gla",
    )(pj, pj, pj, la)


def kernel(x, ffn1_pre_g, ffn1_w_gate, ffn1_w_up, ffn1_w_down, ffn1_post_g, mix_pre_g, w_in,
           gla_w_alpha_up, gla_b_alpha, moba_out_g, gla_out_g, w_out, mix_post_g, ffn2_pre_g,
           ffn2_w_gate, ffn2_w_up, ffn2_w_down, ffn2_post_g):
    batch, seq, d = x.shape
    assert d == D_MODEL and seq % MOBA_BLOCK == 0 and seq % GLA_STEP == 0
    assert (batch * seq) % FFN_TM == 0 and (batch * seq) % MIX_TM == 0
    bf = jnp.bfloat16
    xt = x.reshape(batch * seq, d)
    for l in range(w_in.shape[0]):
        wup = jnp.pad(gla_w_alpha_up[l], ((0, GATE_PAD - GLA_GATE_RANK), (0, 0))).astype(bf)
        xt, pj, kmean, la, wo, wg2, wu2, wd2 = _ffn_proj(
            xt, ffn1_pre_g[l][None], ffn1_w_gate[l], ffn1_w_up[l], ffn1_w_down[l], ffn1_post_g[l][None],
            mix_pre_g[l][None], w_in[l].T, wup, gla_b_alpha[l][None],
            (w_out[l], ffn2_w_gate[l], ffn2_w_up[l], ffn2_w_down[l]))

        kmean = kmean.reshape(batch, seq // MOBA_BLOCK, MOBA_WIDTH)
        o_moba = _moba(pj, kmean, batch, seq)
        o_gla = _gla(pj, la, batch, seq)

        xt = _mix_ffn(xt, o_moba, o_gla, pj, moba_out_g[l][None],
                      gla_out_g[l].reshape(1, GLA_V_WIDTH), wo, mix_post_g[l][None],
                      ffn2_pre_g[l][None], wg2, wu2, wd2, ffn2_post_g[l][None])
    return xt.reshape(batch, seq, d)
```

```python
import functools

import jax
import jax.numpy as jnp
from jax import lax
from jax.experimental import pallas as pl
from jax.experimental.pallas import tpu as pltpu

D_MODEL = 1024
D_FF = 2816
RMS_EPS = 1e-6

MOBA_WIDTH = 512
MOBA_HEAD_DIM = 64
MOBA_BLOCK = 256
MOBA_TOPK = 3
MOBA_MIN_DENOM = 2.0 ** -100
MOBA_COLS = 2
LOG2_E = 1.4426950408889634
MOBA_Q_SCALE = MOBA_HEAD_DIM ** -0.5 * LOG2_E

GLA_HEADS = 4
GLA_KEY_DIM = 64
GLA_VALUE_DIM = 128
GLA_K_WIDTH = GLA_HEADS * GLA_KEY_DIM
GLA_V_WIDTH = GLA_HEADS * GLA_VALUE_DIM
GLA_GATE_RANK = 16
GLA_GATE_TAU = 16.0
GLA_TILE = 256
GLA_STEP = 1024

LANES = 128
BF16_SUBLANES = 16
GATE_PAD = LANES
NEG_BIG = -1e30
SUM_ROWS = 16

FFN_TM = 512
FFN_TF = 256
CAST_ROWS = 256
PROJ_MAIN = 3 * MOBA_WIDTH + 2 * GLA_K_WIDTH + GLA_V_WIDTH
PROJ_OUT = PROJ_MAIN + GLA_V_WIDTH
MIX_TM = 1024
MIX_SUB = 256
VMEM_LIMIT = 56 * 1024 * 1024

_NT = (((1,), (1,)), ((), ()))


def _rms(x, g):
    return x * lax.rsqrt(jnp.mean(x * x, axis=-1, keepdims=True) + RMS_EPS) * g


def _silu(x):
    return x / (1.0 + jnp.exp(-x))


def _resident(shape):
    return pl.BlockSpec(shape, lambda *_: (0,) * len(shape), pipeline_mode=pl.Buffered(1))


def _interleave(stage_lists, lag):
    n_slots = max(len(st) + j * lag for j, st in enumerate(stage_lists))
    for k in range(n_slots):
        for j, st in enumerate(stage_lists):
            if 0 <= k - j * lag < len(st):
                st[k - j * lag]()


def _cast_weights(jobs, stage_ref, sem):
    n_slot = stage_ref.shape[0]
    per_job = [[(src, store, row0, k, n_cols, rows) for k in range(n_rows // rows)]
               for src, store, n_rows, row0, n_cols, rows in jobs]
    chunks = [job[k] for k in range(max(map(len, per_job))) for job in per_job if k < len(job)]

    def copy(c):
        src, _, row0, k, n_cols, rows = chunks[c]
        slot = c % n_slot
        return pltpu.make_async_copy(
            src.at[pl.ds(row0 + k * rows, rows), pl.ds(0, n_cols)],
            stage_ref.at[slot, pl.ds(0, rows), pl.ds(0, n_cols)], sem.at[slot])

    for c in range(min(n_slot, len(chunks))):
        copy(c).start()
    for c, (_, store, _, k, n_cols, rows) in enumerate(chunks):
        copy(c).wait()
        store(slice(k * rows, (k + 1) * rows), stage_ref[c % n_slot, 0:rows, 0:n_cols])
        if c + n_slot < len(chunks):
            copy(c + n_slot).start()


def _cast_job(src_hbm, dst_ref):
    def store(rows, chunk):
        dst_ref[rows, :] = chunk.astype(jnp.bfloat16)

    return (src_hbm, store, dst_ref.shape[0], 0, dst_ref.shape[1], CAST_ROWS)


_HBM = pl.BlockSpec(memory_space=pl.ANY)


def _cast_scratch(n_slot):
    return [pltpu.VMEM((n_slot, CAST_ROWS, D_FF), jnp.float32), pltpu.SemaphoreType.DMA((n_slot,))]


def _row(width, rows=FFN_TM):
    return pl.BlockSpec((rows, width), lambda i: (i, 0))


_ROW_PARAMS = dict(
    compiler_params=pltpu.CompilerParams(
        dimension_semantics=("arbitrary",), vmem_limit_bytes=VMEM_LIMIT))


def _ffn_proj_stages(r, x_ref, pre_ref, post_ref, g_ref, wup_ref, ba_ref,
                     x1_ref, pj_ref, kmean_ref, la_ref,
                     h_ref, wg_ref, wu_ref, wd_ref, wmain_ref, wgr_ref, wat_ref):
    rows = slice(r * MOBA_BLOCK, (r + 1) * MOBA_BLOCK)
    st = {}

    def prenorm():
        st["xn"] = _rms(x_ref[rows, :], pre_ref[...]).astype(jnp.bfloat16)

    def up(c):
        def run():
            cols = slice(c * FFN_TF, (c + 1) * FFN_TF)
            g = jnp.dot(st["xn"], wg_ref[:, cols], preferred_element_type=jnp.float32)
            u = jnp.dot(st["xn"], wu_ref[:, cols], preferred_element_type=jnp.float32)
            h_ref[rows, cols] = (_silu(g) * u).astype(jnp.bfloat16)
        return run

    def down():
        st["f"] = jnp.dot(h_ref[rows, :], wd_ref[...], preferred_element_type=jnp.float32)

    def mid():
        x1 = x_ref[rows, :] + 0.5 * _rms(st.pop("f"), post_ref[...])
        x1_ref[rows, :] = x1
        st["h"] = _rms(x1, g_ref[...]).astype(jnp.bfloat16)

    def proj(w_ref, lo, hi):
        return jnp.dot(st["h"], w_ref[:, lo:hi], preferred_element_type=jnp.float32)

    w, kw, vw = MOBA_WIDTH, GLA_K_WIDTH, GLA_V_WIDTH

    def moba_q():
        pj_ref[rows, 0:w] = (proj(wmain_ref, 0, w) * MOBA_Q_SCALE).astype(jnp.bfloat16)

    def moba_k():
        mk = proj(wmain_ref, w, 2 * w)
        pj_ref[rows, w:2 * w] = mk.astype(jnp.bfloat16)
        kmean_ref[r] = jnp.mean(mk, axis=0, keepdims=True)

    def moba_v():
        pj_ref[rows, 2 * w:3 * w] = proj(wmain_ref, 2 * w, 3 * w).astype(jnp.bfloat16)

    def gla_qk():
        pj_ref[rows, 3 * w:3 * w + 2 * kw] = proj(wmain_ref, 3 * w, 3 * w + 2 * kw).astype(jnp.bfloat16)

    def gla_v():
        pj_ref[rows, 3 * w + 2 * kw:PROJ_MAIN] = proj(wmain_ref, 3 * w + 2 * kw, PROJ_MAIN).astype(jnp.bfloat16)

    def gla_gate():
        pj_ref[rows, PROJ_MAIN:PROJ_MAIN + vw] = proj(wgr_ref, 0, vw).astype(jnp.bfloat16)

    def gla_decay():
        ga = lax.dot_general(st["h"], wat_ref[...], _NT, preferred_element_type=jnp.float32)
        z = jnp.dot(ga.astype(jnp.bfloat16), wup_ref[...],
                    preferred_element_type=jnp.float32) + ba_ref[...]
        log_sig = jnp.minimum(z, 0.0) - jnp.log(1.0 + jnp.exp(-jnp.abs(z)))
        la_ref[rows, :] = log_sig * (LOG2_E / GLA_GATE_TAU)

    return ([prenorm] + [up(c) for c in range(D_FF // FFN_TF)] + [down, mid]
            + [moba_q, moba_k, moba_v, gla_qk, gla_v, gla_gate, gla_decay])


def _ffn_proj_kernel(x_ref, pre_ref, wg_hbm, wu_hbm, wd_hbm, post_ref, g_ref, wint_hbm, wup_ref, ba_ref,
                     *rest, n_next):
    next_in, rest = rest[:n_next], rest[n_next:]
    outs, next_out, scratch = rest[:4], rest[4:4 + n_next], rest[4 + n_next:]
    h_ref, wg_ref, wu_ref, wd_ref, wmain_ref, wgr_ref, wat_ref, stage_ref, sem = scratch

    @pl.when(pl.program_id(0) == 0)
    def _():
        def store_main(rows, chunk):
            wmain_ref[:, rows] = chunk.T.astype(jnp.bfloat16)

        def store_gate(rows, chunk):
            wgr_ref[:, rows] = chunk.T.astype(jnp.bfloat16)

        def store_rank(rows, chunk):
            wat_ref[rows, :] = chunk.astype(jnp.bfloat16)

        _cast_weights(
            [_cast_job(wg_hbm, wg_ref),
             (wint_hbm, store_main, PROJ_MAIN, 0, D_MODEL, CAST_ROWS),
             _cast_job(wu_hbm, wu_ref),
             (wint_hbm, store_gate, GLA_V_WIDTH, PROJ_MAIN + GLA_GATE_RANK, D_MODEL, CAST_ROWS),
             _cast_job(wd_hbm, wd_ref),
             (wint_hbm, store_rank, GATE_PAD, PROJ_MAIN, D_MODEL, GATE_PAD)], stage_ref, sem)

    refs = (x_ref, pre_ref, post_ref, g_ref, wup_ref, ba_ref) + outs + (
        h_ref, wg_ref, wu_ref, wd_ref, wmain_ref, wgr_ref, wat_ref)
    _interleave([_ffn_proj_stages(r, *refs) for r in range(FFN_TM // MOBA_BLOCK)], lag=4)
    for src_ref, dst_ref in zip(next_in, next_out):
        dst_ref[...] = src_ref[...].astype(jnp.bfloat16)


def _ffn_proj(x, pre_g, wg, wu, wd, post_g, g, w_in, wup, ba, next_weights):
    t = x.shape[0]
    n_step = t // FFN_TM
    nblk = FFN_TM // MOBA_BLOCK

    def out(width, dtype=jnp.bfloat16):
        return jax.ShapeDtypeStruct((t, width), dtype)

    def slice_spec(w):
        rows = w.shape[0] // n_step
        rep = 1 if rows % BF16_SUBLANES == 0 else 2
        assert (rows * rep) % BF16_SUBLANES == 0 and w.shape[0] % n_step == 0
        return pl.BlockSpec((rows * rep, w.shape[1]), lambda i: (i // rep, 0))

    next_specs = [slice_spec(w) for w in next_weights]
    return pl.pallas_call(
        functools.partial(_ffn_proj_kernel, n_next=len(next_weights)),
        grid=(n_step,),
        in_specs=[_row(D_MODEL), _resident(pre_g.shape), _HBM, _HBM, _HBM, _resident(post_g.shape),
                  _resident(g.shape), _HBM, _resident(wup.shape), _resident(ba.shape)] + next_specs,
        out_specs=[_row(D_MODEL), _row(PROJ_OUT),
                   pl.BlockSpec((nblk, 1, MOBA_WIDTH), lambda i: (i, 0, 0)), _row(GLA_K_WIDTH)] + next_specs,
        out_shape=[out(D_MODEL, jnp.float32), out(PROJ_OUT),
                   jax.ShapeDtypeStruct((t // MOBA_BLOCK, 1, MOBA_WIDTH), jnp.float32),
                   out(GLA_K_WIDTH, jnp.float32)]
        + [jax.ShapeDtypeStruct(w.shape, jnp.bfloat16) for w in next_weights],
        scratch_shapes=[pltpu.VMEM((FFN_TM, D_FF), jnp.bfloat16),
                        pltpu.VMEM((D_MODEL, D_FF), jnp.bfloat16), pltpu.VMEM((D_MODEL, D_FF), jnp.bfloat16),
                        pltpu.VMEM((D_FF, D_MODEL), jnp.bfloat16),
                        pltpu.VMEM((D_MODEL, PROJ_MAIN), jnp.bfloat16),
                        pltpu.VMEM((D_MODEL, GLA_V_WIDTH), jnp.bfloat16),
                        pltpu.VMEM((GATE_PAD, D_MODEL), jnp.bfloat16)] + _cast_scratch(3),
        name="ffn_proj", **_ROW_PARAMS,
    )(x, pre_g, wg, wu, wd, post_g, g, w_in, wup, ba, *next_weights)


def _mix_ffn_stages(rows, x_ref, om_ref, og_ref, gr_ref, mg_ref, gg_ref, mpost_ref, pre_ref, post_ref,
                    o_ref, h_ref, wo_ref, wg_ref, wu_ref, wd_ref):
    st = {}

    def mix_in():
        om = _rms(om_ref[rows, :].astype(jnp.float32), mg_ref[...]).astype(jnp.bfloat16)
        gate = _silu(gr_ref[rows, :].astype(jnp.float32))
        og = og_ref[rows, :].astype(jnp.float32)
        parts = [om]
        for h in range(GLA_HEADS):
            cols = slice(h * GLA_VALUE_DIM, (h + 1) * GLA_VALUE_DIM)
            parts.append((_rms(og[:, cols], gg_ref[:, cols]) * gate[:, cols]).astype(jnp.bfloat16))
        st["mix_in"] = jnp.concatenate(parts, axis=-1)

    def mix_out():
        st["mix"] = jnp.dot(st.pop("mix_in"), wo_ref[...], preferred_element_type=jnp.float32)

    def norms():
        x2 = x_ref[rows, :] + _rms(st.pop("mix"), mpost_ref[...])
        st["x2"] = x2
        st["xn"] = _rms(x2, pre_ref[...]).astype(jnp.bfloat16)

    def up(c):
        def run():
            cols = slice(c * FFN_TF, (c + 1) * FFN_TF)
            g = jnp.dot(st["xn"], wg_ref[:, cols], preferred_element_type=jnp.float32)
            u = jnp.dot(st["xn"], wu_ref[:, cols], preferred_element_type=jnp.float32)
            h_ref[rows, cols] = (_silu(g) * u).astype(jnp.bfloat16)
        return run

    def down():
        st["f"] = jnp.dot(h_ref[rows, :], wd_ref[...], preferred_element_type=jnp.float32)

    def finish():
        o_ref[rows, :] = st.pop("x2") + 0.5 * _rms(st.pop("f"), post_ref[...])

    return [mix_in, mix_out, norms] + [up(c) for c in range(D_FF // FFN_TF)] + [down, finish]


def _mix_ffn_kernel(x_ref, om_ref, og_ref, gr_ref, mg_ref, gg_ref, wo_ref, mpost_ref,
                    pre_ref, wg_ref, wu_ref, wd_ref, post_ref, o_ref, h_ref):
    refs = (x_ref, om_ref, og_ref, gr_ref, mg_ref, gg_ref, mpost_ref, pre_ref, post_ref,
            o_ref, h_ref, wo_ref, wg_ref, wu_ref, wd_ref)
    _interleave([_mix_ffn_stages(slice(r * MIX_SUB, (r + 1) * MIX_SUB), *refs)
                 for r in range(MIX_TM // MIX_SUB)], lag=4)


def _mix_ffn(x, om, og, pj, mg, gg, wo, mpost_g, pre_g, wg, wu, wd, post_g):
    t = x.shape[0]
    params = (mg, gg, wo, mpost_g, pre_g, wg, wu, wd, post_g)
    return pl.pallas_call(
        _mix_ffn_kernel,
        grid=(t // MIX_TM,),
        in_specs=[_row(D_MODEL, MIX_TM), _row(MOBA_WIDTH, MIX_TM), _row(GLA_V_WIDTH, MIX_TM),
                  pl.BlockSpec((MIX_TM, GLA_V_WIDTH), lambda i: (i, PROJ_MAIN // GLA_V_WIDTH))]
        + [_resident(p.shape) for p in params],
        out_specs=_row(D_MODEL, MIX_TM),
        out_shape=jax.ShapeDtypeStruct((t, D_MODEL), jnp.float32),
        scratch_shapes=[pltpu.VMEM((MIX_TM, D_FF), jnp.bfloat16)],
        name="mix_ffn", **_ROW_PARAMS,
    )(x, om, og, pj, *params)


def _moba_kernel(q_ref, k_ref, v_ref, kmean_ref, o_ref, kaug_ref, *, n_blk):
    blk = MOBA_BLOCK
    dh = MOBA_HEAD_DIM
    seq = n_blk * blk
    n_head = LANES // dh
    wide = n_head * blk

    key_blk = lax.broadcasted_iota(jnp.int32, (seq, LANES), 0) // blk
    hot_col = lax.broadcasted_iota(jnp.int32, (seq, LANES), 1)
    one_hot = jnp.where(key_blk == hot_col, 1.0, 0.0).astype(jnp.bfloat16)
    feat = lax.broadcasted_iota(jnp.int32, (LANES, blk), 0)
    head_rows = [(feat >= h * dh) & (feat < (h + 1) * dh) for h in range(n_head)]
    blk_row = lax.broadcasted_iota(jnp.int32, (n_blk, wide), 0)
    key_i = lax.broadcasted_iota(jnp.int32, (blk, wide), 0)
    qry_i = lax.broadcasted_iota(jnp.int32, (blk, wide), 1) % blk
    causal = key_i <= qry_i
    bias_pad = jnp.zeros((LANES - n_blk, wide), jnp.float32)
    ones = jnp.ones((SUM_ROWS, blk), jnp.float32)

    f_i = lax.broadcasted_iota(jnp.int32, (LANES, LANES), 0) // dh
    g_i = lax.broadcasted_iota(jnp.int32, (LANES, LANES), 1) // dh
    same_head = jnp.where(f_i == g_i, 1.0, 0.0).astype(jnp.bfloat16)
    denominators = []

    def column(c, exact):
        lanes = slice(c * LANES, (c + 1) * LANES)
        kmean = kmean_ref[0][:, lanes]
        km_hi = kmean.astype(jnp.bfloat16)
        km_lo = (kmean - km_hi.astype(jnp.float32)).astype(jnp.bfloat16)
        v_t = [None] * n_blk
        if not exact:
            kaug_ref[c, :, 0:LANES] = k_ref[:, lanes]
            kaug_ref[c, :, LANES:2 * LANES] = one_hot
            kf = k_ref[:, lanes].astype(jnp.float32)
            k2 = jnp.dot((kf * kf).astype(jnp.bfloat16), same_head, preferred_element_type=jnp.float32)
            kmax2 = [jnp.max(k2[j * blk:(j + 1) * blk, :], axis=0, keepdims=True) for j in range(n_blk)]
            for j in range(1, n_blk):
                kmax2[j] = jnp.maximum(kmax2[j], kmax2[j - 1])

        def value_block(j):
            if v_t[j] is None:
                v_j = v_ref[j * blk:(j + 1) * blk, lanes].astype(jnp.float32).T
                v_t[j] = jnp.concatenate([v_j, ones], axis=0).astype(jnp.bfloat16)
            return v_t[j]

        def section(i):
            st = {}
            n_keys = (i + 1) * blk

            def prep():
                q_i = q_ref[i * blk:(i + 1) * blk, lanes].astype(jnp.float32).T
                qi_f = jnp.concatenate([jnp.where(rows, q_i, 0.0) for rows in head_rows], axis=1)
                qi_t = qi_f.astype(jnp.bfloat16)
                if not exact:
                    qn2 = jnp.sum(qi_f * qi_f, axis=0, keepdims=True)
                    kb = jnp.concatenate([jnp.broadcast_to(kmax2[i][:, h * dh:h * dh + 1], (1, blk))
                                          for h in range(n_head)], axis=1)
                    st["shift"] = jnp.sqrt(qn2 * kb)
                if i > MOBA_TOPK:
                    gate = (jnp.dot(km_hi, qi_t, preferred_element_type=jnp.float32)
                            + jnp.dot(km_lo, qi_t, preferred_element_type=jnp.float32))
                    rank = jnp.zeros((n_blk, wide), jnp.float32)
                    for j in range(i):
                        gj = gate[j:j + 1, :]
                        beats = (gj > gate) | ((gj == gate) & (blk_row > j))
                        rank = rank + jnp.where(beats, 1.0, 0.0)
                    keep = (rank < float(MOBA_TOPK)) | (blk_row >= i)
                    bias = jnp.where(keep, 0.0, NEG_BIG)
                    st["q"] = jnp.concatenate([qi_f, bias, bias_pad], axis=0).astype(jnp.bfloat16)
                else:
                    st["q"] = qi_t

            def scores():
                if i > MOBA_TOPK:
                    s = jnp.dot(kaug_ref[c, 0:n_keys, :], st.pop("q"),
                                preferred_element_type=jnp.float32)
                else:
                    s = jnp.dot(k_ref[0:n_keys, lanes], st.pop("q"), preferred_element_type=jnp.float32)
                s_own = jnp.where(causal, s[i * blk:n_keys, :], NEG_BIG)
                sm = s_own if i == 0 else jnp.concatenate([s[0:i * blk, :], s_own], axis=0)
                if exact:
                    st["s"] = sm
                    st["shift"] = jnp.max(sm, axis=0, keepdims=True)
                else:
                    st["p"] = jnp.exp2(sm - st.pop("shift")).astype(jnp.bfloat16)

            def probs():
                if exact:
                    st["p"] = jnp.exp2(st.pop("s") - st.pop("shift")).astype(jnp.bfloat16)

            def values():
                v_i = jnp.concatenate([value_block(j) for j in range(i + 1)], axis=1)
                o_t = jnp.dot(v_i, st.pop("p"), preferred_element_type=jnp.float32)
                denom = o_t[LANES:LANES + 1, :]
                if not exact:
                    denominators.append(denom)
                o_t = o_t[0:LANES, :] * (1.0 / denom)
                o_own = jnp.concatenate([o_t[h * dh:(h + 1) * dh, h * blk:(h + 1) * blk]
                                         for h in range(n_head)], axis=0)
                o_ref[i * blk:(i + 1) * blk, lanes] = o_own.T.astype(o_ref.dtype)

            return [prep, scores, probs, values]

        return [section(j) + section(n_blk - 1 - j) for j in range(n_blk // 2)]

    _interleave([stream for c in range(MOBA_COLS) for stream in column(c, exact=False)], lag=1)

    smallest = denominators[0]
    for denom in denominators[1:]:
        smallest = jnp.minimum(smallest, denom)

    @pl.when(jnp.logical_not(jnp.min(smallest) >= MOBA_MIN_DENOM))
    def _():
        for c in range(MOBA_COLS):
            for stream in column(c, exact=True):
                for stage in stream:
                    stage()


def _moba(pj, kmean, batch, seq):
    n_blk = seq // MOBA_BLOCK
    width = MOBA_COLS * LANES
    n_col = MOBA_WIDTH // width

    def tok(group):
        return pl.BlockSpec((seq, width), lambda b, c: (b, group * n_col + c))

    return pl.pallas_call(
        functools.partial(_moba_kernel, n_blk=n_blk),
        grid=(batch, n_col),
        in_specs=[tok(0), tok(1), tok(2), pl.BlockSpec((1, n_blk, width), lambda b, c: (b, 0, c))],
        out_specs=tok(0),
        out_shape=jax.ShapeDtypeStruct((batch * seq, MOBA_WIDTH), jnp.bfloat16),
        scratch_shapes=[pltpu.VMEM((MOBA_COLS, seq, 2 * LANES), jnp.bfloat16)],
        compiler_params=pltpu.CompilerParams(
            dimension_semantics=("arbitrary", "arbitrary"), vmem_limit_bytes=VMEM_LIMIT),
        name="moba",
    )(pj, pj, pj, kmean)


def _gla_kernel(q_ref, k_ref, v_ref, la_ref, o_ref, st_ref):
    @pl.when(pl.program_id(1) == 0)
    def _():
        st_ref[...] = jnp.zeros_like(st_ref)

    for r in range(GLA_STEP // GLA_TILE):
        rows = slice(r * GLA_TILE, (r + 1) * GLA_TILE)
        _gla_tile(q_ref[rows, :], k_ref[rows, :], v_ref.at[rows, :], la_ref[rows, :],
                  o_ref.at[rows, :], st_ref)


def _gla_tile(q, k, v_ref, la, o_ref, st_ref):
    c = GLA_TILE
    dk, dv = GLA_KEY_DIM, GLA_VALUE_DIM
    row = lax.broadcasted_iota(jnp.int32, la.shape, 0)
    lane = lax.broadcasted_iota(jnp.int32, la.shape, 1)
    odd_head = (lane & dk) != 0
    st_lane = lax.broadcasted_iota(jnp.int32, (dv, GLA_K_WIDTH), 1)

    b = la
    step = 1
    while step < c:
        b = b + jnp.where(row >= step, pltpu.roll(b, step, 0), 0.0)
        step *= 2

    qf = q.astype(jnp.float32) * (dk ** -0.5)
    kf = k.astype(jnp.float32)
    qb, kb = qf.astype(jnp.bfloat16), k

    sub = c // 2
    t_i = lax.broadcasted_iota(jnp.int32, (sub, sub), 0)
    s_i = lax.broadcasted_iota(jnp.int32, (sub, sub), 1)
    level = jnp.where(t_i >= s_i, 31 - lax.clz(t_i ^ s_i), -2)

    def split_heads(x):
        x = x.astype(jnp.bfloat16)
        zero = jnp.zeros_like(x)
        return jnp.where(odd_head, zero, x), jnp.where(odd_head, x, zero)

    def level_products(q_lvl, k_lvl, row_blk, col_blk):
        q_b = q_lvl.astype(jnp.bfloat16)
        k_even, k_odd = split_heads(k_lvl)
        rows = slice(row_blk * sub, (row_blk + 1) * sub)
        keys = slice(col_blk * sub, (col_blk + 1) * sub)
        out = []
        for pair in range(GLA_HEADS // 2):
            cols = slice(pair * LANES, (pair + 1) * LANES)
            k_pair = jnp.concatenate([k_even[keys, cols], k_odd[keys, cols]], axis=0)
            a = lax.dot_general(q_b[rows, cols], k_pair, _NT, preferred_element_type=jnp.float32)
            out += [a[:, 0:sub], a[:, sub:2 * sub]]
        return out

    def add_level(acc, q_lvl, k_lvl, lvl):
        hit = level == lvl
        for r in range(2):
            for h, a in enumerate(level_products(q_lvl, k_lvl, r, r)):
                acc[r][h] = jnp.where(hit, a, acc[r][h])
        return acc

    acc = [[jnp.zeros((sub, sub), jnp.float32) for _ in range(GLA_HEADS)] for _ in range(2)]
    acc = add_level(acc, qf, kf, -1)
    a_cross = None
    b_end = b
    half = 1
    lvl = 0
    while half < c:
        b_before = pltpu.roll(b_end, half, 0)
        q_lvl = qb * jnp.exp2(b - b_before).astype(jnp.bfloat16)
        k_lvl = kb * jnp.exp2(b_end - b).astype(jnp.bfloat16)
        if half < sub:
            acc = add_level(acc, q_lvl, k_lvl, lvl)
        else:
            a_cross = level_products(q_lvl, k_lvl, 1, 0)
        upper = (row & half) != 0
        b_end = jnp.where(upper, b_end, pltpu.roll(b_end, c - half, 0))
        half *= 2
        lvl += 1
    q_in = (qf * jnp.exp2(b)).astype(jnp.bfloat16)
    k_out = (kf * jnp.exp2(b_end - b)).astype(jnp.bfloat16)
    tile_decay = jnp.exp2(b_end[0:1, :])

    for h in range(GLA_HEADS):
        cols = slice((h // 2) * LANES, (h // 2 + 1) * LANES)
        v_h = v_ref[:, h * dv:(h + 1) * dv]
        st = st_ref[h]
        o_inter = lax.dot_general(q_in[:, cols], st[:, cols].astype(jnp.bfloat16), _NT,
                                  preferred_element_type=jnp.float32)
        a_lo = acc[0][h].astype(jnp.bfloat16)
        a_hi = jnp.concatenate([a_cross[h], acc[1][h]], axis=1).astype(jnp.bfloat16)
        o_intra = jnp.concatenate(
            [jnp.dot(a_lo, v_h[0:sub, :], preferred_element_type=jnp.float32),
             jnp.dot(a_hi, v_h, preferred_element_type=jnp.float32)], axis=0)
        o_ref[:, h * dv:(h + 1) * dv] = (o_inter + o_intra).astype(o_ref.dtype)
        v_t = v_h.astype(jnp.float32).T.astype(jnp.bfloat16)
        upd = jnp.dot(v_t, k_out, preferred_element_type=jnp.float32)
        own = (st_lane >= h * dk) & (st_lane < (h + 1) * dk)
        st_ref[h] = st * tile_decay + jnp.where(own, upd, 0.0)


def _gla(pj, la, batch, seq):
    n_tile = seq // GLA_STEP

    def tok(width, col0=0):
        return pl.BlockSpec((GLA_STEP, width), lambda b, t: (b * n_tile + t, col0 // width))

    return pl.pallas_call(
        _gla_kernel,
        grid=(batch, n_tile),
        in_specs=[tok(GLA_K_WIDTH, 3 * MOBA_WIDTH), tok(GLA_K_WIDTH, 3 * MOBA_WIDTH + GLA_K_WIDTH),
                  tok(GLA_V_WIDTH, 3 * MOBA_WIDTH + 2 * GLA_K_WIDTH), tok(GLA_K_WIDTH)],
        out_specs=tok(GLA_V_WIDTH),
        out_shape=jax.ShapeDtypeStruct((batch * seq, GLA_V_WIDTH), jnp.bfloat16),
        scratch_shapes=[pltpu.VMEM((GLA_HEADS, GLA_VALUE_DIM, GLA_K_WIDTH), jnp.float32)],
        compiler_params=pltpu.CompilerParams(
            dimension_semantics=("arbitrary", "arbitrary"), vmem_limit_bytes=VMEM_LIMIT),
        name="gla",
    )(pj, pj, pj, la)


def kernel(x, ffn1_pre_g, ffn1_w_gate, ffn1_w_up, ffn1_w_down, ffn1_post_g, mix_pre_g, w_in,
           gla_w_alpha_up, gla_b_alpha, moba_out_g, gla_out_g, w_out, mix_post_g, ffn2_pre_g,
           ffn2_w_gate, ffn2_w_up, ffn2_w_down, ffn2_post_g):
    batch, seq, d = x.shape
    assert d == D_MODEL and seq % MOBA_BLOCK == 0 and seq % GLA_STEP == 0
    assert (batch * seq) % FFN_TM == 0 and (batch * seq) % MIX_TM == 0
    bf = jnp.bfloat16
    xt = x.reshape(batch * seq, d)
    for l in range(w_in.shape[0]):
        wup = jnp.pad(gla_w_alpha_up[l], ((0, GATE_PAD - GLA_GATE_RANK), (0, 0))).astype(bf)
        xt, pj, kmean, la, wo, wg2, wu2, wd2 = _ffn_proj(
            xt, ffn1_pre_g[l][None], ffn1_w_gate[l], ffn1_w_up[l], ffn1_w_down[l], ffn1_post_g[l][None],
            mix_pre_g[l][None], w_in[l].T, wup, gla_b_alpha[l][None],
            (w_out[l], ffn2_w_gate[l], ffn2_w_up[l], ffn2_w_down[l]))

        kmean = kmean.reshape(batch, seq // MOBA_BLOCK, MOBA_WIDTH)
        o_moba = _moba(pj, kmean, batch, seq)
        o_gla = _gla(pj, la, batch, seq)

        xt = _mix_ffn(xt, o_moba, o_gla, pj, moba_out_g[l][None],
                      gla_out_g[l].reshape(1, GLA_V_WIDTH), wo, mix_post_g[l][None],
                      ffn2_pre_g[l][None], wg2, wu2, wd2, ffn2_post_g[l][None])
    return xt.reshape(batch, seq, d)
```
